```python
import jax, jax.numpy as jnp
from jax import lax
import numpy as np

D_MODEL = 1024
BATCH = 8
SEQ = 2048
DEPTH = 1

N_MEM = 256
D_MIX = D_MODEL
D_HGRN = D_MIX // 2
D_POOL = D_MIX - D_HGRN
HGRN_HEADS = 4
HGRN_HEAD_DIM = D_HGRN // HGRN_HEADS
CHUNK = 64
POOL_WINDOWS = (2, 4, 8, 16)
POOL_GROUPS = len(POOL_WINDOWS)
POOL_GROUP_DIM = D_POOL // POOL_GROUPS
D_IN_PROJ = 4 * D_HGRN + D_POOL
D_FF = ((8 * D_MODEL // 3 + 255) // 256) * 256
XA_HEADS = 4
XA_HEAD_DIM = D_MODEL // XA_HEADS
ALPHA = (2.0 * DEPTH) ** 0.25
BETA = (8.0 * DEPTH) ** -0.25
LN_EPS = 1e-5
RMS_EPS = 1e-6

kernel_name = "hymba_hgrn2_pool_macaron_deepnorm"


def _layernorm(x, g, b):
    xf = x.astype(jnp.float32)
    mu = jnp.mean(xf, axis=-1, keepdims=True)
    var = jnp.mean(jnp.square(xf - mu), axis=-1, keepdims=True)
    return ((xf - mu) * lax.rsqrt(var + LN_EPS) * g.astype(jnp.float32) + b.astype(jnp.float32)).astype(x.dtype)


def _swiglu(h, w_in, w_out):
    gate, up = jnp.split(h @ w_in, 2, axis=-1)
    return (jax.nn.silu(gate) * up) @ w_out


def _hgrn2_chunk_scan(q, k, v, log_f):
    b_, s_, h_, dk = q.shape
    dv = v.shape[-1]
    nc = s_ // CHUNK

    def to_chunks(t):
        return t.reshape(b_, nc, CHUNK, h_, t.shape[-1]).transpose(1, 0, 3, 2, 4)

    qc, kc, vc, gc = to_chunks(q), to_chunks(k), to_chunks(v), to_chunks(log_f)
    causal = jnp.tril(jnp.ones((CHUNK, CHUNK), dtype=bool))[:, :, None]

    def step(state, inp):
        qi, ki, vi, gi = inp
        cum = jnp.cumsum(gi, axis=2)
        diff = cum[:, :, :, None, :] - cum[:, :, None, :, :]
        decay = jnp.exp(jnp.where(causal, diff, -jnp.inf))
        scores = jnp.einsum('bhtk,bhsk,bhtsk->bhts', qi, ki, decay)
        o = (jnp.einsum('bhts,bhsv->bhtv', scores, vi)
             + jnp.einsum('bhtk,bhkv->bhtv', qi * jnp.exp(cum), state))
        last = cum[:, :, -1:, :]
        state = (jnp.exp(last[:, :, 0, :])[..., None] * state
                 + jnp.einsum('bhsk,bhsv->bhkv', ki * jnp.exp(last - cum), vi))
        return state, o

    init = jnp.zeros((b_, h_, dk, dv), jnp.float32)
    _, oc = lax.scan(step, init, (qc, kc, vc, gc))
    return oc.transpose(1, 0, 3, 2, 4).reshape(b_, s_, h_, dv)


def _causal_multiscale_pool(v, pool_w, pool_scale):
    b_, s_, _ = v.shape
    vf = v.astype(jnp.float32)
    cs = jnp.cumsum(vf, axis=1)
    cs_pad = jnp.concatenate([jnp.zeros((b_, 1, D_POOL), jnp.float32), cs], axis=1)
    pos = jnp.arange(1, s_ + 1, dtype=jnp.int32)
    outs = []
    for g, w in enumerate(POOL_WINDOWS):
        sl = slice(g * POOL_GROUP_DIM, (g + 1) * POOL_GROUP_DIM)
        upper = cs_pad[:, 1:, sl]
        lower = jnp.concatenate([jnp.zeros((b_, w - 1, POOL_GROUP_DIM), jnp.float32),
                                 cs_pad[:, :s_ - w + 1, sl]], axis=1)
        count = jnp.minimum(pos, w).astype(jnp.float32)[None, :, None]
        outs.append((upper - lower) / count - vf[:, :, sl])
    pooled = jnp.stack(outs, axis=2)
    mixed = jnp.einsum('bsgc,gcd->bsgd', pooled, pool_w.astype(jnp.float32))
    return mixed.reshape(b_, s_, D_POOL) * pool_scale.astype(jnp.float32)


def _parallel_mixer(h, w_in, lb, gnorm, pool_w, pool_scale, w_out):
    b_, s_, _ = h.shape
    proj = h @ w_in
    q, f, i, g, v = jnp.split(proj, [D_HGRN, 2 * D_HGRN, 3 * D_HGRN, 4 * D_HGRN], axis=-1)
    q = jax.nn.silu(q.astype(jnp.float32))
    forget = lb + (1.0 - lb) * jax.nn.sigmoid(f.astype(jnp.float32))
    key = 1.0 - forget
    log_f = jnp.log(forget)
    heads = lambda t: t.reshape(b_, s_, HGRN_HEADS, HGRN_HEAD_DIM)
    o = _hgrn2_chunk_scan(heads(q), heads(key), heads(i.astype(jnp.float32)), heads(log_f))
    o = o * lax.rsqrt(jnp.mean(jnp.square(o), axis=-1, keepdims=True) + RMS_EPS) * gnorm.astype(jnp.float32)
    o = o.reshape(b_, s_, D_HGRN) * jax.nn.silu(g.astype(jnp.float32))
    p = _causal_multiscale_pool(v, pool_w, pool_scale)
    merged = jnp.concatenate([o, p], axis=-1).astype(h.dtype)
    return merged @ w_out


def _memory_attention(h, mem, wq, wk, wv, wo):
    b_, s_, _ = h.shape
    m_ = mem.shape[1]
    q = (h @ wq).reshape(b_, s_, XA_HEADS, XA_HEAD_DIM)
    k = (mem @ wk).reshape(b_, m_, XA_HEADS, XA_HEAD_DIM)
    v = (mem @ wv).reshape(b_, m_, XA_HEADS, XA_HEAD_DIM)
    s = jnp.einsum('bshd,bmhd->bhsm', q, k).astype(jnp.float32) * (XA_HEAD_DIM ** -0.5)
    p = jax.nn.softmax(s, axis=-1).astype(v.dtype)
    o = jnp.einsum('bhsm,bmhd->bshd', p, v).reshape(b_, s_, D_MODEL)
    return o @ wo


def setup_inputs(seed: int = 0) -> dict:
    key = jax.random.key(seed)
    ks = jax.random.split(key, 26)
    L = DEPTH
    nrm = lambda k, shape, scale: jax.random.normal(k, shape, jnp.float32) * scale
    return {
        "x": nrm(ks[0], (BATCH, SEQ, D_MODEL), 1.0),
        "mem": nrm(ks[1], (BATCH, N_MEM, D_MODEL), 1.0),
        "w_ffn1_in": nrm(ks[2], (L, D_MODEL, 2 * D_FF), D_MODEL ** -0.5),
        "w_ffn1_out": nrm(ks[3], (L, D_FF, D_MODEL), BETA * D_FF ** -0.5),
        "ln1_g": 1.0 + nrm(ks[4], (L, D_MODEL), 0.02),
        "ln1_b": nrm(ks[5], (L, D_MODEL), 0.02),
        "w_mix_in": nrm(ks[6], (L, D_MODEL, D_IN_PROJ), D_MODEL ** -0.5),
        "hgrn_lb": nrm(ks[7], (L + 1, D_HGRN), 0.5),
        "hgrn_gnorm": 1.0 + nrm(ks[8], (L, HGRN_HEAD_DIM), 0.02),
        "pool_w": nrm(ks[9], (L, POOL_GROUPS, POOL_GROUP_DIM, POOL_GROUP_DIM), POOL_GROUP_DIM ** -0.5),
        "pool_scale": 1.0 + nrm(ks[10], (L, D_POOL), 0.1),
        "w_mix_out": nrm(ks[11], (L, D_MIX, D_MODEL), BETA * D_MIX ** -0.5),
        "ln2_g": 1.0 + nrm(ks[12], (L, D_MODEL), 0.02),
        "ln2_b": nrm(ks[13], (L, D_MODEL), 0.02),
        "xa_wq": nrm(ks[14], (L, D_MODEL, D_MODEL), D_MODEL ** -0.5),
        "xa_wk": nrm(ks[15], (L, D_MODEL, D_MODEL), D_MODEL ** -0.5),
        "xa_wv": nrm(ks[16], (L, D_MODEL, D_MODEL), BETA * D_MODEL ** -0.5),
        "xa_wo": nrm(ks[17], (L, D_MODEL, D_MODEL), BETA * D_MODEL ** -0.5),
        "ln3_g": 1.0 + nrm(ks[18], (L, D_MODEL), 0.02),
        "ln3_b": nrm(ks[19], (L, D_MODEL), 0.02),
        "w_ffn2_in": nrm(ks[20], (L, D_MODEL, 2 * D_FF), D_MODEL ** -0.5),
        "w_ffn2_out": nrm(ks[21], (L, D_FF, D_MODEL), BETA * D_FF ** -0.5),
        "ln4_g": 1.0 + nrm(ks[22], (L, D_MODEL), 0.02),
        "ln4_b": nrm(ks[23], (L, D_MODEL), 0.02),
    }


def reference(x, mem, w_ffn1_in, w_ffn1_out, ln1_g, ln1_b, w_mix_in, hgrn_lb, hgrn_gnorm,
              pool_w, pool_scale, w_mix_out, ln2_g, ln2_b, xa_wq, xa_wk, xa_wv, xa_wo,
              ln3_g, ln3_b, w_ffn2_in, w_ffn2_out, ln4_g, ln4_b):
    lower_bounds = jnp.cumsum(jax.nn.softmax(hgrn_lb.astype(jnp.float32), axis=0), axis=0)
    h = x
    for l in range(DEPTH):
        h = _layernorm(ALPHA * h + 0.5 * _swiglu(h, w_ffn1_in[l], w_ffn1_out[l]), ln1_g[l], ln1_b[l])
        h = _layernorm(ALPHA * h + _parallel_mixer(h, w_mix_in[l], lower_bounds[l], hgrn_gnorm[l],
                                                  pool_w[l], pool_scale[l], w_mix_out[l]),
                       ln2_g[l], ln2_b[l])
        h = _layernorm(ALPHA * h + _memory_attention(h, mem, xa_wq[l], xa_wk[l], xa_wv[l], xa_wo[l]),
                       ln3_g[l], ln3_b[l])
        h = _layernorm(ALPHA * h + 0.5 * _swiglu(h, w_ffn2_in[l], w_ffn2_out[l]), ln4_g[l], ln4_b[l])
    return h
```

```python
import functools

import numpy as np
import jax
import jax.numpy as jnp
from jax import lax
from jax.experimental import pallas as pl
from jax.experimental.pallas import tpu as pltpu

F32 = jnp.float32
BF16 = jnp.bfloat16

D_MODEL = 1024
DEPTH = 1
D_HGRN = 512
D_POOL = 512
HGRN_HEADS = 4
HEAD_DIM = 128
POOL_WINDOWS = (2, 4, 8, 16)
POOL_GROUP_DIM = 128
D_FF = 2816
XA_HEADS = 4
XA_HEAD_DIM = 256
ALPHA = (2.0 * DEPTH) ** 0.25
LN_EPS = 1e-5
RMS_EPS = 1e-6

V7X_SUBLANES = 8
V7X_MXU_DIM = 256
V7X_VMEM_LIMIT = 56 * 1024 * 1024

CHUNK = 64
SUB = V7X_SUBLANES
POOL_HALO = 16
FFN_TILE = 512
MIX_TILE = 512
XA_TILE = 512
FF_CHUNKS = ((0, 1024), (1024, 1024), (2048, 768))


def _dot(a, b):
    return jnp.dot(a, b, preferred_element_type=F32)


def _dot_nt(a, b):
    return lax.dot_general(a, b, (((1,), (1,)), ((), ())), preferred_element_type=F32)


def _dot_tn(a, b):
    return lax.dot_general(a, b, (((0,), (0,)), ((), ())), preferred_element_type=F32)


def _silu(x):
    return x * jax.nn.sigmoid(x)


def _layernorm(y, g, b):
    mu = jnp.mean(y, axis=-1, keepdims=True)
    d = y - mu
    var = jnp.mean(d * d, axis=-1, keepdims=True)
    return d * lax.rsqrt(var + LN_EPS) * g + b


def _resident(shape):
    zeros = (0,) * len(shape)
    return pl.BlockSpec(shape, lambda *_: zeros, pipeline_mode=pl.Buffered(1))


def _ffn_ln_kernel(x_ref, win_ref, wout_ref, g_ref, b_ref, o_ref):
    x = x_ref[...]
    xb = x.astype(BF16)
    acc = None
    for c0, cw in FF_CHUNKS:
        gate = _dot(xb, win_ref[:, c0:c0 + cw])
        up = _dot(xb, win_ref[:, D_FF + c0:D_FF + c0 + cw])
        act = (_silu(gate) * up).astype(BF16)
        part = _dot(act, wout_ref[c0:c0 + cw, :])
        acc = part if acc is None else acc + part
    o_ref[...] = _layernorm(ALPHA * x + 0.5 * acc, g_ref[...], b_ref[...])


def _ffn_ln(x2d, w_in, w_out, g, b):
    m = x2d.shape[0]
    assert m % FFN_TILE == 0
    return pl.pallas_call(
        _ffn_ln_kernel,
        grid=(m // FFN_TILE,),
        in_specs=[
            pl.BlockSpec((FFN_TILE, D_MODEL), lambda i: (i, 0)),
            _resident((D_MODEL, 2 * D_FF)),
            _resident((D_FF, D_MODEL)),
            _resident((1, D_MODEL)),
            _resident((1, D_MODEL)),
        ],
        out_specs=pl.BlockSpec((FFN_TILE, D_MODEL), lambda i: (i, 0)),
        out_shape=jax.ShapeDtypeStruct((m, D_MODEL), F32),
        compiler_params=pltpu.CompilerParams(
            dimension_semantics=("arbitrary",), vmem_limit_bytes=V7X_VMEM_LIMIT),
        name="ffn_ln",
    )(x2d, w_in, w_out, g, b)


def _chunk_constants():
    t = np.arange(CHUNK)[:, None]
    s = np.arange(CHUNK)[None, :]
    tri = (s <= t)
    before = (s < (t // SUB) * SUB)
    total = np.ones((SUB, CHUNK), bool)
    summat = np.concatenate([tri, before, total], axis=0).astype(np.float32)
    masks = []
    b = SUB
    while b < CHUNK:
        masks.append((((t // b) % 2 == 1) & ((s // b) == (t // b) - 1)).astype(np.float32))
        b *= 2
    return jnp.asarray(summat, BF16), jnp.asarray(np.stack(masks), F32)


def _split3(x):
    hi = x.astype(BF16)
    r = x - hi.astype(F32)
    mid = r.astype(BF16)
    lo = (r - mid.astype(F32)).astype(BF16)
    return hi, mid, lo


def _mixer_ln_kernel(h_ref, win_ref, lb_ref, gn_ref, pw_ref, ps_ref, wout_ref, g_ref, b_ref,
                     summat_ref, masks_ref, o_ref,
                     q_s, k_s, f_s, i_s, lf_s, gate_s, acc_s, vb_s, state_s):
    ts = h_ref.shape[0]
    seq_tile = pl.program_id(1)
    n_levels = masks_ref.shape[0]

    @pl.when(seq_tile == 0)
    def _():
        state_s[...] = jnp.zeros_like(state_s)
        vb_s[0:POOL_HALO, :] = jnp.zeros((POOL_HALO, D_POOL), F32)
        k_s[0:SUB, :] = jnp.zeros((SUB, D_HGRN), F32)
        f_s[0:SUB, :] = jnp.zeros((SUB, D_HGRN), F32)
        i_s[0:SUB, :] = jnp.zeros((SUB, D_HGRN), F32)

    @pl.when(seq_tile != 0)
    def _():
        vb_s[0:POOL_HALO, :] = vb_s[ts:ts + POOL_HALO, :]

    h = h_ref[...]
    hb = h.astype(BF16)

    def proj(j):
        return _dot(hb, win_ref[:, j * D_HGRN:(j + 1) * D_HGRN])

    a = lb_ref[...]
    e = jnp.exp(a - jnp.max(a, axis=0, keepdims=True))
    lower = e[0:1, :] / jnp.sum(e, axis=0, keepdims=True)

    q_s[...] = _silu(proj(0))
    forget = lower + (1.0 - lower) * jax.nn.sigmoid(proj(1))
    f_s[SUB:SUB + ts, :] = forget
    k_s[SUB:SUB + ts, :] = 1.0 - forget
    lf_s[...] = jnp.log(forget)
    i_s[SUB:SUB + ts, :] = proj(2)
    gate_s[...] = _silu(proj(3))
    vb_s[POOL_HALO:POOL_HALO + ts, :] = proj(4)

    ones_red = jnp.ones((HEAD_DIM, HEAD_DIM), BF16)
    row_in_sub = lax.broadcasted_iota(jnp.int32, (ts, HEAD_DIM), 0) % SUB

    for hd in range(HGRN_HEADS):
        cols = slice(hd * HEAD_DIM, (hd + 1) * HEAD_DIM)
        q = q_s[:, cols]
        decay = None
        acc = None
        for d in range(SUB):
            kd = k_s[SUB - d:SUB - d + ts, cols]
            vd = i_s[SUB - d:SUB - d + ts, cols]
            if d == 0:
                p = q * kd
            else:
                fd = f_s[SUB - d + 1:SUB - d + 1 + ts, cols]
                decay = fd if decay is None else decay * fd
                p = q * kd * decay
            w = _dot(p.astype(BF16), ones_red)
            if d > 0:
                w = jnp.where(row_in_sub >= d, w, 0.0)
            acc = w * vd if acc is None else acc + w * vd
        acc_s[:, cols] = acc

    summat = summat_ref[...]

    def chunk_body(c, carry):
        r0 = pl.multiple_of(c * CHUNK, CHUNK)
        rows = pl.ds(r0, CHUNK)
        rows_h = pl.ds(r0 + SUB, CHUNK)
        hi, mid, lo = _split3(lf_s[rows, :])
        sums = _dot(summat, hi) + _dot(summat, mid) + _dot(summat, lo)
        for hd in range(HGRN_HEADS):
            cols = slice(hd * HEAD_DIM, (hd + 1) * HEAD_DIM)
            cum = sums[0:CHUNK, cols]
            nblk = CHUNK // SUB
            bound = [sums[CHUNK + j * SUB:CHUNK + (j + 1) * SUB, cols] for j in range(nblk)]
            bound.append(sums[2 * CHUNK:2 * CHUNK + SUB, cols])
            q = q_s[rows, cols]
            k = k_s[rows_h, cols]
            v = i_s[rows_h, cols].astype(BF16)
            scores = None
            for lvl in range(n_levels):
                m = 2 ** lvl
                ref_q = jnp.concatenate([bound[(j // m) * m] for j in range(nblk)], axis=0)
                ref_k = jnp.concatenate([bound[(j // m) * m + m] for j in range(nblk)], axis=0)
                ql = (q * jnp.exp(cum - ref_q)).astype(BF16)
                kl = (k * jnp.exp(ref_k - cum)).astype(BF16)
                sc = _dot_nt(ql, kl) * masks_ref[lvl]
                scores = sc if scores is None else scores + sc
            st = state_s[hd]
            total = bound[nblk]
            total_rows = jnp.concatenate([total] * nblk, axis=0)
            o = _dot(scores.astype(BF16), v)
            o = o + _dot_nt((q * jnp.exp(cum)).astype(BF16), st.astype(BF16))
            kdec = (k * jnp.exp(total_rows - cum)).astype(BF16)
            state_s[hd] = st * jnp.exp(total[0:1, :]) + _dot_tn(v, kdec)
            acc_s[rows, cols] = acc_s[rows, cols] + o
        return carry

    lax.fori_loop(0, ts // CHUNK, chunk_body, 0)

    gn = gn_ref[...]
    merged = []
    for hd in range(HGRN_HEADS):
        cols = slice(hd * HEAD_DIM, (hd + 1) * HEAD_DIM)
        o = acc_s[:, cols]
        o = o * lax.rsqrt(jnp.mean(o * o, axis=-1, keepdims=True) + RMS_EPS) * gn
        merged.append((o * gate_s[:, cols]).astype(BF16))

    pos = seq_tile * ts + lax.broadcasted_iota(jnp.int32, (ts, 1), 0) + 1
    for grp, win in enumerate(POOL_WINDOWS):
        cols = slice(grp * POOL_GROUP_DIM, (grp + 1) * POOL_GROUP_DIM)
        cur = vb_s[POOL_HALO:POOL_HALO + ts, cols]
        wsum = cur
        for j in range(1, win):
            wsum = wsum + vb_s[POOL_HALO - j:POOL_HALO - j + ts, cols]
        count = jnp.minimum(pos, win).astype(F32)
        pooled = wsum / count - cur
        mixed = _dot(pooled.astype(BF16), pw_ref[grp]) * ps_ref[:, cols]
        merged.append(mixed.astype(BF16))

    y = _dot(jnp.concatenate(merged, axis=-1), wout_ref[...])
    o_ref[...] = _layernorm(ALPHA * h + y, g_ref[...], b_ref[...])


def _mixer_ln(h2d, batch, seq, w_in, lb, gnorm, pool_w, pool_scale, w_out, g, b):
    ts = MIX_TILE
    assert seq % ts == 0 and ts % CHUNK == 0
    n_seq = seq // ts
    summat, masks = _chunk_constants()
    tile = pl.BlockSpec((ts, D_MODEL), lambda bi, si: (bi * n_seq + si, 0))
    return pl.pallas_call(
        _mixer_ln_kernel,
        grid=(batch, n_seq),
        in_specs=[
            tile,
            _resident(w_in.shape),
            _resident(lb.shape),
            _resident(gnorm.shape),
            _resident(pool_w.shape),
            _resident(pool_scale.shape),
            _resident(w_out.shape),
            _resident(g.shape),
            _resident(b.shape),
            _resident(summat.shape),
            _resident(masks.shape),
        ],
        out_specs=tile,
        out_shape=jax.ShapeDtypeStruct(h2d.shape, F32),
        scratch_shapes=[
            pltpu.VMEM((ts, D_HGRN), F32),
            pltpu.VMEM((ts + SUB, D_HGRN), F32),
            pltpu.VMEM((ts + SUB, D_HGRN), F32),
            pltpu.VMEM((ts + SUB, D_HGRN), F32),
            pltpu.VMEM((ts, D_HGRN), F32),
            pltpu.VMEM((ts, D_HGRN), F32),
            pltpu.VMEM((ts, D_HGRN), F32),
            pltpu.VMEM((ts + POOL_HALO, D_POOL), F32),
            pltpu.VMEM((HGRN_HEADS, HEAD_DIM, HEAD_DIM), F32),
        ],
        compiler_params=pltpu.CompilerParams(
            dimension_semantics=("arbitrary", "arbitrary"), vmem_limit_bytes=V7X_VMEM_LIMIT),
        name="mixer_ln",
    )(h2d, w_in, lb, gnorm, pool_w, pool_scale, w_out, g, b, summat, masks)


def _kv_proj_kernel(mem_ref, wk_ref, wv_ref, k_ref, v_ref):
    mb = mem_ref[...].astype(BF16)
    k_ref[...] = _dot(mb, wk_ref[...]).astype(BF16)
    v_ref[...] = _dot(mb, wv_ref[...]).astype(BF16)


def _kv_proj(mem2d, n_mem, wk, wv):
    rows = mem2d.shape[0]
    tile = pl.BlockSpec((n_mem, D_MODEL), lambda i: (i, 0))
    return pl.pallas_call(
        _kv_proj_kernel,
        grid=(rows // n_mem,),
        in_specs=[tile, _resident(wk.shape), _resident(wv.shape)],
        out_specs=[tile, tile],
        out_shape=[jax.ShapeDtypeStruct((rows, D_MODEL), BF16)] * 2,
        compiler_params=pltpu.CompilerParams(
            dimension_semantics=("arbitrary",), vmem_limit_bytes=V7X_VMEM_LIMIT),
        name="kv_proj",
    )(mem2d, wk, wv)


def _xattn_ln_kernel(h_ref, k_ref, v_ref, wq_ref, wo_ref, g_ref, b_ref, o_ref):
    h = h_ref[...]
    q = _dot(h.astype(BF16), wq_ref[...])
    heads = []
    for hd in range(XA_HEADS):
        cols = slice(hd * XA_HEAD_DIM, (hd + 1) * XA_HEAD_DIM)
        s = _dot_nt(q[:, cols].astype(BF16), k_ref[:, cols]) * (XA_HEAD_DIM ** -0.5)
        s = s - jnp.max(s, axis=-1, keepdims=True)
        p = jnp.exp(s)
        p = p / jnp.sum(p, axis=-1, keepdims=True)
        heads.append(_dot(p.astype(BF16), v_ref[:, cols]).astype(BF16))
    y = _dot(jnp.concatenate(heads, axis=-1), wo_ref[...])
    o_ref[...] = _layernorm(ALPHA * h + y, g_ref[...], b_ref[...])


def _xattn_ln(h2d, batch, seq, k2d, v2d, n_mem, wq, wo, g, b):
    ts = XA_TILE
    assert seq % ts == 0
    n_seq = seq // ts
    tile = pl.BlockSpec((ts, D_MODEL), lambda bi, si: (bi * n_seq + si, 0))
    mem_tile = pl.BlockSpec((n_mem, D_MODEL), lambda bi, si: (bi, 0))
    return pl.pallas_call(
        _xattn_ln_kernel,
        grid=(batch, n_seq),
        in_specs=[tile, mem_tile, mem_tile, _resident(wq.shape), _resident(wo.shape),
                  _resident(g.shape), _resident(b.shape)],
        out_specs=tile,
        out_shape=jax.ShapeDtypeStruct(h2d.shape, F32),
        compiler_params=pltpu.CompilerParams(
            dimension_semantics=("arbitrary", "arbitrary"), vmem_limit_bytes=V7X_VMEM_LIMIT),
        name="xattn_ln",
    )(h2d, k2d, v2d, wq, wo, g, b)


def kernel(x, mem, w_ffn1_in, w_ffn1_out, ln1_g, ln1_b, w_mix_in, hgrn_lb, hgrn_gnorm, pool_w, pool_scale, w_mix_out, ln2_g, ln2_b, xa_wq, xa_wk, xa_wv, xa_wo, ln3_g, ln3_b, w_ffn2_in, w_ffn2_out, ln4_g, ln4_b):
    batch, seq, _ = x.shape
    n_mem = mem.shape[1]
    assert w_ffn1_in.shape[0] == DEPTH == 1
    bf = lambda w: w.astype(BF16)
    h = x.reshape(batch * seq, D_MODEL)
    mem2d = mem.reshape(batch * n_mem, D_MODEL)
    for l in range(DEPTH):
        h = _ffn_ln(h, bf(w_ffn1_in[l]), bf(w_ffn1_out[l]), ln1_g[l:l + 1], ln1_b[l:l + 1])
        h = _mixer_ln(h, batch, seq, bf(w_mix_in[l]), hgrn_lb, hgrn_gnorm[l:l + 1], bf(pool_w[l]),
                      pool_scale[l:l + 1], bf(w_mix_out[l]), ln2_g[l:l + 1], ln2_b[l:l + 1])
        k2d, v2d = _kv_proj(mem2d, n_mem, bf(xa_wk[l]), bf(xa_wv[l]))
        h = _xattn_ln(h, batch, seq, k2d, v2d, n_mem, bf(xa_wq[l]), bf(xa_wo[l]),
                      ln3_g[l:l + 1], ln3_b[l:l + 1])
        h = _ffn_ln(h, bf(w_ffn2_in[l]), bf(w_ffn2_out[l]), ln4_g[l:l + 1], ln4_b[l:l + 1])
    return h.reshape(batch, seq, D_MODEL)
```

```python
import functools

import numpy as np
import jax
import jax.numpy as jnp
from jax import lax
from jax.experimental import pallas as pl
from jax.experimental.pallas import tpu as pltpu

F32 = jnp.float32
BF16 = jnp.bfloat16

D_MODEL = 1024
DEPTH = 1
D_HGRN = 512
D_POOL = 512
HGRN_HEADS = 4
HEAD_DIM = 128
POOL_WINDOWS = (2, 4, 8, 16)
POOL_GROUP_DIM = 128
D_FF = 2816
XA_HEADS = 4
XA_HEAD_DIM = 256
ALPHA = (2.0 * DEPTH) ** 0.25
LN_EPS = 1e-5
RMS_EPS = 1e-6

V7X_SUBLANES = 8
V7X_MXU_DIM = 256
V7X_VMEM_LIMIT = 56 * 1024 * 1024

CHUNK = 64
SUB = V7X_SUBLANES
POOL_HALO = 16
FFN_TILE = 512
MIX_TILE = 512
XA_TILE = 512
FF_CHUNKS = ((0, 1024), (1024, 1024), (2048, 768))


def _dot(a, b):
    return jnp.dot(a, b, preferred_element_type=F32)


def _dot_nt(a, b):
    return lax.dot_general(a, b, (((1,), (1,)), ((), ())), preferred_element_type=F32)


def _dot_tn(a, b):
    return lax.dot_general(a, b, (((0,), (0,)), ((), ())), preferred_element_type=F32)


def _silu(x):
    return x * jax.nn.sigmoid(x)


def _layernorm(y, g, b):
    mu = jnp.mean(y, axis=-1, keepdims=True)
    d = y - mu
    var = jnp.mean(d * d, axis=-1, keepdims=True)
    return d * lax.rsqrt(var + LN_EPS) * g + b


def _resident(shape):
    zeros = (0,) * len(shape)
    return pl.BlockSpec(shape, lambda *_: zeros, pipeline_mode=pl.Buffered(1))


def _ffn_ln_kernel(x_ref, win_ref, wout_ref, g_ref, b_ref, o_ref):
    x = x_ref[...]
    xb = x.astype(BF16)
    acc = None
    for c0, cw in FF_CHUNKS:
        gate = _dot(xb, win_ref[:, c0:c0 + cw])
        up = _dot(xb, win_ref[:, D_FF + c0:D_FF + c0 + cw])
        act = (_silu(gate) * up).astype(BF16)
        part = _dot(act, wout_ref[c0:c0 + cw, :])
        acc = part if acc is None else acc + part
    o_ref[...] = _layernorm(ALPHA * x + 0.5 * acc, g_ref[...], b_ref[...])


def _ffn_ln(x2d, w_in, w_out, g, b):
    m = x2d.shape[0]
    assert m % FFN_TILE == 0
    return pl.pallas_call(
        _ffn_ln_kernel,
        grid=(m // FFN_TILE,),
        in_specs=[
            pl.BlockSpec((FFN_TILE, D_MODEL), lambda i: (i, 0)),
            _resident((D_MODEL, 2 * D_FF)),
            _resident((D_FF, D_MODEL)),
            _resident((1, D_MODEL)),
            _resident((1, D_MODEL)),
        ],
        out_specs=pl.BlockSpec((FFN_TILE, D_MODEL), lambda i: (i, 0)),
        out_shape=jax.ShapeDtypeStruct((m, D_MODEL), F32),
        compiler_params=pltpu.CompilerParams(
            dimension_semantics=("arbitrary",), vmem_limit_bytes=V7X_VMEM_LIMIT),
        name="ffn_ln",
    )(x2d, w_in, w_out, g, b)


N_BLK = CHUNK // SUB
FAR_ORDER = (7, 6, 3, 5, 4, 2, 1)
FAR_COLS = 256
NEAR_LEVELS = (0, 1, 2, 4)
PV_COLS = FAR_COLS + len(NEAR_LEVELS) * CHUNK
assert sorted(FAR_ORDER) == list(range(1, N_BLK)) and SUB * sum(FAR_ORDER) <= FAR_COLS


def _chunk_constants():
    t = np.arange(CHUNK)[:, None]
    s = np.arange(CHUNK)[None, :]
    tri = (s <= t).astype(np.float32)
    mask = np.zeros((CHUNK, PV_COLS), np.float32)
    off = 0
    for i in FAR_ORDER:
        mask[:, off:off + SUB * i] = (t // SUB == i)
        off += SUB * i
    for n, b in enumerate(NEAR_LEVELS):
        valid = (t == s) if b == 0 else (((t // b) % 2 == 1) & ((s // b) == (t // b) - 1))
        mask[:, FAR_COLS + n * CHUNK:FAR_COLS + (n + 1) * CHUNK] = valid
    return jnp.asarray(tri, BF16), jnp.asarray(mask, F32)


def _split3(x):
    hi = x.astype(BF16)
    r = x - hi.astype(F32)
    mid = r.astype(BF16)
    lo = (r - mid.astype(F32)).astype(BF16)
    return hi, mid, lo


def _mixer_ln_kernel(h_ref, win_ref, lb_ref, gn_ref, pw_ref, ps_ref, wout_ref, g_ref, b_ref,
                     tri_ref, mask_ref, o_ref,
                     q_s, k_s, f_s, i_s, cum_s, gate_s, merged_s, vb_s, state_s):
    ts = h_ref.shape[0]
    seq_tile = pl.program_id(1)

    @pl.when(seq_tile == 0)
    def _():
        state_s[...] = jnp.zeros_like(state_s)
        vb_s[0:POOL_HALO, :] = jnp.zeros((POOL_HALO, D_POOL), F32)

    @pl.when(seq_tile != 0)
    def _():
        vb_s[0:POOL_HALO, :] = vb_s[ts:ts + POOL_HALO, :]

    h = h_ref[...]
    hb = h.astype(BF16)

    def proj(j):
        return _dot(hb, win_ref[:, j * D_HGRN:(j + 1) * D_HGRN])

    a = lb_ref[...]
    e = jnp.exp(a - jnp.max(a, axis=0, keepdims=True))
    lower = e[0:1, :] / jnp.sum(e, axis=0, keepdims=True)

    q_s[...] = _silu(proj(0))
    forget = lower + (1.0 - lower) * jax.nn.sigmoid(proj(1))
    f_s[...] = forget
    k_s[...] = 1.0 - forget
    lf = jnp.log(forget)
    tri = tri_ref[...]
    for c in range(ts // CHUNK):
        hi, mid, lo = _split3(lf[c * CHUNK:(c + 1) * CHUNK])
        cum_s[c * CHUNK:(c + 1) * CHUNK, :] = _dot(tri, hi) + _dot(tri, mid) + _dot(tri, lo)
    i_s[...] = proj(2)
    gate_s[...] = _silu(proj(3))
    vb_s[POOL_HALO:POOL_HALO + ts, :] = proj(4)

    sub_row = lax.broadcasted_iota(jnp.int32, (SUB, HEAD_DIM), 0)
    zero_blk = jnp.zeros((SUB, HEAD_DIM), F32)
    far_pad = jnp.zeros((FAR_COLS - SUB * sum(FAR_ORDER), HEAD_DIM), F32)
    gn = gn_ref[...]
    zero_kb = jnp.zeros((CHUNK, HEAD_DIM), BF16)

    def block_diag(a, b):
        return jnp.concatenate([jnp.concatenate([a, zero_kb], axis=1),
                                jnp.concatenate([zero_kb, b], axis=1)], axis=0)

    def chunk_body(c):
        r0 = c * CHUNK
        rows = pl.ds(r0, CHUNK)
        mask = mask_ref[...]
        for hd in range(HGRN_HEADS):
            cols = slice(hd * HEAD_DIM, (hd + 1) * HEAD_DIM)
            q = q_s[rows, cols]
            k = k_s[rows, cols]
            f = f_s[rows, cols]
            v = i_s[rows, cols]
            cum = cum_s[rows, cols]

            def cum_row(row):
                return jnp.broadcast_to(cum_s[pl.ds(r0 + row, 1), cols], (SUB, HEAD_DIM))

            bound = [zero_blk] + [cum_row(SUB * j - 1) for j in range(1, N_BLK + 1)]

            q_far = (q * jnp.exp(cum - jnp.concatenate(bound[:N_BLK], axis=0))).astype(BF16)
            k_parts, v_parts = [], []
            for i in FAR_ORDER:
                n = SUB * i
                k_parts.append(k[:n] * jnp.exp(jnp.concatenate([bound[i]] * i, axis=0) - cum[:n]))
                v_parts.append(v[:n])
            k_far = jnp.concatenate(k_parts + [far_pad], axis=0).astype(BF16)
            s_far = _dot_nt(q_far, k_far)

            ref_q2, ref_k2, ref_q4, ref_k4 = [], [], [], []
            for j in range(N_BLK):
                lo_b, hi_b = bound[j], bound[j + 1]
                m1, m3, m5 = (cum_row(SUB * j + r) for r in (1, 3, 5))
                ref_q4.append(jnp.where(sub_row < 4, lo_b, m3))
                ref_k4.append(jnp.where(sub_row < 4, m3, hi_b))
                ref_q2.append(jnp.where(sub_row < 2, lo_b, jnp.where(sub_row < 4, m1, jnp.where(sub_row < 6, m3, m5))))
                ref_k2.append(jnp.where(sub_row < 2, m1, jnp.where(sub_row < 4, m3, jnp.where(sub_row < 6, m5, hi_b))))
            cat = lambda parts: jnp.concatenate(parts, axis=0)
            q2 = q * jnp.exp(cum - cat(ref_q2))
            k2 = k * jnp.exp(cat(ref_k2) - cum)
            q4 = q * jnp.exp(cum - cat(ref_q4))
            k4 = k * jnp.exp(cat(ref_k4) - cum)
            kb = k.astype(BF16)
            s_a = _dot_nt(jnp.concatenate([q, q * f], axis=1).astype(BF16), block_diag(kb, kb))
            s_b = _dot_nt(jnp.concatenate([q2, q4], axis=1).astype(BF16),
                          block_diag(k2.astype(BF16), k4.astype(BF16)))

            p = (jnp.concatenate([s_far, s_a, s_b], axis=1) * mask).astype(BF16)
            v_all = jnp.concatenate(v_parts + [far_pad] + [v] * len(NEAR_LEVELS), axis=0).astype(BF16)
            o = _dot(p, v_all)

            st = state_s[hd]
            total = bound[N_BLK]
            o = o + _dot_nt((q * jnp.exp(cum)).astype(BF16), st.astype(BF16))
            k_dec = (k * jnp.exp(jnp.concatenate([total] * N_BLK, axis=0) - cum)).astype(BF16)
            state_s[hd] = st * jnp.exp(total[0:1, :]) + _dot_tn(v.astype(BF16), k_dec)

            o = o * lax.rsqrt(jnp.mean(o * o, axis=-1, keepdims=True) + RMS_EPS) * gn
            merged_s[rows, cols] = (o * gate_s[rows, cols]).astype(BF16)

    for c in range(ts // CHUNK):
        chunk_body(c)

    pos = seq_tile * ts + lax.broadcasted_iota(jnp.int32, (ts, 1), 0) + 1
    for grp, win in enumerate(POOL_WINDOWS):
        cols = slice(grp * POOL_GROUP_DIM, (grp + 1) * POOL_GROUP_DIM)
        ext = vb_s[:, cols]
        wsum = ext
        span = 1
        while span < win:
            wsum = wsum + pltpu.roll(wsum, span, 0)
            span *= 2
        cur = ext[POOL_HALO:]
        count = jnp.minimum(pos, win).astype(F32)
        pooled = wsum[POOL_HALO:] / count - cur
        mixed = _dot(pooled.astype(BF16), pw_ref[grp]) * ps_ref[:, cols]
        merged_s[:, D_HGRN + grp * POOL_GROUP_DIM:D_HGRN + (grp + 1) * POOL_GROUP_DIM] = mixed.astype(BF16)

    y = _dot(merged_s[...], wout_ref[...])
    o_ref[...] = _layernorm(ALPHA * h + y, g_ref[...], b_ref[...])


def _mixer_ln(h2d, batch, seq, w_in, lb, gnorm, pool_w, pool_scale, w_out, g, b):
    ts = MIX_TILE
    assert seq % ts == 0 and ts % CHUNK == 0
    n_seq = seq // ts
    tri, mask = _chunk_constants()
    tile = pl.BlockSpec((ts, D_MODEL), lambda bi, si: (bi * n_seq + si, 0))
    return pl.pallas_call(
        _mixer_ln_kernel,
        grid=(batch, n_seq),
        in_specs=[
            tile,
            _resident(w_in.shape),
            _resident(lb.shape),
            _resident(gnorm.shape),
            _resident(pool_w.shape),
            _resident(pool_scale.shape),
            _resident(w_out.shape),
            _resident(g.shape),
            _resident(b.shape),
            _resident(tri.shape),
            _resident(mask.shape),
        ],
        out_specs=tile,
        out_shape=jax.ShapeDtypeStruct(h2d.shape, F32),
        scratch_shapes=[
            pltpu.VMEM((ts, D_HGRN), F32),
            pltpu.VMEM((ts, D_HGRN), F32),
            pltpu.VMEM((ts, D_HGRN), F32),
            pltpu.VMEM((ts, D_HGRN), F32),
            pltpu.VMEM((ts, D_HGRN), F32),
            pltpu.VMEM((ts, D_HGRN), F32),
            pltpu.VMEM((ts, D_HGRN + D_POOL), BF16),
            pltpu.VMEM((ts + POOL_HALO, D_POOL), F32),
            pltpu.VMEM((HGRN_HEADS, HEAD_DIM, HEAD_DIM), F32),
        ],
        compiler_params=pltpu.CompilerParams(
            dimension_semantics=("arbitrary", "arbitrary"), vmem_limit_bytes=V7X_VMEM_LIMIT),
        name="mixer_ln",
    )(h2d, w_in, lb, gnorm, pool_w, pool_scale, w_out, g, b, tri, mask)


def _kv_proj_kernel(mem_ref, wk_ref, wv_ref, k_ref, v_ref):
    mb = mem_ref[...].astype(BF16)
    k_ref[...] = _dot(mb, wk_ref[...]).astype(BF16)
    v_ref[...] = _dot(mb, wv_ref[...]).astype(BF16)


def _kv_proj(mem2d, n_mem, wk, wv):
    rows = mem2d.shape[0]
    tile = pl.BlockSpec((n_mem, D_MODEL), lambda i: (i, 0))
    return pl.pallas_call(
        _kv_proj_kernel,
        grid=(rows // n_mem,),
        in_specs=[tile, _resident(wk.shape), _resident(wv.shape)],
        out_specs=[tile, tile],
        out_shape=[jax.ShapeDtypeStruct((rows, D_MODEL), BF16)] * 2,
        compiler_params=pltpu.CompilerParams(
            dimension_semantics=("arbitrary",), vmem_limit_bytes=V7X_VMEM_LIMIT),
        name="kv_proj",
    )(mem2d, wk, wv)


def _xattn_ln_kernel(h_ref, k_ref, v_ref, wq_ref, wo_ref, g_ref, b_ref, o_ref):
    h = h_ref[...]
    q = _dot(h.astype(BF16), wq_ref[...])
    heads = []
    for hd in range(XA_HEADS):
        cols = slice(hd * XA_HEAD_DIM, (hd + 1) * XA_HEAD_DIM)
        s = _dot_nt(q[:, cols].astype(BF16), k_ref[:, cols]) * (XA_HEAD_DIM ** -0.5)
        s = s - jnp.max(s, axis=-1, keepdims=True)
        p = jnp.exp(s)
        p = p / jnp.sum(p, axis=-1, keepdims=True)
        heads.append(_dot(p.astype(BF16), v_ref[:, cols]).astype(BF16))
    y = _dot(jnp.concatenate(heads, axis=-1), wo_ref[...])
    o_ref[...] = _layernorm(ALPHA * h + y, g_ref[...], b_ref[...])


def _xattn_ln(h2d, batch, seq, k2d, v2d, n_mem, wq, wo, g, b):
    ts = XA_TILE
    assert seq % ts == 0
    n_seq = seq // ts
    tile = pl.BlockSpec((ts, D_MODEL), lambda bi, si: (bi * n_seq + si, 0))
    mem_tile = pl.BlockSpec((n_mem, D_MODEL), lambda bi, si: (bi, 0))
    return pl.pallas_call(
        _xattn_ln_kernel,
        grid=(batch, n_seq),
        in_specs=[tile, mem_tile, mem_tile, _resident(wq.shape), _resident(wo.shape),
                  _resident(g.shape), _resident(b.shape)],
        out_specs=tile,
        out_shape=jax.ShapeDtypeStruct(h2d.shape, F32),
        compiler_params=pltpu.CompilerParams(
            dimension_semantics=("arbitrary", "arbitrary"), vmem_limit_bytes=V7X_VMEM_LIMIT),
        name="xattn_ln",
    )(h2d, k2d, v2d, wq, wo, g, b)


def kernel(x, mem, w_ffn1_in, w_ffn1_out, ln1_g, ln1_b, w_mix_in, hgrn_lb, hgrn_gnorm, pool_w, pool_scale, w_mix_out, ln2_g, ln2_b, xa_wq, xa_wk, xa_wv, xa_wo, ln3_g, ln3_b, w_ffn2_in, w_ffn2_out, ln4_g, ln4_b):
    batch, seq, _ = x.shape
    n_mem = mem.shape[1]
    assert w_ffn1_in.shape[0] == DEPTH == 1
    bf = lambda w: w.astype(BF16)
    h = x.reshape(batch * seq, D_MODEL)
    mem2d = mem.reshape(batch * n_mem, D_MODEL)
    for l in range(DEPTH):
        h = _ffn_ln(h, bf(w_ffn1_in[l]), bf(w_ffn1_out[l]), ln1_g[l:l + 1], ln1_b[l:l + 1])
        h = _mixer_ln(h, batch, seq, bf(w_mix_in[l]), hgrn_lb, hgrn_gnorm[l:l + 1], bf(pool_w[l]),
                      pool_scale[l:l + 1], bf(w_mix_out[l]), ln2_g[l:l + 1], ln2_b[l:l + 1])
        k2d, v2d = _kv_proj(mem2d, n_mem, bf(xa_wk[l]), bf(xa_wv[l]))
        h = _xattn_ln(h, batch, seq, k2d, v2d, n_mem, bf(xa_wq[l]), bf(xa_wo[l]),
                      ln3_g[l:l + 1], ln3_b[l:l + 1])
        h = _ffn_ln(h, bf(w_ffn2_in[l]), bf(w_ffn2_out[l]), ln4_g[l:l + 1], ln4_b[l:l + 1])
    return h.reshape(batch, seq, D_MODEL)
```

```python
import functools

import numpy as np
import jax
import jax.numpy as jnp
from jax import lax
from jax.experimental import pallas as pl
from jax.experimental.pallas import tpu as pltpu

F32 = jnp.float32
BF16 = jnp.bfloat16

D_MODEL = 1024
DEPTH = 1
D_HGRN = 512
D_POOL = 512
HGRN_HEADS = 4
HEAD_DIM = 128
POOL_WINDOWS = (2, 4, 8, 16)
POOL_GROUP_DIM = 128
D_FF = 2816
D_IN_PROJ = 4 * D_HGRN + D_POOL
XA_HEADS = 4
XA_HEAD_DIM = 256
ALPHA = (2.0 * DEPTH) ** 0.25
LN_EPS = 1e-5
RMS_EPS = 1e-6

V7X_SUBLANES = 8
V7X_MXU_DIM = 256
V7X_VMEM_LIMIT = 56 * 1024 * 1024

CHUNK = 64
SUB = V7X_SUBLANES
POOL_HALO = 16
FFN_TILE = 512
MIX_TILE = 512
XA_TILE = 512
FF_CHUNKS = ((0, 1024), (1024, 1024), (2048, 768))


def _dot(a, b):
    return jnp.dot(a, b, preferred_element_type=F32)


def _dot_nt(a, b):
    return lax.dot_general(a, b, (((1,), (1,)), ((), ())), preferred_element_type=F32)


def _dot_tn(a, b):
    return lax.dot_general(a, b, (((0,), (0,)), ((), ())), preferred_element_type=F32)


def _silu(x):
    return x * jax.nn.sigmoid(x)


def _layernorm(y, g, b):
    mu = jnp.mean(y, axis=-1, keepdims=True)
    d = y - mu
    var = jnp.mean(d * d, axis=-1, keepdims=True)
    return d * lax.rsqrt(var + LN_EPS) * g + b


def _resident(shape):
    zeros = (0,) * len(shape)
    return pl.BlockSpec(shape, lambda *_: zeros, pipeline_mode=pl.Buffered(1))


def _ffn_ln_kernel(x_ref, win_ref, wout_ref, g_ref, b_ref, o_ref):
    x = x_ref[...]
    xb = x.astype(BF16)
    acc = None
    for c0, cw in FF_CHUNKS:
        gate = _dot(xb, win_ref[:, c0:c0 + cw])
        up = _dot(xb, win_ref[:, D_FF + c0:D_FF + c0 + cw])
        act = (_silu(gate) * up).astype(BF16)
        part = _dot(act, wout_ref[c0:c0 + cw, :])
        acc = part if acc is None else acc + part
    o_ref[...] = _layernorm(ALPHA * x + 0.5 * acc, g_ref[...], b_ref[...])


def _ffn_ln(x2d, w_in, w_out, g, b):
    m = x2d.shape[0]
    assert m % FFN_TILE == 0
    return pl.pallas_call(
        _ffn_ln_kernel,
        grid=(m // FFN_TILE,),
        in_specs=[
            pl.BlockSpec((FFN_TILE, D_MODEL), lambda i: (i, 0)),
            _resident((D_MODEL, 2 * D_FF)),
            _resident((D_FF, D_MODEL)),
            _resident((1, D_MODEL)),
            _resident((1, D_MODEL)),
        ],
        out_specs=pl.BlockSpec((FFN_TILE, D_MODEL), lambda i: (i, 0)),
        out_shape=jax.ShapeDtypeStruct((m, D_MODEL), F32),
        compiler_params=pltpu.CompilerParams(
            dimension_semantics=("arbitrary",), vmem_limit_bytes=V7X_VMEM_LIMIT),
        name="ffn_ln",
    )(x2d, w_in, w_out, g, b)


N_BLK = CHUNK // SUB
FAR_ORDER = (7, 6, 3, 5, 4, 2, 1)
FAR_COLS = 256
NEAR_LEVELS = (0, 1, 2, 4)
PV_COLS = FAR_COLS + len(NEAR_LEVELS) * CHUNK
assert sorted(FAR_ORDER) == list(range(1, N_BLK)) and SUB * sum(FAR_ORDER) <= FAR_COLS


def _chunk_constants():
    t = np.arange(CHUNK)[:, None]
    s = np.arange(CHUNK)[None, :]
    tri = (s <= t).astype(np.float32)
    mask = np.zeros((CHUNK, PV_COLS), np.float32)
    off = 0
    for i in FAR_ORDER:
        mask[:, off:off + SUB * i] = (t // SUB == i)
        off += SUB * i
    for n, b in enumerate(NEAR_LEVELS):
        valid = (t == s) if b == 0 else (((t // b) % 2 == 1) & ((s // b) == (t // b) - 1))
        mask[:, FAR_COLS + n * CHUNK:FAR_COLS + (n + 1) * CHUNK] = valid
    return jnp.asarray(tri, BF16), jnp.asarray(mask, F32)


def _split3(x):
    hi = x.astype(BF16)
    r = x - hi.astype(F32)
    mid = r.astype(BF16)
    lo = (r - mid.astype(F32)).astype(BF16)
    return hi, mid, lo


def _mixer_ln_kernel(hn_ref, hc_ref, win_ref, lb_ref, gn_ref, pw_ref, ps_ref, wout_ref, g_ref, b_ref,
                     tri_ref, mask_ref, o_ref,
                     q_a, k_a, f_a, i_a, cum_a, gate_a, vb_a,
                     q_s, k_s, f_s, i_s, cum_s, gate_s, vb_s, merged_s, y_s, state_s, *, n_seq):
    ts = hn_ref.shape[0]
    step = pl.program_id(0)
    seq_tile_next = step % n_seq
    seq_tile = (step + n_seq - 1) % n_seq
    handoff = ((q_a, q_s), (k_a, k_s), (f_a, f_s), (i_a, i_s), (cum_a, cum_s), (gate_a, gate_s), (vb_a, vb_s))

    @pl.when(step == 0)
    def _():
        state_s[...] = jnp.zeros_like(state_s)
        for src, _ in handoff:
            src[...] = jnp.zeros_like(src)

    half = ts // 2
    col_blk = V7X_MXU_DIM

    def copy_rows(c):
        rows = slice(c * CHUNK, (c + 1) * CHUNK)
        for src, dst in handoff[:-1]:
            dst[rows, :] = src[rows, :]
        if c == 0:
            vb_s[0:POOL_HALO, :] = vb_a[0:POOL_HALO, :]
        prow = slice(POOL_HALO + c * CHUNK, POOL_HALO + (c + 1) * CHUNK)
        vb_s[prow, :] = vb_a[prow, :]

    def pool_halo():
        vb_a[0:POOL_HALO, :] = jnp.where(seq_tile_next == 0, 0.0, vb_s[ts:ts + POOL_HALO, :])

    a = lb_ref[...]
    e = jnp.exp(a - jnp.max(a, axis=0, keepdims=True))
    lower = e[0:1, :] / jnp.sum(e, axis=0, keepdims=True)

    def in_proj(r, j):
        rows = slice(r * half, (r + 1) * half)
        x = _dot(hn_ref[rows, :].astype(BF16), win_ref[:, j * col_blk:(j + 1) * col_blk])
        kind, cb = divmod(j * col_blk, D_HGRN)
        cols = slice(cb, cb + col_blk)
        if kind == 0:
            q_a[rows, cols] = _silu(x)
        elif kind == 1:
            forget = lower[:, cols] + (1.0 - lower[:, cols]) * jax.nn.sigmoid(x)
            f_a[rows, cols] = forget
            k_a[rows, cols] = 1.0 - forget
            cum_a[rows, cols] = jnp.log(forget)
        elif kind == 2:
            i_a[rows, cols] = x
        elif kind == 3:
            gate_a[rows, cols] = _silu(x)
        else:
            vb_a[POOL_HALO + r * half:POOL_HALO + (r + 1) * half, cols] = x

    def cum_chunks(c0, n):
        tri = tri_ref[...]
        for c in range(c0, c0 + n):
            rows = slice(c * CHUNK, (c + 1) * CHUNK)
            hi, mid, lo = _split3(cum_a[rows, :])
            cum_a[rows, :] = _dot(tri, hi) + _dot(tri, mid) + _dot(tri, lo)

    sub_row = lax.broadcasted_iota(jnp.int32, (SUB, HEAD_DIM), 0)
    zero_blk = jnp.zeros((SUB, HEAD_DIM), F32)
    far_pad = jnp.zeros((FAR_COLS - SUB * sum(FAR_ORDER), HEAD_DIM), F32)
    gn = gn_ref[...]
    zero_kb = jnp.zeros((CHUNK, HEAD_DIM), BF16)

    def block_diag(a, b):
        return jnp.concatenate([jnp.concatenate([a, zero_kb], axis=1),
                                jnp.concatenate([zero_kb, b], axis=1)], axis=0)

    def head_cols(hd):
        return slice(hd * HEAD_DIM, (hd + 1) * HEAD_DIM)

    def scores_stage(c, hd):
        r0 = c * CHUNK
        rows = pl.ds(r0, CHUNK)
        cols = head_cols(hd)
        q = q_s[rows, cols]
        k = k_s[rows, cols]
        f = f_s[rows, cols]
        v = i_s[rows, cols]
        cum = cum_s[rows, cols]

        def cum_row(row):
            return jnp.broadcast_to(cum_s[pl.ds(r0 + row, 1), cols], (SUB, HEAD_DIM))

        bound = [zero_blk] + [cum_row(SUB * j - 1) for j in range(1, N_BLK + 1)]

        q_far = (q * jnp.exp(cum - jnp.concatenate(bound[:N_BLK], axis=0))).astype(BF16)
        k_parts, v_parts = [], []
        for i in FAR_ORDER:
            n = SUB * i
            k_parts.append(k[:n] * jnp.exp(jnp.concatenate([bound[i]] * i, axis=0) - cum[:n]))
            v_parts.append(v[:n])
        k_far = jnp.concatenate(k_parts + [far_pad], axis=0).astype(BF16)
        s_far = _dot_nt(q_far, k_far)

        ref_q2, ref_k2, ref_q4, ref_k4 = [], [], [], []
        for j in range(N_BLK):
            lo_b, hi_b = bound[j], bound[j + 1]
            m1, m3, m5 = (cum_row(SUB * j + r) for r in (1, 3, 5))
            ref_q4.append(jnp.where(sub_row < 4, lo_b, m3))
            ref_k4.append(jnp.where(sub_row < 4, m3, hi_b))
            ref_q2.append(jnp.where(sub_row < 2, lo_b, jnp.where(sub_row < 4, m1, jnp.where(sub_row < 6, m3, m5))))
            ref_k2.append(jnp.where(sub_row < 2, m1, jnp.where(sub_row < 4, m3, jnp.where(sub_row < 6, m5, hi_b))))
        cat = lambda parts: jnp.concatenate(parts, axis=0)
        q2 = q * jnp.exp(cum - cat(ref_q2))
        k2 = k * jnp.exp(cat(ref_k2) - cum)
        q4 = q * jnp.exp(cum - cat(ref_q4))
        k4 = k * jnp.exp(cat(ref_k4) - cum)
        kb = k.astype(BF16)
        s_a = _dot_nt(jnp.concatenate([q, q * f], axis=1).astype(BF16), block_diag(kb, kb))
        s_b = _dot_nt(jnp.concatenate([q2, q4], axis=1).astype(BF16),
                      block_diag(k2.astype(BF16), k4.astype(BF16)))

        total = bound[N_BLK]
        return dict(
            p=(jnp.concatenate([s_far, s_a, s_b], axis=1) * mask_ref[...]).astype(BF16),
            v_all=jnp.concatenate(v_parts + [far_pad] + [v] * len(NEAR_LEVELS), axis=0).astype(BF16),
            q_dec=(q * jnp.exp(cum)).astype(BF16),
            k_dec=(k * jnp.exp(jnp.concatenate([total] * N_BLK, axis=0) - cum)).astype(BF16),
            vb=v.astype(BF16),
            decay=jnp.exp(total[0:1, :]))

    def state_stage(c, hd, sc):
        st = state_s[hd]
        if c == 0:
            st = jnp.where(seq_tile == 0, 0.0, st)
        o = _dot(sc["p"], sc["v_all"]) + _dot_nt(sc["q_dec"], st.astype(BF16))
        state_s[hd] = st * sc["decay"] + _dot_tn(sc["vb"], sc["k_dec"])
        return o

    def norm_stage(c, hd, o):
        rows = pl.ds(c * CHUNK, CHUNK)
        cols = head_cols(hd)
        o = o * lax.rsqrt(jnp.mean(o * o, axis=-1, keepdims=True) + RMS_EPS) * gn
        merged_s[rows, cols] = (o * gate_s[rows, cols]).astype(BF16)

    quarter = ts // 4

    def pool_rows(qd):
        r0, nr = qd * quarter, quarter
        pos = seq_tile * ts + r0 + lax.broadcasted_iota(jnp.int32, (nr, 1), 0) + 1
        for grp, win in enumerate(POOL_WINDOWS):
            cols = slice(grp * POOL_GROUP_DIM, (grp + 1) * POOL_GROUP_DIM)
            ext = vb_s[r0:r0 + POOL_HALO + nr, cols]
            wsum = ext
            span = 1
            while span < win:
                wsum = wsum + pltpu.roll(wsum, span, 0)
                span *= 2
            cur = ext[POOL_HALO:]
            count = jnp.minimum(pos, win).astype(F32)
            pooled = wsum[POOL_HALO:] / count - cur
            mixed = _dot(pooled.astype(BF16), pw_ref[grp]) * ps_ref[:, cols]
            merged_s[r0:r0 + nr, D_HGRN + grp * POOL_GROUP_DIM:D_HGRN + (grp + 1) * POOL_GROUP_DIM] = (
                mixed.astype(BF16))

    def out_proj(qd, n2):
        rows = slice(qd * quarter, (qd + 1) * quarter)
        cols = slice(n2 * (D_MODEL // 2), (n2 + 1) * (D_MODEL // 2))
        y_s[rows, cols] = _dot(merged_s[rows, :], wout_ref[:, cols])

    def out_ln(qd):
        rows = slice(qd * quarter, (qd + 1) * quarter)
        o_ref[rows, :] = _layernorm(ALPHA * hc_ref[rows, :] + y_s[rows, :], g_ref[...], b_ref[...])

    n_chunks = ts // CHUNK
    n_slots = n_chunks * HGRN_HEADS
    P = functools.partial
    COPY, PROJ, CUM, POOL, OUT, LN = 100, 256, 192, 200, 256, 150
    early = [(0, COPY, P(copy_rows, c)) for c in range(2, n_chunks)] + [(0, 10, pool_halo)]
    for r in range(2):
        blocks = list(range(D_IN_PROJ // col_blk))
        forget_blocks = [j for j in blocks if j * col_blk // D_HGRN == 1]
        first = 2 + 2 * r
        early += [(first, PROJ, P(in_proj, r, j)) for j in forget_blocks]
        early += [(first, CUM, P(cum_chunks, r * n_chunks // 2 + 2 * i, 2)) for i in range(n_chunks // 4)]
        early += [(first, PROJ, P(in_proj, r, j)) for j in blocks if j not in forget_blocks]
    late = []
    for qd in range(4):
        first = (qd + 1) * (n_chunks // 4) * HGRN_HEADS
        late += [(first, POOL, P(pool_rows, qd)), (first, OUT, P(out_proj, qd, 0)),
                 (first, OUT, P(out_proj, qd, 1)), (first, LN, P(out_ln, qd))]
    total = sum(cost for _, cost, _ in early + late)
    copy_rows(0)
    copy_rows(1)
    done = 0
    ch = lambda slot: (slot // HGRN_HEADS, slot % HGRN_HEADS)
    scores, outs = {}, {}
    for slot in range(n_slots + 2):
        if slot < n_slots:
            scores[slot] = scores_stage(*ch(slot))
        if 1 <= slot <= n_slots:
            outs[slot - 1] = state_stage(*ch(slot - 1), scores.pop(slot - 1))
        if slot >= 2:
            norm_stage(*ch(slot - 2), outs.pop(slot - 2))
        while done < total * (slot + 1) / n_slots:
            queue = late if late and late[0][0] <= slot - 1 else early
            if not queue or queue[0][0] > slot - 1:
                break
            _, cost, fn = queue.pop(0)
            fn()
            done += cost
    for _, _, fn in early + late:
        fn()


def _mixer_ln(h2d, batch, seq, w_in, lb, gnorm, pool_w, pool_scale, w_out, g, b):
    ts = MIX_TILE
    assert seq % ts == 0 and ts % CHUNK == 0
    n_seq = seq // ts
    n_tiles = batch * n_seq
    tri, mask = _chunk_constants()
    next_tile = pl.BlockSpec((ts, D_MODEL), lambda i: (jnp.minimum(i, n_tiles - 1), 0))
    done_tile = pl.BlockSpec((ts, D_MODEL), lambda i: (jnp.maximum(i - 1, 0), 0))
    proj_set = [pltpu.VMEM((ts, D_HGRN), F32)] * 6 + [pltpu.VMEM((ts + POOL_HALO, D_POOL), F32)]
    return pl.pallas_call(
        functools.partial(_mixer_ln_kernel, n_seq=n_seq),
        grid=(n_tiles + 1,),
        in_specs=[
            next_tile,
            done_tile,
            _resident(w_in.shape),
            _resident(lb.shape),
            _resident(gnorm.shape),
            _resident(pool_w.shape),
            _resident(pool_scale.shape),
            _resident(w_out.shape),
            _resident(g.shape),
            _resident(b.shape),
            _resident(tri.shape),
            _resident(mask.shape),
        ],
        out_specs=done_tile,
        out_shape=jax.ShapeDtypeStruct(h2d.shape, F32),
        scratch_shapes=proj_set + proj_set + [
            pltpu.VMEM((ts, D_HGRN + D_POOL), BF16),
            pltpu.VMEM((ts, D_MODEL), F32),
            pltpu.VMEM((HGRN_HEADS, HEAD_DIM, HEAD_DIM), F32),
        ],
        compiler_params=pltpu.CompilerParams(
            dimension_semantics=("arbitrary",), vmem_limit_bytes=V7X_VMEM_LIMIT),
        name="mixer_ln",
    )(h2d, h2d, w_in, lb, gnorm, pool_w, pool_scale, w_out, g, b, tri, mask)


def _kv_proj_kernel(mem_ref, wk_ref, wv_ref, k_ref, v_ref):
    mb = mem_ref[...].astype(BF16)
    k_ref[...] = _dot(mb, wk_ref[...]).astype(BF16)
    v_ref[...] = _dot(mb, wv_ref[...]).astype(BF16)


def _kv_proj(mem2d, n_mem, wk, wv):
    rows = mem2d.shape[0]
    tile = pl.BlockSpec((n_mem, D_MODEL), lambda i: (i, 0))
    return pl.pallas_call(
        _kv_proj_kernel,
        grid=(rows // n_mem,),
        in_specs=[tile, _resident(wk.shape), _resident(wv.shape)],
        out_specs=[tile, tile],
        out_shape=[jax.ShapeDtypeStruct((rows, D_MODEL), BF16)] * 2,
        compiler_params=pltpu.CompilerParams(
            dimension_semantics=("arbitrary",), vmem_limit_bytes=V7X_VMEM_LIMIT),
        name="kv_proj",
    )(mem2d, wk, wv)


def _xattn_ln_kernel(h_ref, k_ref, v_ref, wq_ref, wo_ref, g_ref, b_ref, o_ref):
    h = h_ref[...]
    q = _dot(h.astype(BF16), wq_ref[...])
    heads = []
    for hd in range(XA_HEADS):
        cols = slice(hd * XA_HEAD_DIM, (hd + 1) * XA_HEAD_DIM)
        s = _dot_nt(q[:, cols].astype(BF16), k_ref[:, cols]) * (XA_HEAD_DIM ** -0.5)
        s = s - jnp.max(s, axis=-1, keepdims=True)
        p = jnp.exp(s)
        p = p / jnp.sum(p, axis=-1, keepdims=True)
        heads.append(_dot(p.astype(BF16), v_ref[:, cols]).astype(BF16))
    y = _dot(jnp.concatenate(heads, axis=-1), wo_ref[...])
    o_ref[...] = _layernorm(ALPHA * h + y, g_ref[...], b_ref[...])


def _xattn_ln(h2d, batch, seq, k2d, v2d, n_mem, wq, wo, g, b):
    ts = XA_TILE
    assert seq % ts == 0
    n_seq = seq // ts
    tile = pl.BlockSpec((ts, D_MODEL), lambda bi, si: (bi * n_seq + si, 0))
    mem_tile = pl.BlockSpec((n_mem, D_MODEL), lambda bi, si: (bi, 0))
    return pl.pallas_call(
        _xattn_ln_kernel,
        grid=(batch, n_seq),
        in_specs=[tile, mem_tile, mem_tile, _resident(wq.shape), _resident(wo.shape),
                  _resident(g.shape), _resident(b.shape)],
        out_specs=tile,
        out_shape=jax.ShapeDtypeStruct(h2d.shape, F32),
        compiler_params=pltpu.CompilerParams(
            dimension_semantics=("arbitrary", "arbitrary"), vmem_limit_bytes=V7X_VMEM_LIMIT),
        name="xattn_ln",
    )(h2d, k2d, v2d, wq, wo, g, b)


def kernel(x, mem, w_ffn1_in, w_ffn1_out, ln1_g, ln1_b, w_mix_in, hgrn_lb, hgrn_gnorm, pool_w, pool_scale, w_mix_out, ln2_g, ln2_b, xa_wq, xa_wk, xa_wv, xa_wo, ln3_g, ln3_b, w_ffn2_in, w_ffn2_out, ln4_g, ln4_b):
    batch, seq, _ = x.shape
    n_mem = mem.shape[1]
    assert w_ffn1_in.shape[0] == DEPTH == 1
    bf = lambda w: w.astype(BF16)
    h = x.reshape(batch * seq, D_MODEL)
    mem2d = mem.reshape(batch * n_mem, D_MODEL)
    for l in range(DEPTH):
        h = _ffn_ln(h, bf(w_ffn1_in[l]), bf(w_ffn1_out[l]), ln1_g[l:l + 1], ln1_b[l:l + 1])
        h = _mixer_ln(h, batch, seq, bf(w_mix_in[l]), hgrn_lb, hgrn_gnorm[l:l + 1], bf(pool_w[l]),
                      pool_scale[l:l + 1], bf(w_mix_out[l]), ln2_g[l:l + 1], ln2_b[l:l + 1])
        k2d, v2d = _kv_proj(mem2d, n_mem, bf(xa_wk[l]), bf(xa_wv[l]))
        h = _xattn_ln(h, batch, seq, k2d, v2d, n_mem, bf(xa_wq[l]), bf(xa_wo[l]),
                      ln3_g[l:l + 1], ln3_b[l:l + 1])
        h = _ffn_ln(h, bf(w_ffn2_in[l]), bf(w_ffn2_out[l]), ln4_g[l:l + 1], ln4_b[l:l + 1])
    return h.reshape(batch, seq, D_MODEL)
```

```python
import functools

import numpy as np
import jax
import jax.numpy as jnp
from jax import lax
from jax.experimental import pallas as pl
from jax.experimental.pallas import tpu as pltpu

F32 = jnp.float32
BF16 = jnp.bfloat16

D_MODEL = 1024
DEPTH = 1
D_HGRN = 512
D_POOL = 512
HGRN_HEADS = 4
HEAD_DIM = 128
POOL_WINDOWS = (2, 4, 8, 16)
POOL_GROUP_DIM = 128
D_FF = 2816
D_IN_PROJ = 4 * D_HGRN + D_POOL
XA_HEADS = 4
XA_HEAD_DIM = 256
ALPHA = (2.0 * DEPTH) ** 0.25
LN_EPS = 1e-5
RMS_EPS = 1e-6

V7X_SUBLANES = 8
V7X_LANES = 128
V7X_MXU_DIM = 256
V7X_VMEM_LIMIT = 56 * 1024 * 1024

CHUNK = 64
SUB = V7X_SUBLANES
POOL_HALO = 16
FFN_TILE = 512
FFN_LN_PIECES = 4
MIX_TILE = 512
XA_TILE = 512
FF_CHUNKS = ((0, 1024), (1024, 1024), (2048, 768))


def _dot(a, b):
    return jnp.dot(a, b, preferred_element_type=F32)


def _dot_nt(a, b):
    return lax.dot_general(a, b, (((1,), (1,)), ((), ())), preferred_element_type=F32)


def _dot_tn(a, b):
    return lax.dot_general(a, b, (((0,), (0,)), ((), ())), preferred_element_type=F32)


def _silu(x):
    return x * jax.nn.sigmoid(x)


def _layernorm(y, g, b):
    mu = jnp.mean(y, axis=-1, keepdims=True)
    d = y - mu
    var = jnp.mean(d * d, axis=-1, keepdims=True)
    return d * lax.rsqrt(var + LN_EPS) * g + b


def _zero_after(v):
    r, c = v.shape
    m = jnp.max(v.reshape(r // V7X_SUBLANES, V7X_SUBLANES, c), axis=0)
    blocks = [m[:, j * V7X_LANES:(j + 1) * V7X_LANES] for j in range(c // V7X_LANES)]
    m = functools.reduce(jnp.maximum, blocks)
    bits = pltpu.bitcast(m, jnp.uint32)
    return pltpu.bitcast((bits >> 16) >> 16, F32)


def _resident(shape):
    zeros = (0,) * len(shape)
    return pl.BlockSpec(shape, lambda *_: zeros, pipeline_mode=pl.Buffered(1))


def _ffn_ln_kernel(x_ref, win_ref, wout_ref, g_ref, b_ref, o_ref, y_s, *, n_tiles):
    step = pl.program_id(0)
    ln_rows = FFN_TILE // FFN_LN_PIECES

    def ln_piece(p):
        rows = slice(p * ln_rows, (p + 1) * ln_rows)
        out = _layernorm(y_s[rows, :], g_ref[...], b_ref[...])
        o_ref[rows, :] = out
        return out

    @pl.when(step == 0)
    def _():
        y_s[...] = jnp.zeros_like(y_s)

    @pl.when(step < n_tiles)
    def _():
        x = x_ref[...]
        xb = x.astype(BF16)
        acc = None
        pieces = list(range(FFN_LN_PIECES))
        per_gap = -(-FFN_LN_PIECES // (len(FF_CHUNKS) - 1))
        lhs = xb
        for c0, cw in FF_CHUNKS:
            gate = _dot(lhs, win_ref[:, c0:c0 + cw])
            up = _dot(lhs, win_ref[:, D_FF + c0:D_FF + c0 + cw])
            act = (_silu(gate) * up).astype(BF16)
            part = _dot(act, wout_ref[c0:c0 + cw, :])
            acc = part if acc is None else acc + part
            if pieces:
                zeros = [_zero_after(ln_piece(pieces.pop(0))) for _ in range(min(per_gap, len(pieces)))]
                zero = sum(zeros[1:], zeros[0]).astype(BF16)
                lhs = xb + jnp.tile(zero, (FFN_TILE // V7X_SUBLANES, D_MODEL // V7X_LANES))
        assert not pieces
        y_s[...] = ALPHA * x + 0.5 * acc

    @pl.when(step == n_tiles)
    def _():
        for p in range(FFN_LN_PIECES):
            ln_piece(p)


def _ffn_ln(x2d, w_in, w_out, g, b):
    m = x2d.shape[0]
    assert m % FFN_TILE == 0
    n_tiles = m // FFN_TILE
    return pl.pallas_call(
        functools.partial(_ffn_ln_kernel, n_tiles=n_tiles),
        grid=(n_tiles + 1,),
        in_specs=[
            pl.BlockSpec((FFN_TILE, D_MODEL), lambda i: (jnp.minimum(i, n_tiles - 1), 0)),
            _resident((D_MODEL, 2 * D_FF)),
            _resident((D_FF, D_MODEL)),
            _resident((1, D_MODEL)),
            _resident((1, D_MODEL)),
        ],
        out_specs=pl.BlockSpec((FFN_TILE, D_MODEL), lambda i: (jnp.maximum(i - 1, 0), 0)),
        out_shape=jax.ShapeDtypeStruct((m, D_MODEL), F32),
        scratch_shapes=[pltpu.VMEM((FFN_TILE, D_MODEL), F32)],
        compiler_params=pltpu.CompilerParams(
            dimension_semantics=("arbitrary",), vmem_limit_bytes=V7X_VMEM_LIMIT),
        name="ffn_ln",
    )(x2d, w_in, w_out, g, b)


N_BLK = CHUNK // SUB
FAR_ORDER = (7, 6, 3, 5, 4, 2, 1)
FAR_COLS = 256
NEAR_LEVELS = (0, 1, 2, 4)
PV_COLS = FAR_COLS + len(NEAR_LEVELS) * CHUNK
assert sorted(FAR_ORDER) == list(range(1, N_BLK)) and SUB * sum(FAR_ORDER) <= FAR_COLS


def _chunk_constants():
    t = np.arange(CHUNK)[:, None]
    s = np.arange(CHUNK)[None, :]
    tri = (s <= t).astype(np.float32)
    mask = np.zeros((CHUNK, PV_COLS), np.float32)
    off = 0
    for i in FAR_ORDER:
        mask[:, off:off + SUB * i] = (t // SUB == i)
        off += SUB * i
    for n, b in enumerate(NEAR_LEVELS):
        valid = (t == s) if b == 0 else (((t // b) % 2 == 1) & ((s // b) == (t // b) - 1))
        mask[:, FAR_COLS + n * CHUNK:FAR_COLS + (n + 1) * CHUNK] = valid
    return jnp.asarray(tri, BF16), jnp.asarray(mask, F32)


def _split3(x):
    hi = x.astype(BF16)
    r = x - hi.astype(F32)
    mid = r.astype(BF16)
    lo = (r - mid.astype(F32)).astype(BF16)
    return hi, mid, lo


def _mixer_ln_kernel(hn_ref, hc_ref, win_ref, lb_ref, gn_ref, pw_ref, ps_ref, wout_ref, g_ref, b_ref,
                     tri_ref, mask_ref, o_ref,
                     q_a, k_a, f_a, i_a, cum_a, gate_a, vb_a,
                     q_s, k_s, f_s, i_s, cum_s, gate_s, vb_s, merged_s, y_s, state_s, *, n_seq):
    ts = hn_ref.shape[0]
    step = pl.program_id(0)
    seq_tile_next = step % n_seq
    seq_tile = (step + n_seq - 1) % n_seq
    handoff = ((q_a, q_s), (k_a, k_s), (f_a, f_s), (i_a, i_s), (cum_a, cum_s), (gate_a, gate_s), (vb_a, vb_s))

    @pl.when(step == 0)
    def _():
        state_s[...] = jnp.zeros_like(state_s)
        for src, _ in handoff:
            src[...] = jnp.zeros_like(src)

    half = ts // 2
    col_blk = V7X_MXU_DIM

    def copy_rows(c):
        rows = slice(c * CHUNK, (c + 1) * CHUNK)
        for src, dst in handoff[:-1]:
            dst[rows, :] = src[rows, :]
        if c == 0:
            vb_s[0:POOL_HALO, :] = vb_a[0:POOL_HALO, :]
        prow = slice(POOL_HALO + c * CHUNK, POOL_HALO + (c + 1) * CHUNK)
        vb_s[prow, :] = vb_a[prow, :]

    def pool_halo():
        vb_a[0:POOL_HALO, :] = jnp.where(seq_tile_next == 0, 0.0, vb_s[ts:ts + POOL_HALO, :])

    a = lb_ref[...]
    e = jnp.exp(a - jnp.max(a, axis=0, keepdims=True))
    lower = e[0:1, :] / jnp.sum(e, axis=0, keepdims=True)

    def in_proj(r, j):
        rows = slice(r * half, (r + 1) * half)
        x = _dot(hn_ref[rows, :].astype(BF16), win_ref[:, j * col_blk:(j + 1) * col_blk])
        kind, cb = divmod(j * col_blk, D_HGRN)
        cols = slice(cb, cb + col_blk)
        if kind == 0:
            q_a[rows, cols] = _silu(x)
        elif kind == 1:
            forget = lower[:, cols] + (1.0 - lower[:, cols]) * jax.nn.sigmoid(x)
            f_a[rows, cols] = forget
            k_a[rows, cols] = 1.0 - forget
            cum_a[rows, cols] = jnp.log(forget)
        elif kind == 2:
            i_a[rows, cols] = x
        elif kind == 3:
            gate_a[rows, cols] = _silu(x)
        else:
            vb_a[POOL_HALO + r * half:POOL_HALO + (r + 1) * half, cols] = x

    def cum_chunks(c0, n):
        tri = tri_ref[...]
        for c in range(c0, c0 + n):
            rows = slice(c * CHUNK, (c + 1) * CHUNK)
            hi, mid, lo = _split3(cum_a[rows, :])
            cum_a[rows, :] = _dot(tri, hi) + _dot(tri, mid) + _dot(tri, lo)

    sub_row = lax.broadcasted_iota(jnp.int32, (SUB, HEAD_DIM), 0)
    zero_blk = jnp.zeros((SUB, HEAD_DIM), F32)
    far_pad = jnp.zeros((FAR_COLS - SUB * sum(FAR_ORDER), HEAD_DIM), F32)
    gn = gn_ref[...]
    zero_kb = jnp.zeros((CHUNK, HEAD_DIM), BF16)

    def block_diag(a, b):
        return jnp.concatenate([jnp.concatenate([a, zero_kb], axis=1),
                                jnp.concatenate([zero_kb, b], axis=1)], axis=0)

    def head_cols(hd):
        return slice(hd * HEAD_DIM, (hd + 1) * HEAD_DIM)

    def scores_stage(c, hd):
        r0 = c * CHUNK
        rows = pl.ds(r0, CHUNK)
        cols = head_cols(hd)
        q = q_s[rows, cols]
        k = k_s[rows, cols]
        f = f_s[rows, cols]
        v = i_s[rows, cols]
        cum = cum_s[rows, cols]

        def cum_row(row):
            return jnp.broadcast_to(cum_s[pl.ds(r0 + row, 1), cols], (SUB, HEAD_DIM))

        bound = [zero_blk] + [cum_row(SUB * j - 1) for j in range(1, N_BLK + 1)]

        q_far = (q * jnp.exp(cum - jnp.concatenate(bound[:N_BLK], axis=0))).astype(BF16)
        k_parts, v_parts = [], []
        for i in FAR_ORDER:
            n = SUB * i
            k_parts.append(k[:n] * jnp.exp(jnp.concatenate([bound[i]] * i, axis=0) - cum[:n]))
            v_parts.append(v[:n])
        k_far = jnp.concatenate(k_parts + [far_pad], axis=0).astype(BF16)
        s_far = _dot_nt(q_far, k_far)

        ref_q2, ref_k2, ref_q4, ref_k4 = [], [], [], []
        for j in range(N_BLK):
            lo_b, hi_b = bound[j], bound[j + 1]
            m1, m3, m5 = (cum_row(SUB * j + r) for r in (1, 3, 5))
            ref_q4.append(jnp.where(sub_row < 4, lo_b, m3))
            ref_k4.append(jnp.where(sub_row < 4, m3, hi_b))
            ref_q2.append(jnp.where(sub_row < 2, lo_b, jnp.where(sub_row < 4, m1, jnp.where(sub_row < 6, m3, m5))))
            ref_k2.append(jnp.where(sub_row < 2, m1, jnp.where(sub_row < 4, m3, jnp.where(sub_row < 6, m5, hi_b))))
        cat = lambda parts: jnp.concatenate(parts, axis=0)
        q2 = q * jnp.exp(cum - cat(ref_q2))
        k2 = k * jnp.exp(cat(ref_k2) - cum)
        q4 = q * jnp.exp(cum - cat(ref_q4))
        k4 = k * jnp.exp(cat(ref_k4) - cum)
        kb = k.astype(BF16)
        s_a = _dot_nt(jnp.concatenate([q, q * f], axis=1).astype(BF16), block_diag(kb, kb))
        s_b = _dot_nt(jnp.concatenate([q2, q4], axis=1).astype(BF16),
                      block_diag(k2.astype(BF16), k4.astype(BF16)))

        total = bound[N_BLK]
        return dict(
            p=(jnp.concatenate([s_far, s_a, s_b], axis=1) * mask_ref[...]).astype(BF16),
            v_all=jnp.concatenate(v_parts + [far_pad] + [v] * len(NEAR_LEVELS), axis=0).astype(BF16),
            q_dec=(q * jnp.exp(cum)).astype(BF16),
            k_dec=(k * jnp.exp(jnp.concatenate([total] * N_BLK, axis=0) - cum)).astype(BF16),
            vb=v.astype(BF16),
            decay=jnp.exp(total[0:1, :]))

    def state_stage(c, hd, sc):
        st = state_s[hd]
        if c == 0:
            st = jnp.where(seq_tile == 0, 0.0, st)
        o = _dot(sc["p"], sc["v_all"]) + _dot_nt(sc["q_dec"], st.astype(BF16))
        state_s[hd] = st * sc["decay"] + _dot_tn(sc["vb"], sc["k_dec"])
        return o

    def norm_stage(c, hd, o):
        rows = pl.ds(c * CHUNK, CHUNK)
        cols = head_cols(hd)
        o = o * lax.rsqrt(jnp.mean(o * o, axis=-1, keepdims=True) + RMS_EPS) * gn
        merged_s[rows, cols] = (o * gate_s[rows, cols]).astype(BF16)

    quarter = ts // 4

    def pool_rows(qd):
        r0, nr = qd * quarter, quarter
        pos = seq_tile * ts + r0 + lax.broadcasted_iota(jnp.int32, (nr, 1), 0) + 1
        for grp, win in enumerate(POOL_WINDOWS):
            cols = slice(grp * POOL_GROUP_DIM, (grp + 1) * POOL_GROUP_DIM)
            ext = vb_s[r0:r0 + POOL_HALO + nr, cols]
            wsum = ext
            span = 1
            while span < win:
                wsum = wsum + pltpu.roll(wsum, span, 0)
                span *= 2
            cur = ext[POOL_HALO:]
            count = jnp.minimum(pos, win).astype(F32)
            pooled = wsum[POOL_HALO:] / count - cur
            mixed = _dot(pooled.astype(BF16), pw_ref[grp]) * ps_ref[:, cols]
            merged_s[r0:r0 + nr, D_HGRN + grp * POOL_GROUP_DIM:D_HGRN + (grp + 1) * POOL_GROUP_DIM] = (
                mixed.astype(BF16))

    def out_proj(qd, n2):
        rows = slice(qd * quarter, (qd + 1) * quarter)
        cols = slice(n2 * (D_MODEL // 2), (n2 + 1) * (D_MODEL // 2))
        y_s[rows, cols] = _dot(merged_s[rows, :], wout_ref[:, cols])

    def out_ln(qd):
        rows = slice(qd * quarter, (qd + 1) * quarter)
        o_ref[rows, :] = _layernorm(ALPHA * hc_ref[rows, :] + y_s[rows, :], g_ref[...], b_ref[...])

    n_chunks = ts // CHUNK
    n_slots = n_chunks * HGRN_HEADS
    P = functools.partial
    COPY, PROJ, CUM, POOL, OUT, LN = 100, 256, 192, 200, 256, 150
    early = [(0, COPY, P(copy_rows, c)) for c in range(2, n_chunks)] + [(0, 10, pool_halo)]
    for r in range(2):
        blocks = list(range(D_IN_PROJ // col_blk))
        forget_blocks = [j for j in blocks if j * col_blk // D_HGRN == 1]
        first = 2 + 2 * r
        early += [(first, PROJ, P(in_proj, r, j)) for j in forget_blocks]
        early += [(first, CUM, P(cum_chunks, r * n_chunks // 2 + 2 * i, 2)) for i in range(n_chunks // 4)]
        early += [(first, PROJ, P(in_proj, r, j)) for j in blocks if j not in forget_blocks]
    late = []
    for qd in range(4):
        first = (qd + 1) * (n_chunks // 4) * HGRN_HEADS
        late += [(first, POOL, P(pool_rows, qd)), (first, OUT, P(out_proj, qd, 0)),
                 (first, OUT, P(out_proj, qd, 1)), (first, LN, P(out_ln, qd))]
    total = sum(cost for _, cost, _ in early + late)
    copy_rows(0)
    copy_rows(1)
    done = 0
    ch = lambda slot: (slot // HGRN_HEADS, slot % HGRN_HEADS)
    scores, outs = {}, {}
    for slot in range(n_slots + 2):
        if slot < n_slots:
            scores[slot] = scores_stage(*ch(slot))
        if 1 <= slot <= n_slots:
            outs[slot - 1] = state_stage(*ch(slot - 1), scores.pop(slot - 1))
        if slot >= 2:
            norm_stage(*ch(slot - 2), outs.pop(slot - 2))
        while done < total * (slot + 1) / n_slots:
            queue = late if late and late[0][0] <= slot - 1 else early
            if not queue or queue[0][0] > slot - 1:
                break
            _, cost, fn = queue.pop(0)
            fn()
            done += cost
    for _, _, fn in early + late:
        fn()


def _mixer_ln(h2d, batch, seq, w_in, lb, gnorm, pool_w, pool_scale, w_out, g, b):
    ts = MIX_TILE
    assert seq % ts == 0 and ts % CHUNK == 0
    n_seq = seq // ts
    n_tiles = batch * n_seq
    tri, mask = _chunk_constants()
    next_tile = pl.BlockSpec((ts, D_MODEL), lambda i: (jnp.minimum(i, n_tiles - 1), 0))
    done_tile = pl.BlockSpec((ts, D_MODEL), lambda i: (jnp.maximum(i - 1, 0), 0))
    proj_set = [pltpu.VMEM((ts, D_HGRN), F32)] * 6 + [pltpu.VMEM((ts + POOL_HALO, D_POOL), F32)]
    return pl.pallas_call(
        functools.partial(_mixer_ln_kernel, n_seq=n_seq),
        grid=(n_tiles + 1,),
        in_specs=[
            next_tile,
            done_tile,
            _resident(w_in.shape),
            _resident(lb.shape),
            _resident(gnorm.shape),
            _resident(pool_w.shape),
            _resident(pool_scale.shape),
            _resident(w_out.shape),
            _resident(g.shape),
            _resident(b.shape),
            _resident(tri.shape),
            _resident(mask.shape),
        ],
        out_specs=done_tile,
        out_shape=jax.ShapeDtypeStruct(h2d.shape, F32),
        scratch_shapes=proj_set + proj_set + [
            pltpu.VMEM((ts, D_HGRN + D_POOL), BF16),
            pltpu.VMEM((ts, D_MODEL), F32),
            pltpu.VMEM((HGRN_HEADS, HEAD_DIM, HEAD_DIM), F32),
        ],
        compiler_params=pltpu.CompilerParams(
            dimension_semantics=("arbitrary",), vmem_limit_bytes=V7X_VMEM_LIMIT),
        name="mixer_ln",
    )(h2d, h2d, w_in, lb, gnorm, pool_w, pool_scale, w_out, g, b, tri, mask)


def _kv_proj_kernel(mem_ref, wk_ref, wv_ref, k_ref, v_ref):
    mb = mem_ref[...].astype(BF16)
    k_ref[...] = _dot(mb, wk_ref[...]).astype(BF16)
    v_ref[...] = _dot(mb, wv_ref[...]).astype(BF16)


def _kv_proj(mem2d, n_mem, wk, wv):
    rows = mem2d.shape[0]
    tile = pl.BlockSpec((n_mem, D_MODEL), lambda i: (i, 0))
    return pl.pallas_call(
        _kv_proj_kernel,
        grid=(rows // n_mem,),
        in_specs=[tile, _resident(wk.shape), _resident(wv.shape)],
        out_specs=[tile, tile],
        out_shape=[jax.ShapeDtypeStruct((rows, D_MODEL), BF16)] * 2,
        compiler_params=pltpu.CompilerParams(
            dimension_semantics=("arbitrary",), vmem_limit_bytes=V7X_VMEM_LIMIT),
        name="kv_proj",
    )(mem2d, wk, wv)


def _xattn_ln_kernel(h_ref, k_ref, v_ref, wq_ref, wo_ref, g_ref, b_ref, o_ref):
    h = h_ref[...]
    q = _dot(h.astype(BF16), wq_ref[...])
    heads = []
    for hd in range(XA_HEADS):
        cols = slice(hd * XA_HEAD_DIM, (hd + 1) * XA_HEAD_DIM)
        s = _dot_nt(q[:, cols].astype(BF16), k_ref[:, cols]) * (XA_HEAD_DIM ** -0.5)
        s = s - jnp.max(s, axis=-1, keepdims=True)
        p = jnp.exp(s)
        p = p / jnp.sum(p, axis=-1, keepdims=True)
        heads.append(_dot(p.astype(BF16), v_ref[:, cols]).astype(BF16))
    y = _dot(jnp.concatenate(heads, axis=-1), wo_ref[...])
    o_ref[...] = _layernorm(ALPHA * h + y, g_ref[...], b_ref[...])


def _xattn_ln(h2d, batch, seq, k2d, v2d, n_mem, wq, wo, g, b):
    ts = XA_TILE
    assert seq % ts == 0
    n_seq = seq // ts
    tile = pl.BlockSpec((ts, D_MODEL), lambda bi, si: (bi * n_seq + si, 0))
    mem_tile = pl.BlockSpec((n_mem, D_MODEL), lambda bi, si: (bi, 0))
    return pl.pallas_call(
        _xattn_ln_kernel,
        grid=(batch, n_seq),
        in_specs=[tile, mem_tile, mem_tile, _resident(wq.shape), _resident(wo.shape),
                  _resident(g.shape), _resident(b.shape)],
        out_specs=tile,
        out_shape=jax.ShapeDtypeStruct(h2d.shape, F32),
        compiler_params=pltpu.CompilerParams(
            dimension_semantics=("arbitrary", "arbitrary"), vmem_limit_bytes=V7X_VMEM_LIMIT),
        name="xattn_ln",
    )(h2d, k2d, v2d, wq, wo, g, b)


def kernel(x, mem, w_ffn1_in, w_ffn1_out, ln1_g, ln1_b, w_mix_in, hgrn_lb, hgrn_gnorm, pool_w, pool_scale, w_mix_out, ln2_g, ln2_b, xa_wq, xa_wk, xa_wv, xa_wo, ln3_g, ln3_b, w_ffn2_in, w_ffn2_out, ln4_g, ln4_b):
    batch, seq, _ = x.shape
    n_mem = mem.shape[1]
    assert w_ffn1_in.shape[0] == DEPTH == 1
    bf = lambda w: w.astype(BF16)
    h = x.reshape(batch * seq, D_MODEL)
    mem2d = mem.reshape(batch * n_mem, D_MODEL)
    for l in range(DEPTH):
        h = _ffn_ln(h, bf(w_ffn1_in[l]), bf(w_ffn1_out[l]), ln1_g[l:l + 1], ln1_b[l:l + 1])
        h = _mixer_ln(h, batch, seq, bf(w_mix_in[l]), hgrn_lb, hgrn_gnorm[l:l + 1], bf(pool_w[l]),
                      pool_scale[l:l + 1], bf(w_mix_out[l]), ln2_g[l:l + 1], ln2_b[l:l + 1])
        k2d, v2d = _kv_proj(mem2d, n_mem, bf(xa_wk[l]), bf(xa_wv[l]))
        h = _xattn_ln(h, batch, seq, k2d, v2d, n_mem, bf(xa_wq[l]), bf(xa_wo[l]),
                      ln3_g[l:l + 1], ln3_b[l:l + 1])
        h = _ffn_ln(h, bf(w_ffn2_in[l]), bf(w_ffn2_out[l]), ln4_g[l:l + 1], ln4_b[l:l + 1])
    return h.reshape(batch, seq, D_MODEL)
```

```python
import functools

import numpy as np
import jax
import jax.numpy as jnp
from jax import lax
from jax.experimental import pallas as pl
from jax.experimental.pallas import tpu as pltpu

F32 = jnp.float32
BF16 = jnp.bfloat16

D_MODEL = 1024
DEPTH = 1
D_HGRN = 512
D_POOL = 512
HGRN_HEADS = 4
HEAD_DIM = 128
POOL_WINDOWS = (2, 4, 8, 16)
POOL_GROUP_DIM = 128
D_FF = 2816
D_IN_PROJ = 4 * D_HGRN + D_POOL
XA_HEADS = 4
XA_HEAD_DIM = 256
ALPHA = (2.0 * DEPTH) ** 0.25
LN_EPS = 1e-5
RMS_EPS = 1e-6

V7X_SUBLANES = 8
V7X_LANES = 128
V7X_MXU_DIM = 256
V7X_VMEM_LIMIT = 56 * 1024 * 1024

CHUNK = 64
SUB = V7X_SUBLANES
POOL_HALO = 16
FFN_TILE = 512
FFN_LN_PIECES = 4
MIX_TILE = 512
XA_TILE = 512
XA_LN_PIECES = 4
XA_LN_FIRST = 2
FF_CHUNKS = ((0, 1024), (1024, 1024), (2048, 768))


def _dot(a, b):
    return jnp.dot(a, b, preferred_element_type=F32)


def _dot_nt(a, b):
    return lax.dot_general(a, b, (((1,), (1,)), ((), ())), preferred_element_type=F32)


def _dot_tn(a, b):
    return lax.dot_general(a, b, (((0,), (0,)), ((), ())), preferred_element_type=F32)


def _silu(x):
    return x * jax.nn.sigmoid(x)


def _layernorm(y, g, b):
    mu = jnp.mean(y, axis=-1, keepdims=True)
    d = y - mu
    var = jnp.mean(d * d, axis=-1, keepdims=True)
    return d * lax.rsqrt(var + LN_EPS) * g + b


def _zero_after(v):
    r, c = v.shape
    m = jnp.max(v.reshape(r // V7X_SUBLANES, V7X_SUBLANES, c), axis=0)
    blocks = [m[:, j * V7X_LANES:(j + 1) * V7X_LANES] for j in range(c // V7X_LANES)]
    m = functools.reduce(jnp.maximum, blocks)
    bits = pltpu.bitcast(m, jnp.uint32)
    return pltpu.bitcast((bits >> 16) >> 16, F32)


def _resident(shape):
    zeros = (0,) * len(shape)
    return pl.BlockSpec(shape, lambda *_: zeros, pipeline_mode=pl.Buffered(1))


def _ffn_ln_kernel(x_ref, win_ref, wout_ref, g_ref, b_ref, o_ref, y_s, *, n_tiles):
    step = pl.program_id(0)
    ln_rows = FFN_TILE // FFN_LN_PIECES

    def ln_piece(p):
        rows = slice(p * ln_rows, (p + 1) * ln_rows)
        out = _layernorm(y_s[rows, :], g_ref[...], b_ref[...])
        o_ref[rows, :] = out
        return out

    @pl.when(step == 0)
    def _():
        y_s[...] = jnp.zeros_like(y_s)

    @pl.when(step < n_tiles)
    def _():
        x = x_ref[...]
        xb = x.astype(BF16)
        acc = None
        pieces = list(range(FFN_LN_PIECES))
        per_gap = -(-FFN_LN_PIECES // (len(FF_CHUNKS) - 1))
        lhs = xb
        for c0, cw in FF_CHUNKS:
            gate = _dot(lhs, win_ref[:, c0:c0 + cw])
            up = _dot(lhs, win_ref[:, D_FF + c0:D_FF + c0 + cw])
            act = (_silu(gate) * up).astype(BF16)
            part = _dot(act, wout_ref[c0:c0 + cw, :])
            acc = part if acc is None else acc + part
            if pieces:
                zeros = [_zero_after(ln_piece(pieces.pop(0))) for _ in range(min(per_gap, len(pieces)))]
                zero = sum(zeros[1:], zeros[0]).astype(BF16)
                lhs = xb + jnp.tile(zero, (FFN_TILE // V7X_SUBLANES, D_MODEL // V7X_LANES))
        assert not pieces
        y_s[...] = ALPHA * x + 0.5 * acc

    @pl.when(step == n_tiles)
    def _():
        for p in range(FFN_LN_PIECES):
            ln_piece(p)


def _ffn_ln(x2d, w_in, w_out, g, b):
    m = x2d.shape[0]
    assert m % FFN_TILE == 0
    n_tiles = m // FFN_TILE
    return pl.pallas_call(
        functools.partial(_ffn_ln_kernel, n_tiles=n_tiles),
        grid=(n_tiles + 1,),
        in_specs=[
            pl.BlockSpec((FFN_TILE, D_MODEL), lambda i: (jnp.minimum(i, n_tiles - 1), 0)),
            _resident((D_MODEL, 2 * D_FF)),
            _resident((D_FF, D_MODEL)),
            _resident((1, D_MODEL)),
            _resident((1, D_MODEL)),
        ],
        out_specs=pl.BlockSpec((FFN_TILE, D_MODEL), lambda i: (jnp.maximum(i - 1, 0), 0)),
        out_shape=jax.ShapeDtypeStruct((m, D_MODEL), F32),
        scratch_shapes=[pltpu.VMEM((FFN_TILE, D_MODEL), F32)],
        compiler_params=pltpu.CompilerParams(
            dimension_semantics=("arbitrary",), vmem_limit_bytes=V7X_VMEM_LIMIT),
        name="ffn_ln",
    )(x2d, w_in, w_out, g, b)


N_BLK = CHUNK // SUB
FAR_ORDER = (7, 6, 3, 5, 4, 2, 1)
FAR_COLS = 256
NEAR_LEVELS = (0, 1, 2, 4)
PV_COLS = FAR_COLS + len(NEAR_LEVELS) * CHUNK
assert sorted(FAR_ORDER) == list(range(1, N_BLK)) and SUB * sum(FAR_ORDER) <= FAR_COLS


def _chunk_constants():
    t = np.arange(CHUNK)[:, None]
    s = np.arange(CHUNK)[None, :]
    tri = (s <= t).astype(np.float32)
    mask = np.zeros((CHUNK, PV_COLS), np.float32)
    off = 0
    for i in FAR_ORDER:
        mask[:, off:off + SUB * i] = (t // SUB == i)
        off += SUB * i
    for n, b in enumerate(NEAR_LEVELS):
        valid = (t == s) if b == 0 else (((t // b) % 2 == 1) & ((s // b) == (t // b) - 1))
        mask[:, FAR_COLS + n * CHUNK:FAR_COLS + (n + 1) * CHUNK] = valid
    return jnp.asarray(tri, BF16), jnp.asarray(mask, F32)


def _split3(x):
    hi = x.astype(BF16)
    r = x - hi.astype(F32)
    mid = r.astype(BF16)
    lo = (r - mid.astype(F32)).astype(BF16)
    return hi, mid, lo


def _mixer_ln_kernel(hn_ref, hc_ref, win_ref, lb_ref, gn_ref, pw_ref, ps_ref, wout_ref, g_ref, b_ref,
                     tri_ref, mask_ref, o_ref,
                     q_a, k_a, f_a, i_a, cum_a, gate_a, vb_a,
                     q_s, k_s, f_s, i_s, cum_s, gate_s, vb_s, merged_s, y_s, state_s, *, n_seq):
    ts = hn_ref.shape[0]
    step = pl.program_id(0)
    seq_tile_next = step % n_seq
    seq_tile = (step + n_seq - 1) % n_seq
    handoff = ((q_a, q_s), (k_a, k_s), (f_a, f_s), (i_a, i_s), (cum_a, cum_s), (gate_a, gate_s), (vb_a, vb_s))

    @pl.when(step == 0)
    def _():
        state_s[...] = jnp.zeros_like(state_s)
        for src, _ in handoff:
            src[...] = jnp.zeros_like(src)

    half = ts // 2
    col_blk = V7X_MXU_DIM

    def copy_rows(c):
        rows = slice(c * CHUNK, (c + 1) * CHUNK)
        for src, dst in handoff[:-1]:
            dst[rows, :] = src[rows, :]
        if c == 0:
            vb_s[0:POOL_HALO, :] = vb_a[0:POOL_HALO, :]
        prow = slice(POOL_HALO + c * CHUNK, POOL_HALO + (c + 1) * CHUNK)
        vb_s[prow, :] = vb_a[prow, :]

    def pool_halo():
        vb_a[0:POOL_HALO, :] = jnp.where(seq_tile_next == 0, 0.0, vb_s[ts:ts + POOL_HALO, :])

    a = lb_ref[...]
    e = jnp.exp(a - jnp.max(a, axis=0, keepdims=True))
    lower = e[0:1, :] / jnp.sum(e, axis=0, keepdims=True)

    def in_proj(r, j):
        rows = slice(r * half, (r + 1) * half)
        x = _dot(hn_ref[rows, :].astype(BF16), win_ref[:, j * col_blk:(j + 1) * col_blk])
        kind, cb = divmod(j * col_blk, D_HGRN)
        cols = slice(cb, cb + col_blk)
        if kind == 0:
            q_a[rows, cols] = _silu(x)
        elif kind == 1:
            forget = lower[:, cols] + (1.0 - lower[:, cols]) * jax.nn.sigmoid(x)
            f_a[rows, cols] = forget
            k_a[rows, cols] = 1.0 - forget
            cum_a[rows, cols] = jnp.log(forget)
        elif kind == 2:
            i_a[rows, cols] = x
        elif kind == 3:
            gate_a[rows, cols] = _silu(x)
        else:
            vb_a[POOL_HALO + r * half:POOL_HALO + (r + 1) * half, cols] = x

    def cum_chunks(c0, n):
        tri = tri_ref[...]
        for c in range(c0, c0 + n):
            rows = slice(c * CHUNK, (c + 1) * CHUNK)
            hi, mid, lo = _split3(cum_a[rows, :])
            cum_a[rows, :] = _dot(tri, hi) + _dot(tri, mid) + _dot(tri, lo)

    sub_row = lax.broadcasted_iota(jnp.int32, (SUB, HEAD_DIM), 0)
    zero_blk = jnp.zeros((SUB, HEAD_DIM), F32)
    far_pad = jnp.zeros((FAR_COLS - SUB * sum(FAR_ORDER), HEAD_DIM), F32)
    gn = gn_ref[...]
    zero_kb = jnp.zeros((CHUNK, HEAD_DIM), BF16)

    def block_diag(a, b):
        return jnp.concatenate([jnp.concatenate([a, zero_kb], axis=1),
                                jnp.concatenate([zero_kb, b], axis=1)], axis=0)

    def head_cols(hd):
        return slice(hd * HEAD_DIM, (hd + 1) * HEAD_DIM)

    def scores_stage(c, hd):
        r0 = c * CHUNK
        rows = pl.ds(r0, CHUNK)
        cols = head_cols(hd)
        q = q_s[rows, cols]
        k = k_s[rows, cols]
        f = f_s[rows, cols]
        v = i_s[rows, cols]
        cum = cum_s[rows, cols]

        def cum_row(row):
            return jnp.broadcast_to(cum_s[pl.ds(r0 + row, 1), cols], (SUB, HEAD_DIM))

        bound = [zero_blk] + [cum_row(SUB * j - 1) for j in range(1, N_BLK + 1)]

        q_far = (q * jnp.exp(cum - jnp.concatenate(bound[:N_BLK], axis=0))).astype(BF16)
        k_parts, v_parts = [], []
        for i in FAR_ORDER:
            n = SUB * i
            k_parts.append(k[:n] * jnp.exp(jnp.concatenate([bound[i]] * i, axis=0) - cum[:n]))
            v_parts.append(v[:n])
        k_far = jnp.concatenate(k_parts + [far_pad], axis=0).astype(BF16)
        s_far = _dot_nt(q_far, k_far)

        ref_q2, ref_k2, ref_q4, ref_k4 = [], [], [], []
        for j in range(N_BLK):
            lo_b, hi_b = bound[j], bound[j + 1]
            m1, m3, m5 = (cum_row(SUB * j + r) for r in (1, 3, 5))
            ref_q4.append(jnp.where(sub_row < 4, lo_b, m3))
            ref_k4.append(jnp.where(sub_row < 4, m3, hi_b))
            ref_q2.append(jnp.where(sub_row < 2, lo_b, jnp.where(sub_row < 4, m1, jnp.where(sub_row < 6, m3, m5))))
            ref_k2.append(jnp.where(sub_row < 2, m1, jnp.where(sub_row < 4, m3, jnp.where(sub_row < 6, m5, hi_b))))
        cat = lambda parts: jnp.concatenate(parts, axis=0)
        q2 = q * jnp.exp(cum - cat(ref_q2))
        k2 = k * jnp.exp(cat(ref_k2) - cum)
        q4 = q * jnp.exp(cum - cat(ref_q4))
        k4 = k * jnp.exp(cat(ref_k4) - cum)
        kb = k.astype(BF16)
        s_a = _dot_nt(jnp.concatenate([q, q * f], axis=1).astype(BF16), block_diag(kb, kb))
        s_b = _dot_nt(jnp.concatenate([q2, q4], axis=1).astype(BF16),
                      block_diag(k2.astype(BF16), k4.astype(BF16)))

        total = bound[N_BLK]
        return dict(
            p=(jnp.concatenate([s_far, s_a, s_b], axis=1) * mask_ref[...]).astype(BF16),
            v_all=jnp.concatenate(v_parts + [far_pad] + [v] * len(NEAR_LEVELS), axis=0).astype(BF16),
            q_dec=(q * jnp.exp(cum)).astype(BF16),
            k_dec=(k * jnp.exp(jnp.concatenate([total] * N_BLK, axis=0) - cum)).astype(BF16),
            vb=v.astype(BF16),
            decay=jnp.exp(total[0:1, :]))

    def state_stage(c, hd, sc):
        st = state_s[hd]
        if c == 0:
            st = jnp.where(seq_tile == 0, 0.0, st)
        o = _dot(sc["p"], sc["v_all"]) + _dot_nt(sc["q_dec"], st.astype(BF16))
        state_s[hd] = st * sc["decay"] + _dot_tn(sc["vb"], sc["k_dec"])
        return o

    def norm_stage(c, hd, o):
        rows = pl.ds(c * CHUNK, CHUNK)
        cols = head_cols(hd)
        o = o * lax.rsqrt(jnp.mean(o * o, axis=-1, keepdims=True) + RMS_EPS) * gn
        merged_s[rows, cols] = (o * gate_s[rows, cols]).astype(BF16)

    quarter = ts // 4

    def pool_rows(qd):
        r0, nr = qd * quarter, quarter
        pos = seq_tile * ts + r0 + lax.broadcasted_iota(jnp.int32, (nr, 1), 0) + 1
        for grp, win in enumerate(POOL_WINDOWS):
            cols = slice(grp * POOL_GROUP_DIM, (grp + 1) * POOL_GROUP_DIM)
            ext = vb_s[r0:r0 + POOL_HALO + nr, cols]
            wsum = ext
            span = 1
            while span < win:
                wsum = wsum + pltpu.roll(wsum, span, 0)
                span *= 2
            cur = ext[POOL_HALO:]
            count = jnp.minimum(pos, win).astype(F32)
            pooled = wsum[POOL_HALO:] / count - cur
            mixed = _dot(pooled.astype(BF16), pw_ref[grp]) * ps_ref[:, cols]
            merged_s[r0:r0 + nr, D_HGRN + grp * POOL_GROUP_DIM:D_HGRN + (grp + 1) * POOL_GROUP_DIM] = (
                mixed.astype(BF16))

    def out_proj(qd, n2):
        rows = slice(qd * quarter, (qd + 1) * quarter)
        cols = slice(n2 * (D_MODEL // 2), (n2 + 1) * (D_MODEL // 2))
        y_s[rows, cols] = _dot(merged_s[rows, :], wout_ref[:, cols])

    def out_ln(qd):
        rows = slice(qd * quarter, (qd + 1) * quarter)
        o_ref[rows, :] = _layernorm(ALPHA * hc_ref[rows, :] + y_s[rows, :], g_ref[...], b_ref[...])

    n_chunks = ts // CHUNK
    n_slots = n_chunks * HGRN_HEADS
    P = functools.partial
    COPY, PROJ, CUM, POOL, OUT, LN = 100, 256, 192, 200, 256, 150
    early = [(0, COPY, P(copy_rows, c)) for c in range(2, n_chunks)] + [(0, 10, pool_halo)]
    for r in range(2):
        blocks = list(range(D_IN_PROJ // col_blk))
        forget_blocks = [j for j in blocks if j * col_blk // D_HGRN == 1]
        first = 2 + 2 * r
        early += [(first, PROJ, P(in_proj, r, j)) for j in forget_blocks]
        early += [(first, CUM, P(cum_chunks, r * n_chunks // 2 + 2 * i, 2)) for i in range(n_chunks // 4)]
        early += [(first, PROJ, P(in_proj, r, j)) for j in blocks if j not in forget_blocks]
    late = []
    for qd in range(4):
        first = (qd + 1) * (n_chunks // 4) * HGRN_HEADS
        late += [(first, POOL, P(pool_rows, qd)), (first, OUT, P(out_proj, qd, 0)),
                 (first, OUT, P(out_proj, qd, 1)), (first, LN, P(out_ln, qd))]
    total = sum(cost for _, cost, _ in early + late)
    copy_rows(0)
    copy_rows(1)
    done = 0
    ch = lambda slot: (slot // HGRN_HEADS, slot % HGRN_HEADS)
    scores, outs = {}, {}
    for slot in range(n_slots + 2):
        if slot < n_slots:
            scores[slot] = scores_stage(*ch(slot))
        if 1 <= slot <= n_slots:
            outs[slot - 1] = state_stage(*ch(slot - 1), scores.pop(slot - 1))
        if slot >= 2:
            norm_stage(*ch(slot - 2), outs.pop(slot - 2))
        while done < total * (slot + 1) / n_slots:
            queue = late if late and late[0][0] <= slot - 1 else early
            if not queue or queue[0][0] > slot - 1:
                break
            _, cost, fn = queue.pop(0)
            fn()
            done += cost
    for _, _, fn in early + late:
        fn()


def _mixer_ln(h2d, batch, seq, w_in, lb, gnorm, pool_w, pool_scale, w_out, g, b):
    ts = MIX_TILE
    assert seq % ts == 0 and ts % CHUNK == 0
    n_seq = seq // ts
    n_tiles = batch * n_seq
    tri, mask = _chunk_constants()
    next_tile = pl.BlockSpec((ts, D_MODEL), lambda i: (jnp.minimum(i, n_tiles - 1), 0))
    done_tile = pl.BlockSpec((ts, D_MODEL), lambda i: (jnp.maximum(i - 1, 0), 0))
    proj_set = [pltpu.VMEM((ts, D_HGRN), F32)] * 6 + [pltpu.VMEM((ts + POOL_HALO, D_POOL), F32)]
    return pl.pallas_call(
        functools.partial(_mixer_ln_kernel, n_seq=n_seq),
        grid=(n_tiles + 1,),
        in_specs=[
            next_tile,
            done_tile,
            _resident(w_in.shape),
            _resident(lb.shape),
            _resident(gnorm.shape),
            _resident(pool_w.shape),
            _resident(pool_scale.shape),
            _resident(w_out.shape),
            _resident(g.shape),
            _resident(b.shape),
            _resident(tri.shape),
            _resident(mask.shape),
        ],
        out_specs=done_tile,
        out_shape=jax.ShapeDtypeStruct(h2d.shape, F32),
        scratch_shapes=proj_set + proj_set + [
            pltpu.VMEM((ts, D_HGRN + D_POOL), BF16),
            pltpu.VMEM((ts, D_MODEL), F32),
            pltpu.VMEM((HGRN_HEADS, HEAD_DIM, HEAD_DIM), F32),
        ],
        compiler_params=pltpu.CompilerParams(
            dimension_semantics=("arbitrary",), vmem_limit_bytes=V7X_VMEM_LIMIT),
        name="mixer_ln",
    )(h2d, h2d, w_in, lb, gnorm, pool_w, pool_scale, w_out, g, b, tri, mask)


def _kv_proj_kernel(mem_ref, wk_ref, wv_ref, k_ref, v_ref):
    mb = mem_ref[...].astype(BF16)
    k_ref[...] = _dot(mb, wk_ref[...]).astype(BF16)
    v_ref[...] = _dot(mb, wv_ref[...]).astype(BF16)


def _kv_proj(mem2d, n_mem, wk, wv):
    rows = mem2d.shape[0]
    tile = pl.BlockSpec((n_mem, D_MODEL), lambda i: (i, 0))
    return pl.pallas_call(
        _kv_proj_kernel,
        grid=(rows // n_mem,),
        in_specs=[tile, _resident(wk.shape), _resident(wv.shape)],
        out_specs=[tile, tile],
        out_shape=[jax.ShapeDtypeStruct((rows, D_MODEL), BF16)] * 2,
        compiler_params=pltpu.CompilerParams(
            dimension_semantics=("arbitrary",), vmem_limit_bytes=V7X_VMEM_LIMIT),
        name="kv_proj",
    )(mem2d, wk, wv)


def _xattn_ln_kernel(h_ref, k_ref, v_ref, wq_ref, wo_ref, g_ref, b_ref, o_ref, y_s, *, n_tiles):
    step = pl.program_id(0)
    ts = h_ref.shape[0]
    ln_rows = ts // XA_LN_PIECES

    def ln_piece(p):
        rows = slice(p * ln_rows, (p + 1) * ln_rows)
        out = _layernorm(y_s[rows, :], g_ref[...], b_ref[...])
        o_ref[rows, :] = out
        return out

    @pl.when(step == 0)
    def _():
        y_s[...] = jnp.zeros_like(y_s)

    @pl.when(step < n_tiles)
    def _():
        h = h_ref[...]
        hb = h.astype(BF16)
        q = _dot(hb, wq_ref[...])
        pieces = list(range(XA_LN_PIECES))
        heads = []
        for hd in range(XA_HEADS):
            cols = slice(hd * XA_HEAD_DIM, (hd + 1) * XA_HEAD_DIM)
            qh = q[:, cols].astype(BF16)
            if pieces:
                take = min(len(pieces), XA_LN_FIRST if hd == 0 else 1)
                zeros = [_zero_after(ln_piece(pieces.pop(0))) for _ in range(take)]
                zero = sum(zeros[1:], zeros[0]).astype(BF16)
                qh = qh + jnp.tile(zero, (ts // V7X_SUBLANES, XA_HEAD_DIM // V7X_LANES))
            s = _dot_nt(qh, k_ref[:, cols]) * (XA_HEAD_DIM ** -0.5)
            s = s - jnp.max(s, axis=-1, keepdims=True)
            p = jnp.exp(s)
            p = p / jnp.sum(p, axis=-1, keepdims=True)
            heads.append(_dot(p.astype(BF16), v_ref[:, cols]).astype(BF16))
        assert not pieces
        y_s[...] = ALPHA * h + _dot(jnp.concatenate(heads, axis=-1), wo_ref[...])

    @pl.when(step == n_tiles)
    def _():
        for p in range(XA_LN_PIECES):
            ln_piece(p)


def _xattn_ln(h2d, batch, seq, k2d, v2d, n_mem, wq, wo, g, b):
    ts = XA_TILE
    assert seq % ts == 0
    n_seq = seq // ts
    n_tiles = batch * n_seq
    attended = lambda i: jnp.minimum(i, n_tiles - 1)
    tile = pl.BlockSpec((ts, D_MODEL), lambda i: (attended(i), 0))
    mem_tile = pl.BlockSpec((n_mem, D_MODEL), lambda i: (attended(i) // n_seq, 0))
    return pl.pallas_call(
        functools.partial(_xattn_ln_kernel, n_tiles=n_tiles),
        grid=(n_tiles + 1,),
        in_specs=[tile, mem_tile, mem_tile, _resident(wq.shape), _resident(wo.shape),
                  _resident(g.shape), _resident(b.shape)],
        out_specs=pl.BlockSpec((ts, D_MODEL), lambda i: (jnp.maximum(i - 1, 0), 0)),
        out_shape=jax.ShapeDtypeStruct(h2d.shape, F32),
        scratch_shapes=[pltpu.VMEM((ts, D_MODEL), F32)],
        compiler_params=pltpu.CompilerParams(
            dimension_semantics=("arbitrary",), vmem_limit_bytes=V7X_VMEM_LIMIT),
        name="xattn_ln",
    )(h2d, k2d, v2d, wq, wo, g, b)


def kernel(x, mem, w_ffn1_in, w_ffn1_out, ln1_g, ln1_b, w_mix_in, hgrn_lb, hgrn_gnorm, pool_w, pool_scale, w_mix_out, ln2_g, ln2_b, xa_wq, xa_wk, xa_wv, xa_wo, ln3_g, ln3_b, w_ffn2_in, w_ffn2_out, ln4_g, ln4_b):
    batch, seq, _ = x.shape
    n_mem = mem.shape[1]
    assert w_ffn1_in.shape[0] == DEPTH == 1
    bf = lambda w: w.astype(BF16)
    h = x.reshape(batch * seq, D_MODEL)
    mem2d = mem.reshape(batch * n_mem, D_MODEL)
    for l in range(DEPTH):
        h = _ffn_ln(h, bf(w_ffn1_in[l]), bf(w_ffn1_out[l]), ln1_g[l:l + 1], ln1_b[l:l + 1])
        h = _mixer_ln(h, batch, seq, bf(w_mix_in[l]), hgrn_lb, hgrn_gnorm[l:l + 1], bf(pool_w[l]),
                      pool_scale[l:l + 1], bf(w_mix_out[l]), ln2_g[l:l + 1], ln2_b[l:l + 1])
        k2d, v2d = _kv_proj(mem2d, n_mem, bf(xa_wk[l]), bf(xa_wv[l]))
        h = _xattn_ln(h, batch, seq, k2d, v2d, n_mem, bf(xa_wq[l]), bf(xa_wo[l]),
                      ln3_g[l:l + 1], ln3_b[l:l + 1])
        h = _ffn_ln(h, bf(w_ffn2_in[l]), bf(w_ffn2_out[l]), ln4_g[l:l + 1], ln4_b[l:l + 1])
    return h.reshape(batch, seq, D_MODEL)
```

```python
import functools

import numpy as np
import jax
import jax.numpy as jnp
from jax import lax
from jax.experimental import pallas as pl
from jax.experimental.pallas import tpu as pltpu

F32 = jnp.float32
BF16 = jnp.bfloat16

D_MODEL = 1024
DEPTH = 1
D_HGRN = 512
D_POOL = 512
HGRN_HEADS = 4
HEAD_DIM = 128
POOL_WINDOWS = (2, 4, 8, 16)
POOL_GROUP_DIM = 128
D_FF = 2816
D_IN_PROJ = 4 * D_HGRN + D_POOL
XA_HEADS = 4
XA_HEAD_DIM = 256
ALPHA = (2.0 * DEPTH) ** 0.25
LN_EPS = 1e-5
RMS_EPS = 1e-6

V7X_SUBLANES = 8
V7X_LANES = 128
V7X_MXU_DIM = 256
V7X_VMEM_LIMIT = 56 * 1024 * 1024

CHUNK = 64
SUB = V7X_SUBLANES
POOL_HALO = 16
FFN_TILE = 512
FFN_LN_PIECES = 4
MIX_TILE = 512
XA_TILE = 512
XA_LN_PIECES = 4
XA_LN_FIRST = 2
FF_CHUNKS = tuple((c0, min(512, D_FF - c0)) for c0 in range(0, D_FF, 512))


def _dot(a, b):
    return jnp.dot(a, b, preferred_element_type=F32)


def _dot_nt(a, b):
    return lax.dot_general(a, b, (((1,), (1,)), ((), ())), preferred_element_type=F32)


def _dot_tn(a, b):
    return lax.dot_general(a, b, (((0,), (0,)), ((), ())), preferred_element_type=F32)


def _silu(x):
    return x * jax.nn.sigmoid(x)


def _layernorm(y, g, b):
    mu = jnp.mean(y, axis=-1, keepdims=True)
    d = y - mu
    var = jnp.mean(d * d, axis=-1, keepdims=True)
    return d * lax.rsqrt(var + LN_EPS) * g + b


def _zero_after(v):
    sub = V7X_SUBLANES * (4 // v.dtype.itemsize)
    r, c = v.shape
    m = jnp.max(v.reshape(r // sub, sub, c), axis=0)
    m = functools.reduce(jnp.maximum, [m[:, j * V7X_LANES:(j + 1) * V7X_LANES] for j in range(c // V7X_LANES)])
    m = m.astype(F32)
    if sub > V7X_SUBLANES:
        m = jnp.maximum(m[:V7X_SUBLANES], m[V7X_SUBLANES:])
    bits = pltpu.bitcast(m, jnp.uint32)
    return pltpu.bitcast((bits >> 16) >> 16, F32)


def _resident(shape):
    zeros = (0,) * len(shape)
    return pl.BlockSpec(shape, lambda *_: zeros, pipeline_mode=pl.Buffered(1))


def _ffn_ln_kernel(*refs, n_tiles, n_side):
    x_ref, win_ref, wout_ref, g_ref, b_ref = refs[:5]
    side_in = refs[5:5 + n_side]
    o_ref = refs[5 + n_side]
    side_out = refs[6 + n_side:6 + 2 * n_side]
    y_s = refs[-1]
    step = pl.program_id(0)
    ln_rows = FFN_TILE // FFN_LN_PIECES
    gaps = len(FF_CHUNKS) - 1

    def ln_piece(p):
        rows = slice(p * ln_rows, (p + 1) * ln_rows)
        out = _layernorm(y_s[rows, :], g_ref[...], b_ref[...])
        o_ref[rows, :] = out
        return out

    def cast_side(n):
        v = side_in[n][...].astype(BF16)
        side_out[n][...] = v
        return v

    @pl.when(step == 0)
    def _():
        y_s[...] = jnp.zeros_like(y_s)

    @pl.when(step < n_tiles)
    def _():
        x = x_ref[...]
        xb = x.astype(BF16)
        acc = None
        jobs = [(FFN_TILE * D_MODEL // FFN_LN_PIECES * 3, functools.partial(ln_piece, p)) for p in range(FFN_LN_PIECES)]
        jobs += [(side_in[n].shape[0] * side_in[n].shape[1], functools.partial(cast_side, n)) for n in range(n_side)]
        bins = [[0, []] for _ in range(gaps)]
        for cost, job in sorted(jobs, key=lambda cj: -cj[0]):
            target = min(bins, key=lambda bn: bn[0])
            target[0] += cost
            target[1].append(job)
        lhs = xb
        for n, (c0, cw) in enumerate(FF_CHUNKS):
            gate = _dot(lhs, win_ref[:, c0:c0 + cw])
            up = _dot(lhs, win_ref[:, D_FF + c0:D_FF + c0 + cw])
            act = (_silu(gate) * up).astype(BF16)
            part = _dot(act, wout_ref[c0:c0 + cw, :])
            acc = part if acc is None else acc + part
            if n < gaps and bins[n][1]:
                zeros = [_zero_after(job()) for job in bins[n][1]]
                zero = sum(zeros[1:], zeros[0]).astype(BF16)
                lhs = xb + jnp.tile(zero, (FFN_TILE // V7X_SUBLANES, D_MODEL // V7X_LANES))
        y_s[...] = ALPHA * x + 0.5 * acc

    @pl.when(step == n_tiles)
    def _():
        for p in range(FFN_LN_PIECES):
            ln_piece(p)


def _ffn_ln(x2d, w_in, w_out, g, b, side=()):
    m = x2d.shape[0]
    assert m % FFN_TILE == 0
    n_tiles = m // FFN_TILE
    current = lambda i: (jnp.minimum(i, n_tiles - 1), 0)
    side_rows = 2 * V7X_SUBLANES
    flat = [w.reshape(n_tiles * side_rows, -1) for w in side]
    assert all(f.shape[1] % V7X_LANES == 0 for f in flat)
    side_specs = [pl.BlockSpec((side_rows, f.shape[1]), current) for f in flat]
    out = pl.pallas_call(
        functools.partial(_ffn_ln_kernel, n_tiles=n_tiles, n_side=len(side)),
        grid=(n_tiles + 1,),
        in_specs=[
            pl.BlockSpec((FFN_TILE, D_MODEL), current),
            _resident((D_MODEL, 2 * D_FF)),
            _resident((D_FF, D_MODEL)),
            _resident((1, D_MODEL)),
            _resident((1, D_MODEL)),
        ] + side_specs,
        out_specs=[pl.BlockSpec((FFN_TILE, D_MODEL), lambda i: (jnp.maximum(i - 1, 0), 0))] + side_specs,
        out_shape=[jax.ShapeDtypeStruct((m, D_MODEL), F32)] + [jax.ShapeDtypeStruct(f.shape, BF16) for f in flat],
        scratch_shapes=[pltpu.VMEM((FFN_TILE, D_MODEL), F32)],
        compiler_params=pltpu.CompilerParams(
            dimension_semantics=("arbitrary",), vmem_limit_bytes=V7X_VMEM_LIMIT),
        name="ffn_ln",
    )(x2d, w_in, w_out, g, b, *flat)
    return out[0], [o.reshape(w.shape) for o, w in zip(out[1:], side)]


N_BLK = CHUNK // SUB
FAR_ORDER = (7, 6, 3, 5, 4, 2, 1)
FAR_COLS = 256
NEAR_LEVELS = (0, 1, 2, 4)
PV_COLS = FAR_COLS + len(NEAR_LEVELS) * CHUNK
assert sorted(FAR_ORDER) == list(range(1, N_BLK)) and SUB * sum(FAR_ORDER) <= FAR_COLS


def _chunk_constants():
    t = np.arange(CHUNK)[:, None]
    s = np.arange(CHUNK)[None, :]
    tri = (s <= t).astype(np.float32)
    mask = np.zeros((CHUNK, PV_COLS), np.float32)
    off = 0
    for i in FAR_ORDER:
        mask[:, off:off + SUB * i] = (t // SUB == i)
        off += SUB * i
    for n, b in enumerate(NEAR_LEVELS):
        valid = (t == s) if b == 0 else (((t // b) % 2 == 1) & ((s // b) == (t // b) - 1))
        mask[:, FAR_COLS + n * CHUNK:FAR_COLS + (n + 1) * CHUNK] = valid
    return jnp.asarray(tri, BF16), jnp.asarray(mask, F32)


def _split3(x):
    hi = x.astype(BF16)
    r = x - hi.astype(F32)
    mid = r.astype(BF16)
    lo = (r - mid.astype(F32)).astype(BF16)
    return hi, mid, lo


def _mixer_ln_kernel(hn_ref, hc_ref, win_ref, lb_ref, gn_ref, pw_ref, ps_ref, wout_ref, g_ref, b_ref,
                     tri_ref, mask_ref, o_ref,
                     q_a, k_a, f_a, i_a, cum_a, gate_a, vb_a,
                     q_s, k_s, f_s, i_s, cum_s, gate_s, vb_s, merged_s, y_s, state_s, *, n_seq):
    ts = hn_ref.shape[0]
    step = pl.program_id(0)
    seq_tile_next = step % n_seq
    seq_tile = (step + n_seq - 1) % n_seq
    handoff = ((q_a, q_s), (k_a, k_s), (f_a, f_s), (i_a, i_s), (cum_a, cum_s), (gate_a, gate_s), (vb_a, vb_s))

    @pl.when(step == 0)
    def _():
        state_s[...] = jnp.zeros_like(state_s)
        for src, _ in handoff:
            src[...] = jnp.zeros_like(src)

    half = ts // 2
    col_blk = V7X_MXU_DIM

    def copy_rows(c):
        rows = slice(c * CHUNK, (c + 1) * CHUNK)
        for src, dst in handoff[:-1]:
            dst[rows, :] = src[rows, :]
        if c == 0:
            vb_s[0:POOL_HALO, :] = vb_a[0:POOL_HALO, :]
        prow = slice(POOL_HALO + c * CHUNK, POOL_HALO + (c + 1) * CHUNK)
        vb_s[prow, :] = vb_a[prow, :]

    def pool_halo():
        vb_a[0:POOL_HALO, :] = jnp.where(seq_tile_next == 0, 0.0, vb_s[ts:ts + POOL_HALO, :])

    a = lb_ref[...]
    e = jnp.exp(a - jnp.max(a, axis=0, keepdims=True))
    lower = e[0:1, :] / jnp.sum(e, axis=0, keepdims=True)

    def in_proj(r, j):
        rows = slice(r * half, (r + 1) * half)
        x = _dot(hn_ref[rows, :].astype(BF16), win_ref[:, j * col_blk:(j + 1) * col_blk])
        kind, cb = divmod(j * col_blk, D_HGRN)
        cols = slice(cb, cb + col_blk)
        if kind == 0:
            q_a[rows, cols] = _silu(x)
        elif kind == 1:
            forget = lower[:, cols] + (1.0 - lower[:, cols]) * jax.nn.sigmoid(x)
            f_a[rows, cols] = forget
            k_a[rows, cols] = 1.0 - forget
            cum_a[rows, cols] = jnp.log(forget)
        elif kind == 2:
            i_a[rows, cols] = x
        elif kind == 3:
            gate_a[rows, cols] = _silu(x)
        else:
            vb_a[POOL_HALO + r * half:POOL_HALO + (r + 1) * half, cols] = x

    def cum_chunks(c0, n):
        tri = tri_ref[...]
        for c in range(c0, c0 + n):
            rows = slice(c * CHUNK, (c + 1) * CHUNK)
            hi, mid, lo = _split3(cum_a[rows, :])
            cum_a[rows, :] = _dot(tri, hi) + _dot(tri, mid) + _dot(tri, lo)

    sub_row = lax.broadcasted_iota(jnp.int32, (SUB, HEAD_DIM), 0)
    zero_blk = jnp.zeros((SUB, HEAD_DIM), F32)
    far_pad = jnp.zeros((FAR_COLS - SUB * sum(FAR_ORDER), HEAD_DIM), F32)
    gn = gn_ref[...]
    zero_kb = jnp.zeros((CHUNK, HEAD_DIM), BF16)

    def block_diag(a, b):
        return jnp.concatenate([jnp.concatenate([a, zero_kb], axis=1),
                                jnp.concatenate([zero_kb, b], axis=1)], axis=0)

    def head_cols(hd):
        return slice(hd * HEAD_DIM, (hd + 1) * HEAD_DIM)

    def scores_stage(c, hd):
        r0 = c * CHUNK
        rows = pl.ds(r0, CHUNK)
        cols = head_cols(hd)
        q = q_s[rows, cols]
        k = k_s[rows, cols]
        f = f_s[rows, cols]
        v = i_s[rows, cols]
        cum = cum_s[rows, cols]

        def cum_row(row):
            return jnp.broadcast_to(cum_s[pl.ds(r0 + row, 1), cols], (SUB, HEAD_DIM))

        bound = [zero_blk] + [cum_row(SUB * j - 1) for j in range(1, N_BLK + 1)]

        q_far = (q * jnp.exp(cum - jnp.concatenate(bound[:N_BLK], axis=0))).astype(BF16)
        k_parts, v_parts = [], []
        for i in FAR_ORDER:
            n = SUB * i
            k_parts.append(k[:n] * jnp.exp(jnp.concatenate([bound[i]] * i, axis=0) - cum[:n]))
            v_parts.append(v[:n])
        k_far = jnp.concatenate(k_parts + [far_pad], axis=0).astype(BF16)
        s_far = _dot_nt(q_far, k_far)

        ref_q2, ref_k2, ref_q4, ref_k4 = [], [], [], []
        for j in range(N_BLK):
            lo_b, hi_b = bound[j], bound[j + 1]
            m1, m3, m5 = (cum_row(SUB * j + r) for r in (1, 3, 5))
            ref_q4.append(jnp.where(sub_row < 4, lo_b, m3))
            ref_k4.append(jnp.where(sub_row < 4, m3, hi_b))
            ref_q2.append(jnp.where(sub_row < 2, lo_b, jnp.where(sub_row < 4, m1, jnp.where(sub_row < 6, m3, m5))))
            ref_k2.append(jnp.where(sub_row < 2, m1, jnp.where(sub_row < 4, m3, jnp.where(sub_row < 6, m5, hi_b))))
        cat = lambda parts: jnp.concatenate(parts, axis=0)
        q2 = q * jnp.exp(cum - cat(ref_q2))
        k2 = k * jnp.exp(cat(ref_k2) - cum)
        q4 = q * jnp.exp(cum - cat(ref_q4))
        k4 = k * jnp.exp(cat(ref_k4) - cum)
        kb = k.astype(BF16)
        s_a = _dot_nt(jnp.concatenate([q, q * f], axis=1).astype(BF16), block_diag(kb, kb))
        s_b = _dot_nt(jnp.concatenate([q2, q4], axis=1).astype(BF16),
                      block_diag(k2.astype(BF16), k4.astype(BF16)))

        total = bound[N_BLK]
        return dict(
            p=(jnp.concatenate([s_far, s_a, s_b], axis=1) * mask_ref[...]).astype(BF16),
            v_all=jnp.concatenate(v_parts + [far_pad] + [v] * len(NEAR_LEVELS), axis=0).astype(BF16),
            q_dec=(q * jnp.exp(cum)).astype(BF16),
            k_dec=(k * jnp.exp(jnp.concatenate([total] * N_BLK, axis=0) - cum)).astype(BF16),
            vb=v.astype(BF16),
            decay=jnp.exp(total[0:1, :]))

    def state_stage(c, hd, sc):
        st = state_s[hd]
        if c == 0:
            st = jnp.where(seq_tile == 0, 0.0, st)
        o = _dot(sc["p"], sc["v_all"]) + _dot_nt(sc["q_dec"], st.astype(BF16))
        state_s[hd] = st * sc["decay"] + _dot_tn(sc["vb"], sc["k_dec"])
        return o

    def norm_stage(c, hd, o):
        rows = pl.ds(c * CHUNK, CHUNK)
        cols = head_cols(hd)
        o = o * lax.rsqrt(jnp.mean(o * o, axis=-1, keepdims=True) + RMS_EPS) * gn
        merged_s[rows, cols] = (o * gate_s[rows, cols]).astype(BF16)

    quarter = ts // 4

    def pool_rows(qd):
        r0, nr = qd * quarter, quarter
        pos = seq_tile * ts + r0 + lax.broadcasted_iota(jnp.int32, (nr, 1), 0) + 1
        for grp, win in enumerate(POOL_WINDOWS):
            cols = slice(grp * POOL_GROUP_DIM, (grp + 1) * POOL_GROUP_DIM)
            ext = vb_s[r0:r0 + POOL_HALO + nr, cols]
            wsum = ext
            span = 1
            while span < win:
                wsum = wsum + pltpu.roll(wsum, span, 0)
                span *= 2
            cur = ext[POOL_HALO:]
            count = jnp.minimum(pos, win).astype(F32)
            pooled = wsum[POOL_HALO:] / count - cur
            mixed = _dot(pooled.astype(BF16), pw_ref[grp]) * ps_ref[:, cols]
            merged_s[r0:r0 + nr, D_HGRN + grp * POOL_GROUP_DIM:D_HGRN + (grp + 1) * POOL_GROUP_DIM] = (
                mixed.astype(BF16))

    def out_proj(qd, n2):
        rows = slice(qd * quarter, (qd + 1) * quarter)
        cols = slice(n2 * (D_MODEL // 2), (n2 + 1) * (D_MODEL // 2))
        y_s[rows, cols] = _dot(merged_s[rows, :], wout_ref[:, cols])

    def out_ln(qd):
        rows = slice(qd * quarter, (qd + 1) * quarter)
        o_ref[rows, :] = _layernorm(ALPHA * hc_ref[rows, :] + y_s[rows, :], g_ref[...], b_ref[...])

    n_chunks = ts // CHUNK
    n_slots = n_chunks * HGRN_HEADS
    P = functools.partial
    COPY, PROJ, CUM, POOL, OUT, LN = 100, 256, 192, 200, 256, 150
    early = [(0, COPY, P(copy_rows, c)) for c in range(2, n_chunks)] + [(0, 10, pool_halo)]
    for r in range(2):
        blocks = list(range(D_IN_PROJ // col_blk))
        forget_blocks = [j for j in blocks if j * col_blk // D_HGRN == 1]
        first = 2 + 2 * r
        early += [(first, PROJ, P(in_proj, r, j)) for j in forget_blocks]
        early += [(first, CUM, P(cum_chunks, r * n_chunks // 2 + 2 * i, 2)) for i in range(n_chunks // 4)]
        early += [(first, PROJ, P(in_proj, r, j)) for j in blocks if j not in forget_blocks]
    late = []
    for qd in range(4):
        first = (qd + 1) * (n_chunks // 4) * HGRN_HEADS
        late += [(first, POOL, P(pool_rows, qd)), (first, OUT, P(out_proj, qd, 0)),
                 (first, OUT, P(out_proj, qd, 1)), (first, LN, P(out_ln, qd))]
    total = sum(cost for _, cost, _ in early + late)
    copy_rows(0)
    copy_rows(1)
    done = 0
    ch = lambda slot: (slot // HGRN_HEADS, slot % HGRN_HEADS)
    scores, outs = {}, {}
    for slot in range(n_slots + 2):
        if slot < n_slots:
            scores[slot] = scores_stage(*ch(slot))
        if 1 <= slot <= n_slots:
            outs[slot - 1] = state_stage(*ch(slot - 1), scores.pop(slot - 1))
        if slot >= 2:
            norm_stage(*ch(slot - 2), outs.pop(slot - 2))
        while done < total * (slot + 1) / n_slots:
            queue = late if late and late[0][0] <= slot - 1 else early
            if not queue or queue[0][0] > slot - 1:
                break
            _, cost, fn = queue.pop(0)
            fn()
            done += cost
    for _, _, fn in early + late:
        fn()


def _mixer_ln(h2d, batch, seq, w_in, lb, gnorm, pool_w, pool_scale, w_out, g, b):
    ts = MIX_TILE
    assert seq % ts == 0 and ts % CHUNK == 0
    n_seq = seq // ts
    n_tiles = batch * n_seq
    tri, mask = _chunk_constants()
    next_tile = pl.BlockSpec((ts, D_MODEL), lambda i: (jnp.minimum(i, n_tiles - 1), 0))
    done_tile = pl.BlockSpec((ts, D_MODEL), lambda i: (jnp.maximum(i - 1, 0), 0))
    proj_set = [pltpu.VMEM((ts, D_HGRN), F32)] * 6 + [pltpu.VMEM((ts + POOL_HALO, D_POOL), F32)]
    return pl.pallas_call(
        functools.partial(_mixer_ln_kernel, n_seq=n_seq),
        grid=(n_tiles + 1,),
        in_specs=[
            next_tile,
            done_tile,
            _resident(w_in.shape),
            _resident(lb.shape),
            _resident(gnorm.shape),
            _resident(pool_w.shape),
            _resident(pool_scale.shape),
            _resident(w_out.shape),
            _resident(g.shape),
            _resident(b.shape),
            _resident(tri.shape),
            _resident(mask.shape),
        ],
        out_specs=done_tile,
        out_shape=jax.ShapeDtypeStruct(h2d.shape, F32),
        scratch_shapes=proj_set + proj_set + [
            pltpu.VMEM((ts, D_HGRN + D_POOL), BF16),
            pltpu.VMEM((ts, D_MODEL), F32),
            pltpu.VMEM((HGRN_HEADS, HEAD_DIM, HEAD_DIM), F32),
        ],
        compiler_params=pltpu.CompilerParams(
            dimension_semantics=("arbitrary",), vmem_limit_bytes=V7X_VMEM_LIMIT),
        name="mixer_ln",
    )(h2d, h2d, w_in, lb, gnorm, pool_w, pool_scale, w_out, g, b, tri, mask)


def _kv_proj_kernel(mem_ref, wk_ref, wv_ref, k_ref, v_ref):
    mb = mem_ref[...].astype(BF16)
    k_ref[...] = _dot(mb, wk_ref[...]).astype(BF16)
    v_ref[...] = _dot(mb, wv_ref[...]).astype(BF16)


def _kv_proj(mem2d, n_mem, wk, wv):
    rows = mem2d.shape[0]
    tile = pl.BlockSpec((n_mem, D_MODEL), lambda i: (i, 0))
    return pl.pallas_call(
        _kv_proj_kernel,
        grid=(rows // n_mem,),
        in_specs=[tile, _resident(wk.shape), _resident(wv.shape)],
        out_specs=[tile, tile],
        out_shape=[jax.ShapeDtypeStruct((rows, D_MODEL), BF16)] * 2,
        compiler_params=pltpu.CompilerParams(
            dimension_semantics=("arbitrary",), vmem_limit_bytes=V7X_VMEM_LIMIT),
        name="kv_proj",
    )(mem2d, wk, wv)


def _xattn_ln_kernel(h_ref, k_ref, v_ref, wq_ref, wo_ref, g_ref, b_ref, o_ref, y_s, *, n_tiles):
    step = pl.program_id(0)
    ts = h_ref.shape[0]
    ln_rows = ts // XA_LN_PIECES

    def ln_piece(p):
        rows = slice(p * ln_rows, (p + 1) * ln_rows)
        out = _layernorm(y_s[rows, :], g_ref[...], b_ref[...])
        o_ref[rows, :] = out
        return out

    @pl.when(step == 0)
    def _():
        y_s[...] = jnp.zeros_like(y_s)

    @pl.when(step < n_tiles)
    def _():
        h = h_ref[...]
        hb = h.astype(BF16)
        q = _dot(hb, wq_ref[...])
        pieces = list(range(XA_LN_PIECES))
        heads = []
        for hd in range(XA_HEADS):
            cols = slice(hd * XA_HEAD_DIM, (hd + 1) * XA_HEAD_DIM)
            qh = q[:, cols].astype(BF16)
            if pieces:
                take = min(len(pieces), XA_LN_FIRST if hd == 0 else 1)
                zeros = [_zero_after(ln_piece(pieces.pop(0))) for _ in range(take)]
                zero = sum(zeros[1:], zeros[0]).astype(BF16)
                qh = qh + jnp.tile(zero, (ts // V7X_SUBLANES, XA_HEAD_DIM // V7X_LANES))
            s = _dot_nt(qh, k_ref[:, cols]) * (XA_HEAD_DIM ** -0.5)
            s = s - jnp.max(s, axis=-1, keepdims=True)
            p = jnp.exp(s)
            p = p / jnp.sum(p, axis=-1, keepdims=True)
            heads.append(_dot(p.astype(BF16), v_ref[:, cols]).astype(BF16))
        assert not pieces
        y_s[...] = ALPHA * h + _dot(jnp.concatenate(heads, axis=-1), wo_ref[...])

    @pl.when(step == n_tiles)
    def _():
        for p in range(XA_LN_PIECES):
            ln_piece(p)


def _xattn_ln(h2d, batch, seq, k2d, v2d, n_mem, wq, wo, g, b):
    ts = XA_TILE
    assert seq % ts == 0
    n_seq = seq // ts
    n_tiles = batch * n_seq
    attended = lambda i: jnp.minimum(i, n_tiles - 1)
    tile = pl.BlockSpec((ts, D_MODEL), lambda i: (attended(i), 0))
    mem_tile = pl.BlockSpec((n_mem, D_MODEL), lambda i: (attended(i) // n_seq, 0))
    return pl.pallas_call(
        functools.partial(_xattn_ln_kernel, n_tiles=n_tiles),
        grid=(n_tiles + 1,),
        in_specs=[tile, mem_tile, mem_tile, _resident(wq.shape), _resident(wo.shape),
                  _resident(g.shape), _resident(b.shape)],
        out_specs=pl.BlockSpec((ts, D_MODEL), lambda i: (jnp.maximum(i - 1, 0), 0)),
        out_shape=jax.ShapeDtypeStruct(h2d.shape, F32),
        scratch_shapes=[pltpu.VMEM((ts, D_MODEL), F32)],
        compiler_params=pltpu.CompilerParams(
            dimension_semantics=("arbitrary",), vmem_limit_bytes=V7X_VMEM_LIMIT),
        name="xattn_ln",
    )(h2d, k2d, v2d, wq, wo, g, b)


def kernel(x, mem, w_ffn1_in, w_ffn1_out, ln1_g, ln1_b, w_mix_in, hgrn_lb, hgrn_gnorm, pool_w, pool_scale, w_mix_out, ln2_g, ln2_b, xa_wq, xa_wk, xa_wv, xa_wo, ln3_g, ln3_b, w_ffn2_in, w_ffn2_out, ln4_g, ln4_b):
    batch, seq, _ = x.shape
    n_mem = mem.shape[1]
    assert w_ffn1_in.shape[0] == DEPTH == 1
    bf = lambda w: w.astype(BF16)
    h = x.reshape(batch * seq, D_MODEL)
    mem2d = mem.reshape(batch * n_mem, D_MODEL)
    for l in range(DEPTH):
        later = (w_mix_in[l], pool_w[l], w_mix_out[l], xa_wq[l], xa_wk[l], xa_wv[l], xa_wo[l],
                 w_ffn2_in[l], w_ffn2_out[l])
        h, later = _ffn_ln(h, bf(w_ffn1_in[l]), bf(w_ffn1_out[l]), ln1_g[l:l + 1], ln1_b[l:l + 1], side=later)
        mix_in, pool, mix_out, wq, wk, wv, wo, ffn2_in, ffn2_out = later
        h = _mixer_ln(h, batch, seq, mix_in, hgrn_lb, hgrn_gnorm[l:l + 1], pool,
                      pool_scale[l:l + 1], mix_out, ln2_g[l:l + 1], ln2_b[l:l + 1])
        k2d, v2d = _kv_proj(mem2d, n_mem, wk, wv)
        h = _xattn_ln(h, batch, seq, k2d, v2d, n_mem, wq, wo, ln3_g[l:l + 1], ln3_b[l:l + 1])
        h, _ = _ffn_ln(h, ffn2_in, ffn2_out, ln4_g[l:l + 1], ln4_b[l:l + 1])
    return h.reshape(batch, seq, D_MODEL)
```

```python
import functools

import numpy as np
import jax
import jax.numpy as jnp
from jax import lax
from jax.experimental import pallas as pl
from jax.experimental.pallas import tpu as pltpu

F32 = jnp.float32
BF16 = jnp.bfloat16

D_MODEL = 1024
DEPTH = 1
D_HGRN = 512
D_POOL = 512
HGRN_HEADS = 4
HEAD_DIM = 128
POOL_WINDOWS = (2, 4, 8, 16)
POOL_GROUP_DIM = 128
D_FF = 2816
D_IN_PROJ = 4 * D_HGRN + D_POOL
XA_HEADS = 4
XA_HEAD_DIM = 256
ALPHA = (2.0 * DEPTH) ** 0.25
LN_EPS = 1e-5
RMS_EPS = 1e-6

V7X_SUBLANES = 8
V7X_LANES = 128
V7X_MXU_DIM = 256
V7X_VMEM_LIMIT = 56 * 1024 * 1024

CHUNK = 64
SUB = V7X_SUBLANES
POOL_HALO = 16
FFN_TILE = 512
FFN_LN_PIECES = 4
MIX_TILE = 512
XA_TILE = 512
XA_LN_PIECES = 4
XA_LN_FIRST = 2
FF_CHUNKS = tuple((c0, min(512, D_FF - c0)) for c0 in range(0, D_FF, 512))


def _dot(a, b):
    return jnp.dot(a, b, preferred_element_type=F32)


def _dot_nt(a, b):
    return lax.dot_general(a, b, (((1,), (1,)), ((), ())), preferred_element_type=F32)


def _dot_tn(a, b):
    return lax.dot_general(a, b, (((0,), (0,)), ((), ())), preferred_element_type=F32)


def _silu(x):
    return x * jax.nn.sigmoid(x)


def _layernorm(y, g, b):
    mu = jnp.mean(y, axis=-1, keepdims=True)
    d = y - mu
    var = jnp.mean(d * d, axis=-1, keepdims=True)
    return d * lax.rsqrt(var + LN_EPS) * g + b


def _zero_after(v):
    sub = V7X_SUBLANES * (4 // v.dtype.itemsize)
    r, c = v.shape
    m = jnp.max(v.reshape(r // sub, sub, c), axis=0)
    m = functools.reduce(jnp.maximum, [m[:, j * V7X_LANES:(j + 1) * V7X_LANES] for j in range(c // V7X_LANES)])
    m = m.astype(F32)
    if sub > V7X_SUBLANES:
        m = jnp.maximum(m[:V7X_SUBLANES], m[V7X_SUBLANES:])
    bits = pltpu.bitcast(m, jnp.uint32)
    return pltpu.bitcast((bits >> 16) >> 16, F32)


def _resident(shape):
    zeros = (0,) * len(shape)
    return pl.BlockSpec(shape, lambda *_: zeros, pipeline_mode=pl.Buffered(1))


def _ffn_ln_kernel(*refs, n_tiles, n_side):
    x_ref, win_ref, wout_ref, g_ref, b_ref = refs[:5]
    side_in = refs[5:5 + n_side]
    o_ref = refs[5 + n_side]
    side_out = refs[6 + n_side:6 + 2 * n_side]
    y_s = refs[-1]
    step = pl.program_id(0)
    ln_rows = FFN_TILE // FFN_LN_PIECES
    gaps = len(FF_CHUNKS) - 1

    def ln_piece(p):
        rows = slice(p * ln_rows, (p + 1) * ln_rows)
        out = _layernorm(y_s[rows, :], g_ref[...], b_ref[...])
        o_ref[rows, :] = out
        return out

    def cast_side(n):
        v = side_in[n][...].astype(BF16)
        side_out[n][...] = v
        return v

    @pl.when(step == 0)
    def _():
        y_s[...] = jnp.zeros_like(y_s)

    @pl.when(step < n_tiles)
    def _():
        x = x_ref[...]
        xb = x.astype(BF16)
        acc = None
        jobs = [(FFN_TILE * D_MODEL // FFN_LN_PIECES * 3, functools.partial(ln_piece, p)) for p in range(FFN_LN_PIECES)]
        jobs += [(side_in[n].shape[0] * side_in[n].shape[1], functools.partial(cast_side, n)) for n in range(n_side)]
        bins = [[0, []] for _ in range(gaps)]
        for cost, job in sorted(jobs, key=lambda cj: -cj[0]):
            target = min(bins, key=lambda bn: bn[0])
            target[0] += cost
            target[1].append(job)
        lhs = xb
        for n, (c0, cw) in enumerate(FF_CHUNKS):
            gate = _dot(lhs, win_ref[:, c0:c0 + cw])
            up = _dot(lhs, win_ref[:, D_FF + c0:D_FF + c0 + cw])
            act = (_silu(gate) * up).astype(BF16)
            part = _dot(act, wout_ref[c0:c0 + cw, :])
            acc = part if acc is None else acc + part
            if n < gaps and bins[n][1]:
                zeros = [_zero_after(job()) for job in bins[n][1]]
                zero = sum(zeros[1:], zeros[0]).astype(BF16)
                lhs = xb + jnp.tile(zero, (FFN_TILE // V7X_SUBLANES, D_MODEL // V7X_LANES))
        y_s[...] = ALPHA * x + 0.5 * acc

    @pl.when(step == n_tiles)
    def _():
        for p in range(FFN_LN_PIECES):
            ln_piece(p)


def _side_blocks(w, layer, n_steps):
    _, r, c = w.shape
    packed_rows = 2 * V7X_SUBLANES
    rows = next(n for n in range(packed_rows, r + 1, packed_rows) if r % n == 0 and r // n <= n_steps)
    index = lambda i: (jnp.minimum(i, r // rows - 1), 0)
    return (pl.BlockSpec((None, rows, c), lambda i: (layer,) + index(i)), pl.BlockSpec((rows, c), index),
            jax.ShapeDtypeStruct((r, c), BF16))


def _ffn_ln(x2d, w_in, w_out, g, b, side=(), layer=0):
    m = x2d.shape[0]
    assert m % FFN_TILE == 0
    n_tiles = m // FFN_TILE
    current = lambda i: (jnp.minimum(i, n_tiles - 1), 0)
    blocks = [_side_blocks(w, layer, n_tiles) for w in side]
    out = pl.pallas_call(
        functools.partial(_ffn_ln_kernel, n_tiles=n_tiles, n_side=len(side)),
        grid=(n_tiles + 1,),
        in_specs=[
            pl.BlockSpec((FFN_TILE, D_MODEL), current),
            _resident((D_MODEL, 2 * D_FF)),
            _resident((D_FF, D_MODEL)),
            _resident((1, D_MODEL)),
            _resident((1, D_MODEL)),
        ] + [blk[0] for blk in blocks],
        out_specs=[pl.BlockSpec((FFN_TILE, D_MODEL), lambda i: (jnp.maximum(i - 1, 0), 0))]
        + [blk[1] for blk in blocks],
        out_shape=[jax.ShapeDtypeStruct((m, D_MODEL), F32)] + [blk[2] for blk in blocks],
        scratch_shapes=[pltpu.VMEM((FFN_TILE, D_MODEL), F32)],
        compiler_params=pltpu.CompilerParams(
            dimension_semantics=("arbitrary",), vmem_limit_bytes=V7X_VMEM_LIMIT),
        name="ffn_ln",
    )(x2d, w_in, w_out, g, b, *side)
    return out[0], out[1:]


N_BLK = CHUNK // SUB
FAR_ORDER = (7, 6, 3, 5, 4, 2, 1)
FAR_COLS = 256
NEAR_LEVELS = (0, 1, 2, 4)
PV_COLS = FAR_COLS + len(NEAR_LEVELS) * CHUNK
assert sorted(FAR_ORDER) == list(range(1, N_BLK)) and SUB * sum(FAR_ORDER) <= FAR_COLS


def _chunk_constants():
    t = np.arange(CHUNK)[:, None]
    s = np.arange(CHUNK)[None, :]
    tri = (s <= t).astype(np.float32)
    mask = np.zeros((CHUNK, PV_COLS), np.float32)
    off = 0
    for i in FAR_ORDER:
        mask[:, off:off + SUB * i] = (t // SUB == i)
        off += SUB * i
    for n, b in enumerate(NEAR_LEVELS):
        valid = (t == s) if b == 0 else (((t // b) % 2 == 1) & ((s // b) == (t // b) - 1))
        mask[:, FAR_COLS + n * CHUNK:FAR_COLS + (n + 1) * CHUNK] = valid
    return jnp.asarray(tri, BF16), jnp.asarray(mask, F32)


def _split3(x):
    hi = x.astype(BF16)
    r = x - hi.astype(F32)
    mid = r.astype(BF16)
    lo = (r - mid.astype(F32)).astype(BF16)
    return hi, mid, lo


def _mixer_ln_kernel(hn_ref, hc_ref, win_ref, lb_ref, gn_ref, pw_ref, ps_ref, wout_ref, g_ref, b_ref,
                     tri_ref, mask_ref, o_ref,
                     q_a, k_a, f_a, i_a, cum_a, gate_a, vb_a,
                     q_s, k_s, f_s, i_s, cum_s, gate_s, vb_s, merged_s, y_s, state_s, *, n_seq):
    ts = hn_ref.shape[0]
    step = pl.program_id(0)
    seq_tile_next = step % n_seq
    seq_tile = (step + n_seq - 1) % n_seq
    handoff = ((q_a, q_s), (k_a, k_s), (f_a, f_s), (i_a, i_s), (cum_a, cum_s), (gate_a, gate_s), (vb_a, vb_s))

    @pl.when(step == 0)
    def _():
        state_s[...] = jnp.zeros_like(state_s)
        for src, _ in handoff:
            src[...] = jnp.zeros_like(src)

    half = ts // 2
    col_blk = V7X_MXU_DIM

    def copy_rows(c):
        rows = slice(c * CHUNK, (c + 1) * CHUNK)
        for src, dst in handoff[:-1]:
            dst[rows, :] = src[rows, :]
        if c == 0:
            vb_s[0:POOL_HALO, :] = vb_a[0:POOL_HALO, :]
        prow = slice(POOL_HALO + c * CHUNK, POOL_HALO + (c + 1) * CHUNK)
        vb_s[prow, :] = vb_a[prow, :]

    def pool_halo():
        vb_a[0:POOL_HALO, :] = jnp.where(seq_tile_next == 0, 0.0, vb_s[ts:ts + POOL_HALO, :])

    a = lb_ref[...]
    e = jnp.exp(a - jnp.max(a, axis=0, keepdims=True))
    lower = e[0:1, :] / jnp.sum(e, axis=0, keepdims=True)

    def in_proj(r, j):
        rows = slice(r * half, (r + 1) * half)
        x = _dot(hn_ref[rows, :].astype(BF16), win_ref[:, j * col_blk:(j + 1) * col_blk])
        kind, cb = divmod(j * col_blk, D_HGRN)
        cols = slice(cb, cb + col_blk)
        if kind == 0:
            q_a[rows, cols] = _silu(x)
        elif kind == 1:
            forget = lower[:, cols] + (1.0 - lower[:, cols]) * jax.nn.sigmoid(x)
            f_a[rows, cols] = forget
            k_a[rows, cols] = 1.0 - forget
            cum_a[rows, cols] = jnp.log(forget)
        elif kind == 2:
            i_a[rows, cols] = x
        elif kind == 3:
            gate_a[rows, cols] = _silu(x)
        else:
            vb_a[POOL_HALO + r * half:POOL_HALO + (r + 1) * half, cols] = x

    def cum_chunks(c0, n):
        tri = tri_ref[...]
        for c in range(c0, c0 + n):
            rows = slice(c * CHUNK, (c + 1) * CHUNK)
            hi, mid, lo = _split3(cum_a[rows, :])
            cum_a[rows, :] = _dot(tri, hi) + _dot(tri, mid) + _dot(tri, lo)

    sub_row = lax.broadcasted_iota(jnp.int32, (SUB, HEAD_DIM), 0)
    zero_blk = jnp.zeros((SUB, HEAD_DIM), F32)
    far_pad = jnp.zeros((FAR_COLS - SUB * sum(FAR_ORDER), HEAD_DIM), F32)
    gn = gn_ref[...]
    zero_kb = jnp.zeros((CHUNK, HEAD_DIM), BF16)

    def block_diag(a, b):
        return jnp.concatenate([jnp.concatenate([a, zero_kb], axis=1),
                                jnp.concatenate([zero_kb, b], axis=1)], axis=0)

    def head_cols(hd):
        return slice(hd * HEAD_DIM, (hd + 1) * HEAD_DIM)

    def scores_stage(c, hd):
        r0 = c * CHUNK
        rows = pl.ds(r0, CHUNK)
        cols = head_cols(hd)
        q = q_s[rows, cols]
        k = k_s[rows, cols]
        f = f_s[rows, cols]
        v = i_s[rows, cols]
        cum = cum_s[rows, cols]

        def cum_row(row):
            return jnp.broadcast_to(cum_s[pl.ds(r0 + row, 1), cols], (SUB, HEAD_DIM))

        bound = [zero_blk] + [cum_row(SUB * j - 1) for j in range(1, N_BLK + 1)]

        q_far = (q * jnp.exp(cum - jnp.concatenate(bound[:N_BLK], axis=0))).astype(BF16)
        k_parts, v_parts = [], []
        for i in FAR_ORDER:
            n = SUB * i
            k_parts.append(k[:n] * jnp.exp(jnp.concatenate([bound[i]] * i, axis=0) - cum[:n]))
            v_parts.append(v[:n])
        k_far = jnp.concatenate(k_parts + [far_pad], axis=0).astype(BF16)
        s_far = _dot_nt(q_far, k_far)

        ref_q2, ref_k2, ref_q4, ref_k4 = [], [], [], []
        for j in range(N_BLK):
            lo_b, hi_b = bound[j], bound[j + 1]
            m1, m3, m5 = (cum_row(SUB * j + r) for r in (1, 3, 5))
            ref_q4.append(jnp.where(sub_row < 4, lo_b, m3))
            ref_k4.append(jnp.where(sub_row < 4, m3, hi_b))
            ref_q2.append(jnp.where(sub_row < 2, lo_b, jnp.where(sub_row < 4, m1, jnp.where(sub_row < 6, m3, m5))))
            ref_k2.append(jnp.where(sub_row < 2, m1, jnp.where(sub_row < 4, m3, jnp.where(sub_row < 6, m5, hi_b))))
        cat = lambda parts: jnp.concatenate(parts, axis=0)
        q2 = q * jnp.exp(cum - cat(ref_q2))
        k2 = k * jnp.exp(cat(ref_k2) - cum)
        q4 = q * jnp.exp(cum - cat(ref_q4))
        k4 = k * jnp.exp(cat(ref_k4) - cum)
        kb = k.astype(BF16)
        s_a = _dot_nt(jnp.concatenate([q, q * f], axis=1).astype(BF16), block_diag(kb, kb))
        s_b = _dot_nt(jnp.concatenate([q2, q4], axis=1).astype(BF16),
                      block_diag(k2.astype(BF16), k4.astype(BF16)))

        total = bound[N_BLK]
        return dict(
            p=(jnp.concatenate([s_far, s_a, s_b], axis=1) * mask_ref[...]).astype(BF16),
            v_all=jnp.concatenate(v_parts + [far_pad] + [v] * len(NEAR_LEVELS), axis=0).astype(BF16),
            q_dec=(q * jnp.exp(cum)).astype(BF16),
            k_dec=(k * jnp.exp(jnp.concatenate([total] * N_BLK, axis=0) - cum)).astype(BF16),
            vb=v.astype(BF16),
            decay=jnp.exp(total[0:1, :]))

    def state_stage(c, hd, sc):
        st = state_s[hd]
        if c == 0:
            st = jnp.where(seq_tile == 0, 0.0, st)
        o = _dot(sc["p"], sc["v_all"]) + _dot_nt(sc["q_dec"], st.astype(BF16))
        state_s[hd] = st * sc["decay"] + _dot_tn(sc["vb"], sc["k_dec"])
        return o

    def norm_stage(c, hd, o):
        rows = pl.ds(c * CHUNK, CHUNK)
        cols = head_cols(hd)
        o = o * lax.rsqrt(jnp.mean(o * o, axis=-1, keepdims=True) + RMS_EPS) * gn
        merged_s[rows, cols] = (o * gate_s[rows, cols]).astype(BF16)

    quarter = ts // 4

    def pool_rows(qd):
        r0, nr = qd * quarter, quarter
        pos = seq_tile * ts + r0 + lax.broadcasted_iota(jnp.int32, (nr, 1), 0) + 1
        for grp, win in enumerate(POOL_WINDOWS):
            cols = slice(grp * POOL_GROUP_DIM, (grp + 1) * POOL_GROUP_DIM)
            ext = vb_s[r0:r0 + POOL_HALO + nr, cols]
            wsum = ext
            span = 1
            while span < win:
                wsum = wsum + pltpu.roll(wsum, span, 0)
                span *= 2
            cur = ext[POOL_HALO:]
            count = jnp.minimum(pos, win).astype(F32)
            pooled = wsum[POOL_HALO:] / count - cur
            mixed = _dot(pooled.astype(BF16), pw_ref[grp]) * ps_ref[:, cols]
            merged_s[r0:r0 + nr, D_HGRN + grp * POOL_GROUP_DIM:D_HGRN + (grp + 1) * POOL_GROUP_DIM] = (
                mixed.astype(BF16))

    def out_proj(qd, n2):
        rows = slice(qd * quarter, (qd + 1) * quarter)
        cols = slice(n2 * (D_MODEL // 2), (n2 + 1) * (D_MODEL // 2))
        y_s[rows, cols] = _dot(merged_s[rows, :], wout_ref[:, cols])

    def out_ln(qd):
        rows = slice(qd * quarter, (qd + 1) * quarter)
        o_ref[rows, :] = _layernorm(ALPHA * hc_ref[rows, :] + y_s[rows, :], g_ref[...], b_ref[...])

    n_chunks = ts // CHUNK
    n_slots = n_chunks * HGRN_HEADS
    P = functools.partial
    COPY, PROJ, CUM, POOL, OUT, LN = 100, 256, 192, 200, 256, 150
    early = [(0, COPY, P(copy_rows, c)) for c in range(2, n_chunks)] + [(0, 10, pool_halo)]
    for r in range(2):
        blocks = list(range(D_IN_PROJ // col_blk))
        forget_blocks = [j for j in blocks if j * col_blk // D_HGRN == 1]
        first = 2 + 2 * r
        early += [(first, PROJ, P(in_proj, r, j)) for j in forget_blocks]
        early += [(first, CUM, P(cum_chunks, r * n_chunks // 2 + 2 * i, 2)) for i in range(n_chunks // 4)]
        early += [(first, PROJ, P(in_proj, r, j)) for j in blocks if j not in forget_blocks]
    late = []
    for qd in range(4):
        first = (qd + 1) * (n_chunks // 4) * HGRN_HEADS
        late += [(first, POOL, P(pool_rows, qd)), (first, OUT, P(out_proj, qd, 0)),
                 (first, OUT, P(out_proj, qd, 1)), (first, LN, P(out_ln, qd))]
    total = sum(cost for _, cost, _ in early + late)
    copy_rows(0)
    copy_rows(1)
    done = 0
    ch = lambda slot: (slot // HGRN_HEADS, slot % HGRN_HEADS)
    scores, outs = {}, {}
    for slot in range(n_slots + 2):
        if slot < n_slots:
            scores[slot] = scores_stage(*ch(slot))
        if 1 <= slot <= n_slots:
            outs[slot - 1] = state_stage(*ch(slot - 1), scores.pop(slot - 1))
        if slot >= 2:
            norm_stage(*ch(slot - 2), outs.pop(slot - 2))
        while done < total * (slot + 1) / n_slots:
            queue = late if late and late[0][0] <= slot - 1 else early
            if not queue or queue[0][0] > slot - 1:
                break
            _, cost, fn = queue.pop(0)
            fn()
            done += cost
    for _, _, fn in early + late:
        fn()


def _mixer_ln(h2d, batch, seq, w_in, lb, gnorm, pool_w, pool_scale, w_out, g, b):
    ts = MIX_TILE
    assert seq % ts == 0 and ts % CHUNK == 0
    n_seq = seq // ts
    n_tiles = batch * n_seq
    tri, mask = _chunk_constants()
    next_tile = pl.BlockSpec((ts, D_MODEL), lambda i: (jnp.minimum(i, n_tiles - 1), 0))
    done_tile = pl.BlockSpec((ts, D_MODEL), lambda i: (jnp.maximum(i - 1, 0), 0))
    proj_set = [pltpu.VMEM((ts, D_HGRN), F32)] * 6 + [pltpu.VMEM((ts + POOL_HALO, D_POOL), F32)]
    return pl.pallas_call(
        functools.partial(_mixer_ln_kernel, n_seq=n_seq),
        grid=(n_tiles + 1,),
        in_specs=[
            next_tile,
            done_tile,
            _resident(w_in.shape),
            _resident(lb.shape),
            _resident(gnorm.shape),
            _resident(pool_w.shape),
            _resident(pool_scale.shape),
            _resident(w_out.shape),
            _resident(g.shape),
            _resident(b.shape),
            _resident(tri.shape),
            _resident(mask.shape),
        ],
        out_specs=done_tile,
        out_shape=jax.ShapeDtypeStruct(h2d.shape, F32),
        scratch_shapes=proj_set + proj_set + [
            pltpu.VMEM((ts, D_HGRN + D_POOL), BF16),
            pltpu.VMEM((ts, D_MODEL), F32),
            pltpu.VMEM((HGRN_HEADS, HEAD_DIM, HEAD_DIM), F32),
        ],
        compiler_params=pltpu.CompilerParams(
            dimension_semantics=("arbitrary",), vmem_limit_bytes=V7X_VMEM_LIMIT),
        name="mixer_ln",
    )(h2d, h2d, w_in, lb, gnorm, pool_w, pool_scale, w_out, g, b, tri, mask)


def _kv_proj_kernel(mem_ref, wk_ref, wv_ref, k_ref, v_ref):
    mb = mem_ref[...].astype(BF16)
    k_ref[...] = _dot(mb, wk_ref[...]).astype(BF16)
    v_ref[...] = _dot(mb, wv_ref[...]).astype(BF16)


def _kv_proj(mem2d, n_mem, wk, wv):
    rows = mem2d.shape[0]
    tile = pl.BlockSpec((n_mem, D_MODEL), lambda i: (i, 0))
    return pl.pallas_call(
        _kv_proj_kernel,
        grid=(rows // n_mem,),
        in_specs=[tile, _resident(wk.shape), _resident(wv.shape)],
        out_specs=[tile, tile],
        out_shape=[jax.ShapeDtypeStruct((rows, D_MODEL), BF16)] * 2,
        compiler_params=pltpu.CompilerParams(
            dimension_semantics=("arbitrary",), vmem_limit_bytes=V7X_VMEM_LIMIT),
        name="kv_proj",
    )(mem2d, wk, wv)


def _xattn_ln_kernel(h_ref, k_ref, v_ref, wq_ref, wo_ref, g_ref, b_ref, o_ref, y_s, *, n_tiles):
    step = pl.program_id(0)
    ts = h_ref.shape[0]
    ln_rows = ts // XA_LN_PIECES

    def ln_piece(p):
        rows = slice(p * ln_rows, (p + 1) * ln_rows)
        out = _layernorm(y_s[rows, :], g_ref[...], b_ref[...])
        o_ref[rows, :] = out
        return out

    @pl.when(step == 0)
    def _():
        y_s[...] = jnp.zeros_like(y_s)

    @pl.when(step < n_tiles)
    def _():
        h = h_ref[...]
        hb = h.astype(BF16)
        q = _dot(hb, wq_ref[...])
        pieces = list(range(XA_LN_PIECES))
        heads = []
        for hd in range(XA_HEADS):
            cols = slice(hd * XA_HEAD_DIM, (hd + 1) * XA_HEAD_DIM)
            qh = q[:, cols].astype(BF16)
            if pieces:
                take = min(len(pieces), XA_LN_FIRST if hd == 0 else 1)
                zeros = [_zero_after(ln_piece(pieces.pop(0))) for _ in range(take)]
                zero = sum(zeros[1:], zeros[0]).astype(BF16)
                qh = qh + jnp.tile(zero, (ts // V7X_SUBLANES, XA_HEAD_DIM // V7X_LANES))
            s = _dot_nt(qh, k_ref[:, cols]) * (XA_HEAD_DIM ** -0.5)
            s = s - jnp.max(s, axis=-1, keepdims=True)
            p = jnp.exp(s)
            p = p / jnp.sum(p, axis=-1, keepdims=True)
            heads.append(_dot(p.astype(BF16), v_ref[:, cols]).astype(BF16))
        assert not pieces
        y_s[...] = ALPHA * h + _dot(jnp.concatenate(heads, axis=-1), wo_ref[...])

    @pl.when(step == n_tiles)
    def _():
        for p in range(XA_LN_PIECES):
            ln_piece(p)


def _xattn_ln(h2d, batch, seq, k2d, v2d, n_mem, wq, wo, g, b):
    ts = XA_TILE
    assert seq % ts == 0
    n_seq = seq // ts
    n_tiles = batch * n_seq
    attended = lambda i: jnp.minimum(i, n_tiles - 1)
    tile = pl.BlockSpec((ts, D_MODEL), lambda i: (attended(i), 0))
    mem_tile = pl.BlockSpec((n_mem, D_MODEL), lambda i: (attended(i) // n_seq, 0))
    return pl.pallas_call(
        functools.partial(_xattn_ln_kernel, n_tiles=n_tiles),
        grid=(n_tiles + 1,),
        in_specs=[tile, mem_tile, mem_tile, _resident(wq.shape), _resident(wo.shape),
                  _resident(g.shape), _resident(b.shape)],
        out_specs=pl.BlockSpec((ts, D_MODEL), lambda i: (jnp.maximum(i - 1, 0), 0)),
        out_shape=jax.ShapeDtypeStruct(h2d.shape, F32),
        scratch_shapes=[pltpu.VMEM((ts, D_MODEL), F32)],
        compiler_params=pltpu.CompilerParams(
            dimension_semantics=("arbitrary",), vmem_limit_bytes=V7X_VMEM_LIMIT),
        name="xattn_ln",
    )(h2d, k2d, v2d, wq, wo, g, b)


def kernel(x, mem, w_ffn1_in, w_ffn1_out, ln1_g, ln1_b, w_mix_in, hgrn_lb, hgrn_gnorm, pool_w, pool_scale, w_mix_out, ln2_g, ln2_b, xa_wq, xa_wk, xa_wv, xa_wo, ln3_g, ln3_b, w_ffn2_in, w_ffn2_out, ln4_g, ln4_b):
    batch, seq, _ = x.shape
    n_mem = mem.shape[1]
    assert w_ffn1_in.shape[0] == DEPTH == 1
    bf = lambda w: w.astype(BF16)
    h = x.reshape(batch * seq, D_MODEL)
    mem2d = mem.reshape(batch * n_mem, D_MODEL)
    for l in range(DEPTH):
        later = (w_mix_in, pool_w.reshape(DEPTH, D_POOL, POOL_GROUP_DIM), w_mix_out, xa_wq, xa_wk, xa_wv, xa_wo,
                 w_ffn2_in, w_ffn2_out)
        h, later = _ffn_ln(h, bf(w_ffn1_in[l]), bf(w_ffn1_out[l]), ln1_g[l:l + 1], ln1_b[l:l + 1],
                           side=later, layer=l)
        mix_in, pool, mix_out, wq, wk, wv, wo, ffn2_in, ffn2_out = later
        pool = pool.reshape(len(POOL_WINDOWS), POOL_GROUP_DIM, POOL_GROUP_DIM)
        h = _mixer_ln(h, batch, seq, mix_in, hgrn_lb, hgrn_gnorm[l:l + 1], pool,
                      pool_scale[l:l + 1], mix_out, ln2_g[l:l + 1], ln2_b[l:l + 1])
        k2d, v2d = _kv_proj(mem2d, n_mem, wk, wv)
        h = _xattn_ln(h, batch, seq, k2d, v2d, n_mem, wq, wo, ln3_g[l:l + 1], ln3_b[l:l + 1])
        h, _ = _ffn_ln(h, ffn2_in, ffn2_out, ln4_g[l:l + 1], ln4_b[l:l + 1])
    return h.reshape(batch, seq, D_MODEL)
```

```python
import functools

import numpy as np
import jax
import jax.numpy as jnp
from jax import lax
from jax.experimental import pallas as pl
from jax.experimental.pallas import tpu as pltpu

F32 = jnp.float32
BF16 = jnp.bfloat16

D_MODEL = 1024
DEPTH = 1
D_HGRN = 512
D_POOL = 512
HGRN_HEADS = 4
HEAD_DIM = 128
POOL_WINDOWS = (2, 4, 8, 16)
POOL_GROUP_DIM = 128
D_FF = 2816
D_IN_PROJ = 4 * D_HGRN + D_POOL
XA_HEADS = 4
XA_HEAD_DIM = 256
ALPHA = (2.0 * DEPTH) ** 0.25
LN_EPS = 1e-5
RMS_EPS = 1e-6

V7X_SUBLANES = 8
V7X_LANES = 128
V7X_MXU_DIM = 256
V7X_VMEM_LIMIT = 56 * 1024 * 1024

CHUNK = 64
SUB = V7X_SUBLANES
POOL_HALO = 16
FFN_TILE = 512
FFN_LN_PIECES = 4
MIX_TILE = 512
XA_TILE = 512
XA_LN_PIECES = 4
XA_LN_FIRST = 2
FF_CHUNKS = tuple((c0, min(512, D_FF - c0)) for c0 in range(0, D_FF, 512))


def _dot(a, b):
    return jnp.dot(a, b, preferred_element_type=F32)


def _dot_nt(a, b):
    return lax.dot_general(a, b, (((1,), (1,)), ((), ())), preferred_element_type=F32)


def _dot_tn(a, b):
    return lax.dot_general(a, b, (((0,), (0,)), ((), ())), preferred_element_type=F32)


def _silu(x):
    return x * jax.nn.sigmoid(x)


def _layernorm(y, g, b):
    mu = jnp.mean(y, axis=-1, keepdims=True)
    d = y - mu
    var = jnp.mean(d * d, axis=-1, keepdims=True)
    return d * lax.rsqrt(var + LN_EPS) * g + b


def _zero_after(v):
    sub = V7X_SUBLANES * (4 // v.dtype.itemsize)
    r, c = v.shape
    m = jnp.max(v.reshape(r // sub, sub, c), axis=0)
    m = functools.reduce(jnp.maximum, [m[:, j * V7X_LANES:(j + 1) * V7X_LANES] for j in range(c // V7X_LANES)])
    m = m.astype(F32)
    if sub > V7X_SUBLANES:
        m = jnp.maximum(m[:V7X_SUBLANES], m[V7X_SUBLANES:])
    bits = pltpu.bitcast(m, jnp.uint32)
    return pltpu.bitcast((bits >> 16) >> 16, F32)


def _resident(shape):
    zeros = (0,) * len(shape)
    return pl.BlockSpec(shape, lambda *_: zeros, pipeline_mode=pl.Buffered(1))


def _ffn_ln_kernel(*refs, n_tiles, n_side):
    x_ref, win_ref, wout_ref, g_ref, b_ref = refs[:5]
    side_in = refs[5:5 + n_side]
    o_ref = refs[5 + n_side]
    side_out = refs[6 + n_side:6 + 2 * n_side]
    y_s = refs[-1]
    step = pl.program_id(0)
    ln_rows = FFN_TILE // FFN_LN_PIECES
    gaps = len(FF_CHUNKS) - 1

    def ln_piece(p):
        rows = slice(p * ln_rows, (p + 1) * ln_rows)
        out = _layernorm(y_s[rows, :], g_ref[...], b_ref[...])
        o_ref[rows, :] = out
        return out

    def cast_side(n):
        v = side_in[n][...].astype(BF16)
        side_out[n][...] = v
        return v

    @pl.when(step == 0)
    def _():
        y_s[...] = jnp.zeros_like(y_s)

    @pl.when(step < n_tiles)
    def _():
        x = x_ref[...]
        xb = x.astype(BF16)
        acc = None
        jobs = [(FFN_TILE * D_MODEL // FFN_LN_PIECES * 3, functools.partial(ln_piece, p)) for p in range(FFN_LN_PIECES)]
        jobs += [(side_in[n].shape[0] * side_in[n].shape[1], functools.partial(cast_side, n)) for n in range(n_side)]
        bins = [[0, []] for _ in range(gaps)]
        for cost, job in sorted(jobs, key=lambda cj: -cj[0]):
            target = min(bins, key=lambda bn: bn[0])
            target[0] += cost
            target[1].append(job)
        lhs = xb
        for n, (c0, cw) in enumerate(FF_CHUNKS):
            gate = _dot(lhs, win_ref[:, c0:c0 + cw])
            up = _dot(lhs, win_ref[:, D_FF + c0:D_FF + c0 + cw])
            act = (_silu(gate) * up).astype(BF16)
            part = _dot(act, wout_ref[c0:c0 + cw, :])
            acc = part if acc is None else acc + part
            if n < gaps and bins[n][1]:
                zeros = [_zero_after(job()) for job in bins[n][1]]
                zero = sum(zeros[1:], zeros[0]).astype(BF16)
                lhs = xb + jnp.tile(zero, (FFN_TILE // V7X_SUBLANES, D_MODEL // V7X_LANES))
        y_s[...] = ALPHA * x + 0.5 * acc

    @pl.when(step == n_tiles)
    def _():
        for p in range(FFN_LN_PIECES):
            ln_piece(p)


def _side_blocks(w, layer, n_steps):
    _, r, c = w.shape
    packed_rows = 2 * V7X_SUBLANES
    rows = next(n for n in range(packed_rows, r + 1, packed_rows) if r % n == 0 and r // n <= n_steps)
    index = lambda i: (jnp.minimum(i, r // rows - 1), 0)
    return (pl.BlockSpec((None, rows, c), lambda i: (layer,) + index(i)), pl.BlockSpec((rows, c), index),
            jax.ShapeDtypeStruct((r, c), BF16))


def _ffn_ln(x2d, w_in, w_out, g, b, side=(), layer=0):
    m = x2d.shape[0]
    assert m % FFN_TILE == 0
    n_tiles = m // FFN_TILE
    current = lambda i: (jnp.minimum(i, n_tiles - 1), 0)
    blocks = [_side_blocks(w, layer, n_tiles) for w in side]
    out = pl.pallas_call(
        functools.partial(_ffn_ln_kernel, n_tiles=n_tiles, n_side=len(side)),
        grid=(n_tiles + 1,),
        in_specs=[
            pl.BlockSpec((FFN_TILE, D_MODEL), current),
            _resident((D_MODEL, 2 * D_FF)),
            _resident((D_FF, D_MODEL)),
            _resident((1, D_MODEL)),
            _resident((1, D_MODEL)),
        ] + [blk[0] for blk in blocks],
        out_specs=[pl.BlockSpec((FFN_TILE, D_MODEL), lambda i: (jnp.maximum(i - 1, 0), 0))]
        + [blk[1] for blk in blocks],
        out_shape=[jax.ShapeDtypeStruct((m, D_MODEL), F32)] + [blk[2] for blk in blocks],
        scratch_shapes=[pltpu.VMEM((FFN_TILE, D_MODEL), F32)],
        compiler_params=pltpu.CompilerParams(
            dimension_semantics=("arbitrary",), vmem_limit_bytes=V7X_VMEM_LIMIT),
        name="ffn_ln",
    )(x2d, w_in, w_out, g, b, *side)
    return out[0], out[1:]


N_BLK = CHUNK // SUB
FAR_ORDER = (7, 6, 3, 5, 4, 2, 1)
FAR_COLS = 256
NEAR_LEVELS = (0, 1, 2, 4)
PV_COLS = FAR_COLS + len(NEAR_LEVELS) * CHUNK
assert sorted(FAR_ORDER) == list(range(1, N_BLK)) and SUB * sum(FAR_ORDER) <= FAR_COLS


def _chunk_constants():
    t = np.arange(CHUNK)[:, None]
    s = np.arange(CHUNK)[None, :]
    tri = (s <= t).astype(np.float32)
    mask = np.zeros((CHUNK, PV_COLS), np.float32)
    off = 0
    for i in FAR_ORDER:
        mask[:, off:off + SUB * i] = (t // SUB == i)
        off += SUB * i
    for n, b in enumerate(NEAR_LEVELS):
        valid = (t == s) if b == 0 else (((t // b) % 2 == 1) & ((s // b) == (t // b) - 1))
        mask[:, FAR_COLS + n * CHUNK:FAR_COLS + (n + 1) * CHUNK] = valid
    return jnp.asarray(tri, BF16), jnp.asarray(mask, BF16)


def _split3(x):
    hi = x.astype(BF16)
    r = x - hi.astype(F32)
    mid = r.astype(BF16)
    lo = (r - mid.astype(F32)).astype(BF16)
    return hi, mid, lo


def _mixer_ln_kernel(hn_ref, hc_ref, win_ref, lb_ref, gn_ref, pw_ref, ps_ref, wout_ref, g_ref, b_ref,
                     tri_ref, mask_ref, o_ref,
                     q_a, k_a, f_a, i_a, cum_a, gate_a, vb_a,
                     q_s, k_s, f_s, i_s, cum_s, gate_s, vb_s, hb_s, merged_s, y_s, state_s, *, n_seq):
    ts = hn_ref.shape[0]
    step = pl.program_id(0)
    seq_tile_next = step % n_seq
    seq_tile = (step + n_seq - 1) % n_seq
    handoff = ((q_a, q_s), (k_a, k_s), (f_a, f_s), (i_a, i_s), (cum_a, cum_s), (gate_a, gate_s), (vb_a, vb_s))

    @pl.when(step == 0)
    def _():
        state_s[...] = jnp.zeros_like(state_s)
        for src, _ in handoff:
            src[...] = jnp.zeros_like(src)

    half = ts // 2
    col_blk = V7X_MXU_DIM

    def copy_rows(c):
        rows = slice(c * CHUNK, (c + 1) * CHUNK)
        for src, dst in handoff[:-1]:
            dst[rows, :] = src[rows, :]
        if c == 0:
            vb_s[0:POOL_HALO, :] = vb_a[0:POOL_HALO, :]
        prow = slice(POOL_HALO + c * CHUNK, POOL_HALO + (c + 1) * CHUNK)
        vb_s[prow, :] = vb_a[prow, :]

    def pool_halo():
        vb_a[0:POOL_HALO, :] = jnp.where(seq_tile_next == 0, 0.0, vb_s[ts:ts + POOL_HALO, :])

    a = lb_ref[...]
    e = jnp.exp(a - jnp.max(a, axis=0, keepdims=True))
    lower = e[0:1, :] / jnp.sum(e, axis=0, keepdims=True)

    def cast_next(r):
        rows = slice(r * half, (r + 1) * half)
        hb_s[rows, :] = hn_ref[rows, :].astype(BF16)

    def in_proj(r, j):
        rows = slice(r * half, (r + 1) * half)
        x = _dot(hb_s[rows, :], win_ref[:, j * col_blk:(j + 1) * col_blk])
        kind, cb = divmod(j * col_blk, D_HGRN)
        cols = slice(cb, cb + col_blk)
        if kind == 0:
            q_a[rows, cols] = _silu(x)
        elif kind == 1:
            forget = lower[:, cols] + (1.0 - lower[:, cols]) * jax.nn.sigmoid(x)
            f_a[rows, cols] = forget
            k_a[rows, cols] = 1.0 - forget
            cum_a[rows, cols] = jnp.log(forget)
        elif kind == 2:
            i_a[rows, cols] = x
        elif kind == 3:
            gate_a[rows, cols] = _silu(x)
        else:
            vb_a[POOL_HALO + r * half:POOL_HALO + (r + 1) * half, cols] = x

    def cum_chunks(c0, n):
        tri = tri_ref[...]
        for c in range(c0, c0 + n):
            rows = slice(c * CHUNK, (c + 1) * CHUNK)
            hi, mid, lo = _split3(cum_a[rows, :])
            cum_a[rows, :] = _dot(tri, hi) + _dot(tri, mid) + _dot(tri, lo)

    sub_row = lax.broadcasted_iota(jnp.int32, (SUB, HEAD_DIM), 0)
    zero_blk = jnp.zeros((SUB, HEAD_DIM), F32)
    far_pad = jnp.zeros((FAR_COLS - SUB * sum(FAR_ORDER), HEAD_DIM), F32)
    gn = gn_ref[...]
    zero_kb = jnp.zeros((CHUNK, HEAD_DIM), BF16)

    def block_diag(a, b):
        return jnp.concatenate([jnp.concatenate([a, zero_kb], axis=1),
                                jnp.concatenate([zero_kb, b], axis=1)], axis=0)

    def head_cols(hd):
        return slice(hd * HEAD_DIM, (hd + 1) * HEAD_DIM)

    def scores_stage(c, hd):
        r0 = c * CHUNK
        rows = pl.ds(r0, CHUNK)
        cols = head_cols(hd)
        q = q_s[rows, cols]
        k = k_s[rows, cols]
        f = f_s[rows, cols]
        v = i_s[rows, cols]
        cum = cum_s[rows, cols]

        def cum_row(row):
            return jnp.broadcast_to(cum_s[pl.ds(r0 + row, 1), cols], (SUB, HEAD_DIM))

        bound = [zero_blk] + [cum_row(SUB * j - 1) for j in range(1, N_BLK + 1)]

        q_far = (q * jnp.exp(cum - jnp.concatenate(bound[:N_BLK], axis=0))).astype(BF16)
        k_parts, v_parts = [], []
        for i in FAR_ORDER:
            n = SUB * i
            k_parts.append(k[:n] * jnp.exp(jnp.concatenate([bound[i]] * i, axis=0) - cum[:n]))
            v_parts.append(v[:n])
        k_far = jnp.concatenate(k_parts + [far_pad], axis=0).astype(BF16)
        s_far = _dot_nt(q_far, k_far)

        ref_q2, ref_k2, ref_q4, ref_k4 = [], [], [], []
        for j in range(N_BLK):
            lo_b, hi_b = bound[j], bound[j + 1]
            m1, m3, m5 = (cum_row(SUB * j + r) for r in (1, 3, 5))
            ref_q4.append(jnp.where(sub_row < 4, lo_b, m3))
            ref_k4.append(jnp.where(sub_row < 4, m3, hi_b))
            ref_q2.append(jnp.where(sub_row < 2, lo_b, jnp.where(sub_row < 4, m1, jnp.where(sub_row < 6, m3, m5))))
            ref_k2.append(jnp.where(sub_row < 2, m1, jnp.where(sub_row < 4, m3, jnp.where(sub_row < 6, m5, hi_b))))
        cat = lambda parts: jnp.concatenate(parts, axis=0)
        q2 = q * jnp.exp(cum - cat(ref_q2))
        k2 = k * jnp.exp(cat(ref_k2) - cum)
        q4 = q * jnp.exp(cum - cat(ref_q4))
        k4 = k * jnp.exp(cat(ref_k4) - cum)
        kb = k.astype(BF16)
        s_a = _dot_nt(jnp.concatenate([q, q * f], axis=1).astype(BF16), block_diag(kb, kb))
        s_b = _dot_nt(jnp.concatenate([q2, q4], axis=1).astype(BF16),
                      block_diag(k2.astype(BF16), k4.astype(BF16)))

        total = bound[N_BLK]
        return dict(
            p=jnp.concatenate([s_far, s_a, s_b], axis=1).astype(BF16) * mask_ref[...],
            v_all=jnp.concatenate(v_parts + [far_pad] + [v] * len(NEAR_LEVELS), axis=0).astype(BF16),
            q_dec=(q * jnp.exp(cum)).astype(BF16),
            k_dec=(k * jnp.exp(jnp.concatenate([total] * N_BLK, axis=0) - cum)).astype(BF16),
            vb=v.astype(BF16),
            decay=jnp.exp(total[0:1, :]))

    def state_stage(c, hd, sc):
        st = state_s[hd]
        if c == 0:
            st = jnp.where(seq_tile == 0, 0.0, st)
        o = _dot(sc["p"], sc["v_all"]) + _dot_nt(sc["q_dec"], st.astype(BF16))
        state_s[hd] = st * sc["decay"] + _dot_tn(sc["vb"], sc["k_dec"])
        return o

    def norm_stage(c, hd, o):
        rows = pl.ds(c * CHUNK, CHUNK)
        cols = head_cols(hd)
        o = o * lax.rsqrt(jnp.mean(o * o, axis=-1, keepdims=True) + RMS_EPS) * gn
        merged_s[rows, cols] = (o * gate_s[rows, cols]).astype(BF16)

    quarter = ts // 4

    def pool_rows(qd):
        r0, nr = qd * quarter, quarter
        pos = seq_tile * ts + r0 + lax.broadcasted_iota(jnp.int32, (nr, 1), 0) + 1
        for grp, win in enumerate(POOL_WINDOWS):
            cols = slice(grp * POOL_GROUP_DIM, (grp + 1) * POOL_GROUP_DIM)
            ext = vb_s[r0:r0 + POOL_HALO + nr, cols]
            wsum = ext
            span = 1
            while span < win:
                wsum = wsum + pltpu.roll(wsum, span, 0)
                span *= 2
            cur = ext[POOL_HALO:]
            count = jnp.minimum(pos, win).astype(F32)
            pooled = wsum[POOL_HALO:] / count - cur
            mixed = _dot(pooled.astype(BF16), pw_ref[grp]) * ps_ref[:, cols]
            merged_s[r0:r0 + nr, D_HGRN + grp * POOL_GROUP_DIM:D_HGRN + (grp + 1) * POOL_GROUP_DIM] = (
                mixed.astype(BF16))

    def out_proj(qd, n2):
        rows = slice(qd * quarter, (qd + 1) * quarter)
        cols = slice(n2 * (D_MODEL // 2), (n2 + 1) * (D_MODEL // 2))
        y_s[rows, cols] = _dot(merged_s[rows, :], wout_ref[:, cols])

    def out_ln(qd):
        rows = slice(qd * quarter, (qd + 1) * quarter)
        o_ref[rows, :] = _layernorm(ALPHA * hc_ref[rows, :] + y_s[rows, :], g_ref[...], b_ref[...])

    n_chunks = ts // CHUNK
    n_slots = n_chunks * HGRN_HEADS
    P = functools.partial
    COPY, PROJ, CUM, POOL, OUT, LN = 100, 256, 192, 200, 256, 150
    early = [(0, COPY, P(copy_rows, c)) for c in range(2, n_chunks)] + [(0, 10, pool_halo)]
    for r in range(2):
        blocks = list(range(D_IN_PROJ // col_blk))
        forget_blocks = [j for j in blocks if j * col_blk // D_HGRN == 1]
        first = 2 + 2 * r
        early += [(first, COPY, P(cast_next, r))]
        early += [(first, PROJ, P(in_proj, r, j)) for j in forget_blocks]
        early += [(first, CUM, P(cum_chunks, r * n_chunks // 2 + 2 * i, 2)) for i in range(n_chunks // 4)]
        early += [(first, PROJ, P(in_proj, r, j)) for j in blocks if j not in forget_blocks]
    late = []
    for qd in range(4):
        first = (qd + 1) * (n_chunks // 4) * HGRN_HEADS
        late += [(first, POOL, P(pool_rows, qd)), (first, OUT, P(out_proj, qd, 0)),
                 (first, OUT, P(out_proj, qd, 1)), (first, LN, P(out_ln, qd))]
    total = sum(cost for _, cost, _ in early + late)
    copy_rows(0)
    copy_rows(1)
    done = 0
    ch = lambda slot: (slot // HGRN_HEADS, slot % HGRN_HEADS)
    scores, outs = {}, {}
    for slot in range(n_slots + 2):
        if slot < n_slots:
            scores[slot] = scores_stage(*ch(slot))
        if 1 <= slot <= n_slots:
            outs[slot - 1] = state_stage(*ch(slot - 1), scores.pop(slot - 1))
        if slot >= 2:
            norm_stage(*ch(slot - 2), outs.pop(slot - 2))
        while done < total * (slot + 1) / n_slots:
            queue = late if late and late[0][0] <= slot - 1 else early
            if not queue or queue[0][0] > slot - 1:
                break
            _, cost, fn = queue.pop(0)
            fn()
            done += cost
    for _, _, fn in early + late:
        fn()


def _mixer_ln(h2d, batch, seq, w_in, lb, gnorm, pool_w, pool_scale, w_out, g, b):
    ts = MIX_TILE
    assert seq % ts == 0 and ts % CHUNK == 0
    n_seq = seq // ts
    n_tiles = batch * n_seq
    tri, mask = _chunk_constants()
    next_tile = pl.BlockSpec((ts, D_MODEL), lambda i: (jnp.minimum(i, n_tiles - 1), 0))
    done_tile = pl.BlockSpec((ts, D_MODEL), lambda i: (jnp.maximum(i - 1, 0), 0))
    proj_set = [pltpu.VMEM((ts, D_HGRN), F32)] * 6 + [pltpu.VMEM((ts + POOL_HALO, D_POOL), F32)]
    return pl.pallas_call(
        functools.partial(_mixer_ln_kernel, n_seq=n_seq),
        grid=(n_tiles + 1,),
        in_specs=[
            next_tile,
            done_tile,
            _resident(w_in.shape),
            _resident(lb.shape),
            _resident(gnorm.shape),
            _resident(pool_w.shape),
            _resident(pool_scale.shape),
            _resident(w_out.shape),
            _resident(g.shape),
            _resident(b.shape),
            _resident(tri.shape),
            _resident(mask.shape),
        ],
        out_specs=done_tile,
        out_shape=jax.ShapeDtypeStruct(h2d.shape, F32),
        scratch_shapes=proj_set + proj_set + [
            pltpu.VMEM((ts, D_MODEL), BF16),
            pltpu.VMEM((ts, D_HGRN + D_POOL), BF16),
            pltpu.VMEM((ts, D_MODEL), F32),
            pltpu.VMEM((HGRN_HEADS, HEAD_DIM, HEAD_DIM), F32),
        ],
        compiler_params=pltpu.CompilerParams(
            dimension_semantics=("arbitrary",), vmem_limit_bytes=V7X_VMEM_LIMIT),
        name="mixer_ln",
    )(h2d, h2d, w_in, lb, gnorm, pool_w, pool_scale, w_out, g, b, tri, mask)


def _kv_proj_kernel(*refs, n_side):
    mem_ref, wk_ref, wv_ref = refs[:3]
    side_in = refs[3:3 + n_side]
    k_ref, v_ref = refs[3 + n_side:5 + n_side]
    side_out = refs[5 + n_side:5 + 2 * n_side]
    wk_b, wv_b = refs[-2:]

    @pl.when(pl.program_id(0) == 0)
    def _():
        wk_b[...] = wk_ref[...].astype(BF16)
        wv_b[...] = wv_ref[...].astype(BF16)

    mb = mem_ref[...].astype(BF16)
    k_ref[...] = _dot(mb, wk_b[...]).astype(BF16)
    v_ref[...] = _dot(mb, wv_b[...]).astype(BF16)
    for src, dst in zip(side_in, side_out):
        dst[...] = src[...].astype(BF16)


def _kv_proj(mem2d, n_mem, wk, wv, side=(), layer=0):
    rows = mem2d.shape[0]
    n_steps = rows // n_mem
    tile = pl.BlockSpec((n_mem, D_MODEL), lambda i: (i, 0))
    weight = pl.BlockSpec((None, D_MODEL, D_MODEL), lambda i: (layer, 0, 0), pipeline_mode=pl.Buffered(1))
    blocks = [_side_blocks(w, layer, n_steps) for w in side]
    out = pl.pallas_call(
        functools.partial(_kv_proj_kernel, n_side=len(side)),
        grid=(n_steps,),
        in_specs=[tile, weight, weight] + [blk[0] for blk in blocks],
        out_specs=[tile, tile] + [blk[1] for blk in blocks],
        out_shape=[jax.ShapeDtypeStruct((rows, D_MODEL), BF16)] * 2 + [blk[2] for blk in blocks],
        scratch_shapes=[pltpu.VMEM((D_MODEL, D_MODEL), BF16)] * 2,
        compiler_params=pltpu.CompilerParams(
            dimension_semantics=("arbitrary",), vmem_limit_bytes=V7X_VMEM_LIMIT),
        name="kv_proj",
    )(mem2d, wk, wv, *side)
    return out[0], out[1], out[2:]


def _xattn_ln_kernel(h_ref, k_ref, v_ref, wq_ref, wo_ref, g_ref, b_ref, o_ref, y_s, *, n_tiles):
    step = pl.program_id(0)
    ts = h_ref.shape[0]
    ln_rows = ts // XA_LN_PIECES

    def ln_piece(p):
        rows = slice(p * ln_rows, (p + 1) * ln_rows)
        out = _layernorm(y_s[rows, :], g_ref[...], b_ref[...])
        o_ref[rows, :] = out
        return out

    @pl.when(step == 0)
    def _():
        y_s[...] = jnp.zeros_like(y_s)

    @pl.when(step < n_tiles)
    def _():
        h = h_ref[...]
        hb = h.astype(BF16)
        q = _dot(hb, wq_ref[...])
        pieces = list(range(XA_LN_PIECES))
        heads = []
        for hd in range(XA_HEADS):
            cols = slice(hd * XA_HEAD_DIM, (hd + 1) * XA_HEAD_DIM)
            qh = q[:, cols].astype(BF16)
            if pieces:
                take = min(len(pieces), XA_LN_FIRST if hd == 0 else 1)
                zeros = [_zero_after(ln_piece(pieces.pop(0))) for _ in range(take)]
                zero = sum(zeros[1:], zeros[0]).astype(BF16)
                qh = qh + jnp.tile(zero, (ts // V7X_SUBLANES, XA_HEAD_DIM // V7X_LANES))
            s = _dot_nt(qh, k_ref[:, cols]) * (XA_HEAD_DIM ** -0.5)
            s = s - jnp.max(s, axis=-1, keepdims=True)
            p = jnp.exp(s)
            p = p / jnp.sum(p, axis=-1, keepdims=True)
            heads.append(_dot(p.astype(BF16), v_ref[:, cols]).astype(BF16))
        assert not pieces
        y_s[...] = ALPHA * h + _dot(jnp.concatenate(heads, axis=-1), wo_ref[...])

    @pl.when(step == n_tiles)
    def _():
        for p in range(XA_LN_PIECES):
            ln_piece(p)


def _xattn_ln(h2d, batch, seq, k2d, v2d, n_mem, wq, wo, g, b):
    ts = XA_TILE
    assert seq % ts == 0
    n_seq = seq // ts
    n_tiles = batch * n_seq
    attended = lambda i: jnp.minimum(i, n_tiles - 1)
    tile = pl.BlockSpec((ts, D_MODEL), lambda i: (attended(i), 0))
    mem_tile = pl.BlockSpec((n_mem, D_MODEL), lambda i: (attended(i) // n_seq, 0))
    return pl.pallas_call(
        functools.partial(_xattn_ln_kernel, n_tiles=n_tiles),
        grid=(n_tiles + 1,),
        in_specs=[tile, mem_tile, mem_tile, _resident(wq.shape), _resident(wo.shape),
                  _resident(g.shape), _resident(b.shape)],
        out_specs=pl.BlockSpec((ts, D_MODEL), lambda i: (jnp.maximum(i - 1, 0), 0)),
        out_shape=jax.ShapeDtypeStruct(h2d.shape, F32),
        scratch_shapes=[pltpu.VMEM((ts, D_MODEL), F32)],
        compiler_params=pltpu.CompilerParams(
            dimension_semantics=("arbitrary",), vmem_limit_bytes=V7X_VMEM_LIMIT),
        name="xattn_ln",
    )(h2d, k2d, v2d, wq, wo, g, b)


def kernel(x, mem, w_ffn1_in, w_ffn1_out, ln1_g, ln1_b, w_mix_in, hgrn_lb, hgrn_gnorm, pool_w, pool_scale, w_mix_out, ln2_g, ln2_b, xa_wq, xa_wk, xa_wv, xa_wo, ln3_g, ln3_b, w_ffn2_in, w_ffn2_out, ln4_g, ln4_b):
    batch, seq, _ = x.shape
    n_mem = mem.shape[1]
    assert w_ffn1_in.shape[0] == DEPTH == 1
    h = x.reshape(batch * seq, D_MODEL)
    mem2d = mem.reshape(batch * n_mem, D_MODEL)
    for l in range(DEPTH):
        k2d, v2d, (ffn1_in, ffn1_out) = _kv_proj(mem2d, n_mem, xa_wk, xa_wv, side=(w_ffn1_in, w_ffn1_out), layer=l)
        later = (w_mix_in, pool_w.reshape(DEPTH, D_POOL, POOL_GROUP_DIM), w_mix_out, xa_wq, xa_wo,
                 w_ffn2_in, w_ffn2_out)
        h, later = _ffn_ln(h, ffn1_in, ffn1_out, ln1_g[l:l + 1], ln1_b[l:l + 1], side=later, layer=l)
        mix_in, pool, mix_out, wq, wo, ffn2_in, ffn2_out = later
        pool = pool.reshape(len(POOL_WINDOWS), POOL_GROUP_DIM, POOL_GROUP_DIM)
        h = _mixer_ln(h, batch, seq, mix_in, hgrn_lb, hgrn_gnorm[l:l + 1], pool,
                      pool_scale[l:l + 1], mix_out, ln2_g[l:l + 1], ln2_b[l:l + 1])
        h = _xattn_ln(h, batch, seq, k2d, v2d, n_mem, wq, wo, ln3_g[l:l + 1], ln3_b[l:l + 1])
        h, _ = _ffn_ln(h, ffn2_in, ffn2_out, ln4_g[l:l + 1], ln4_b[l:l + 1])
    return h.reshape(batch, seq, D_MODEL)
```

```python
import functools

import numpy as np
import jax
import jax.numpy as jnp
from jax import lax
from jax.experimental import pallas as pl
from jax.experimental.pallas import tpu as pltpu

F32 = jnp.float32
BF16 = jnp.bfloat16

D_MODEL = 1024
DEPTH = 1
D_HGRN = 512
D_POOL = 512
HGRN_HEADS = 4
HEAD_DIM = 128
POOL_WINDOWS = (2, 4, 8, 16)
POOL_GROUP_DIM = 128
D_FF = 2816
D_IN_PROJ = 4 * D_HGRN + D_POOL
XA_HEADS = 4
XA_HEAD_DIM = 256
ALPHA = (2.0 * DEPTH) ** 0.25
LN_EPS = 1e-5
RMS_EPS = 1e-6

V7X_SUBLANES = 8
V7X_LANES = 128
V7X_MXU_DIM = 256
V7X_VMEM_LIMIT = 56 * 1024 * 1024

CHUNK = 64
SUB = V7X_SUBLANES
POOL_HALO = 16
FFN_TILE = 512
FFN_LN_PIECES = 4
MIX_TILE = 512
XA_TILE = 512
XA_LN_PIECES = 4
XA_LN_FIRST = 2
FF_CHUNKS = tuple((c0, min(512, D_FF - c0)) for c0 in range(0, D_FF, 512))


def _dot(a, b):
    return jnp.dot(a, b, preferred_element_type=F32)


def _dot_nt(a, b):
    return lax.dot_general(a, b, (((1,), (1,)), ((), ())), preferred_element_type=F32)


def _dot_tn(a, b):
    return lax.dot_general(a, b, (((0,), (0,)), ((), ())), preferred_element_type=F32)


def _silu(x):
    return x * jax.nn.sigmoid(x)


def _layernorm(y, g, b):
    mu = jnp.mean(y, axis=-1, keepdims=True)
    d = y - mu
    var = jnp.mean(d * d, axis=-1, keepdims=True)
    return d * lax.rsqrt(var + LN_EPS) * g + b


def _zero_after(v):
    sub = V7X_SUBLANES * (4 // v.dtype.itemsize)
    r, c = v.shape
    m = jnp.max(v.reshape(r // sub, sub, c), axis=0)
    m = functools.reduce(jnp.maximum, [m[:, j * V7X_LANES:(j + 1) * V7X_LANES] for j in range(c // V7X_LANES)])
    m = m.astype(F32)
    if sub > V7X_SUBLANES:
        m = jnp.maximum(m[:V7X_SUBLANES], m[V7X_SUBLANES:])
    bits = pltpu.bitcast(m, jnp.uint32)
    return pltpu.bitcast((bits >> 16) >> 16, F32)


def _resident(shape):
    zeros = (0,) * len(shape)
    return pl.BlockSpec(shape, lambda *_: zeros, pipeline_mode=pl.Buffered(1))


def _ffn_ln_kernel(*refs, n_tiles, n_side):
    x_ref, win_ref, wout_ref, g_ref, b_ref = refs[:5]
    side_in = refs[5:5 + n_side]
    o_ref = refs[5 + n_side]
    side_out = refs[6 + n_side:6 + 2 * n_side]
    y_s = refs[-1]
    step = pl.program_id(0)
    ln_rows = FFN_TILE // FFN_LN_PIECES
    gaps = len(FF_CHUNKS) - 1

    def ln_piece(p):
        rows = slice(p * ln_rows, (p + 1) * ln_rows)
        out = _layernorm(y_s[rows, :], g_ref[...], b_ref[...])
        o_ref[rows, :] = out
        return out

    def cast_side(n):
        v = side_in[n][...].astype(BF16)
        side_out[n][...] = v
        return v

    @pl.when(step == 0)
    def _():
        y_s[...] = jnp.zeros_like(y_s)

    @pl.when(step < n_tiles)
    def _():
        x = x_ref[...]
        xb = x.astype(BF16)
        acc = None
        jobs = [(FFN_TILE * D_MODEL // FFN_LN_PIECES * 3, functools.partial(ln_piece, p)) for p in range(FFN_LN_PIECES)]
        jobs += [(side_in[n].shape[0] * side_in[n].shape[1], functools.partial(cast_side, n)) for n in range(n_side)]
        bins = [[0, []] for _ in range(gaps)]
        for cost, job in sorted(jobs, key=lambda cj: -cj[0]):
            target = min(bins, key=lambda bn: bn[0])
            target[0] += cost
            target[1].append(job)
        lhs = xb
        for n, (c0, cw) in enumerate(FF_CHUNKS):
            gate = _dot(lhs, win_ref[:, c0:c0 + cw])
            up = _dot(lhs, win_ref[:, D_FF + c0:D_FF + c0 + cw])
            act = (_silu(gate) * up).astype(BF16)
            part = _dot(act, wout_ref[c0:c0 + cw, :])
            acc = part if acc is None else acc + part
            if n < gaps and bins[n][1]:
                zeros = [_zero_after(job()) for job in bins[n][1]]
                zero = sum(zeros[1:], zeros[0]).astype(BF16)
                lhs = xb + jnp.tile(zero, (FFN_TILE // V7X_SUBLANES, D_MODEL // V7X_LANES))
        y_s[...] = ALPHA * x + 0.5 * acc

    @pl.when(step == n_tiles)
    def _():
        for p in range(FFN_LN_PIECES):
            ln_piece(p)


def _side_blocks(w, layer, n_steps):
    _, r, c = w.shape
    packed_rows = 2 * V7X_SUBLANES
    rows = next(n for n in range(packed_rows, r + 1, packed_rows) if r % n == 0 and r // n <= n_steps)
    index = lambda i: (jnp.minimum(i, r // rows - 1), 0)
    return (pl.BlockSpec((None, rows, c), lambda i: (layer,) + index(i)), pl.BlockSpec((rows, c), index),
            jax.ShapeDtypeStruct((r, c), BF16))


def _ffn_ln(x2d, w_in, w_out, g, b, side=(), layer=0):
    m = x2d.shape[0]
    assert m % FFN_TILE == 0
    n_tiles = m // FFN_TILE
    current = lambda i: (jnp.minimum(i, n_tiles - 1), 0)
    blocks = [_side_blocks(w, layer, n_tiles) for w in side]
    out = pl.pallas_call(
        functools.partial(_ffn_ln_kernel, n_tiles=n_tiles, n_side=len(side)),
        grid=(n_tiles + 1,),
        in_specs=[
            pl.BlockSpec((FFN_TILE, D_MODEL), current),
            _resident((D_MODEL, 2 * D_FF)),
            _resident((D_FF, D_MODEL)),
            _resident((1, D_MODEL)),
            _resident((1, D_MODEL)),
        ] + [blk[0] for blk in blocks],
        out_specs=[pl.BlockSpec((FFN_TILE, D_MODEL), lambda i: (jnp.maximum(i - 1, 0), 0))]
        + [blk[1] for blk in blocks],
        out_shape=[jax.ShapeDtypeStruct((m, D_MODEL), F32)] + [blk[2] for blk in blocks],
        scratch_shapes=[pltpu.VMEM((FFN_TILE, D_MODEL), F32)],
        compiler_params=pltpu.CompilerParams(
            dimension_semantics=("arbitrary",), vmem_limit_bytes=V7X_VMEM_LIMIT),
        name="ffn_ln",
    )(x2d, w_in, w_out, g, b, *side)
    return out[0], out[1:]


N_BLK = CHUNK // SUB
FAR_ORDER = (7, 6, 3, 5, 4, 2, 1)
FAR_COLS = 256
NEAR_LEVELS = (0, 1, 2, 4)
PV_COLS = FAR_COLS + len(NEAR_LEVELS) * CHUNK
assert sorted(FAR_ORDER) == list(range(1, N_BLK)) and SUB * sum(FAR_ORDER) <= FAR_COLS


def _chunk_constants():
    t = np.arange(CHUNK)[:, None]
    s = np.arange(CHUNK)[None, :]
    tri = (s <= t).astype(np.float32)
    mask = np.zeros((CHUNK, PV_COLS), np.float32)
    off = 0
    for i in FAR_ORDER:
        mask[:, off:off + SUB * i] = (t // SUB == i)
        off += SUB * i
    for n, b in enumerate(NEAR_LEVELS):
        valid = (t == s) if b == 0 else (((t // b) % 2 == 1) & ((s // b) == (t // b) - 1))
        mask[:, FAR_COLS + n * CHUNK:FAR_COLS + (n + 1) * CHUNK] = valid
    return jnp.asarray(tri, BF16), jnp.asarray(mask, BF16)


def _split3(x):
    hi = x.astype(BF16)
    r = x - hi.astype(F32)
    mid = r.astype(BF16)
    lo = (r - mid.astype(F32)).astype(BF16)
    return hi, mid, lo


def _mixer_ln_kernel(hn_ref, hc_ref, win_ref, lb_ref, gn_ref, pw_ref, ps_ref, wout_ref, g_ref, b_ref,
                     tri_ref, mask_ref, o_ref,
                     q_a, k_a, f_a, i_a, cum_a, gate_a, vb_a,
                     q_s, k_s, f_s, i_s, cum_s, gate_s, vb_s, hb_s, merged_s, y_s, state_s, *, n_seq):
    ts = hn_ref.shape[0]
    step = pl.program_id(0)
    seq_tile_next = step % n_seq
    seq_tile = (step + n_seq - 1) % n_seq
    handoff = ((q_a, q_s), (k_a, k_s), (f_a, f_s), (i_a, i_s), (cum_a, cum_s), (gate_a, gate_s), (vb_a, vb_s))

    @pl.when(step == 0)
    def _():
        state_s[...] = jnp.zeros_like(state_s)
        for src, _ in handoff:
            src[...] = jnp.zeros_like(src)

    col_blk = V7X_MXU_DIM

    def copy_rows(c):
        rows = slice(c * CHUNK, (c + 1) * CHUNK)
        for src, dst in handoff[:-1]:
            dst[rows, :] = src[rows, :]
        if c == 0:
            vb_s[0:POOL_HALO, :] = vb_a[0:POOL_HALO, :]
        prow = slice(POOL_HALO + c * CHUNK, POOL_HALO + (c + 1) * CHUNK)
        vb_s[prow, :] = vb_a[prow, :]

    def pool_halo():
        vb_a[0:POOL_HALO, :] = jnp.where(seq_tile_next == 0, 0.0, vb_s[ts:ts + POOL_HALO, :])

    a = lb_ref[...]
    e = jnp.exp(a - jnp.max(a, axis=0, keepdims=True))
    lower = e[0:1, :] / jnp.sum(e, axis=0, keepdims=True)

    def cast_next():
        hb_s[...] = hn_ref[...].astype(BF16)

    def in_proj(j):
        x = _dot(hb_s[...], win_ref[:, j * col_blk:(j + 1) * col_blk])
        kind, cb = divmod(j * col_blk, D_HGRN)
        cols = slice(cb, cb + col_blk)
        if kind == 0:
            q_a[:, cols] = _silu(x)
        elif kind == 1:
            forget = lower[:, cols] + (1.0 - lower[:, cols]) * jax.nn.sigmoid(x)
            f_a[:, cols] = forget
            k_a[:, cols] = 1.0 - forget
            cum_a[:, cols] = jnp.log(forget)
        elif kind == 2:
            i_a[:, cols] = x
        elif kind == 3:
            gate_a[:, cols] = _silu(x)
        else:
            vb_a[POOL_HALO:POOL_HALO + ts, cols] = x

    def cum_chunks(c0, n):
        tri = tri_ref[...]
        for c in range(c0, c0 + n):
            rows = slice(c * CHUNK, (c + 1) * CHUNK)
            hi, mid, lo = _split3(cum_a[rows, :])
            cum_a[rows, :] = _dot(tri, hi) + _dot(tri, mid) + _dot(tri, lo)

    sub_row = lax.broadcasted_iota(jnp.int32, (SUB, HEAD_DIM), 0)
    zero_blk = jnp.zeros((SUB, HEAD_DIM), F32)
    far_pad = jnp.zeros((FAR_COLS - SUB * sum(FAR_ORDER), HEAD_DIM), F32)
    gn = gn_ref[...]
    zero_kb = jnp.zeros((CHUNK, HEAD_DIM), BF16)

    def block_diag(a, b):
        return jnp.concatenate([jnp.concatenate([a, zero_kb], axis=1),
                                jnp.concatenate([zero_kb, b], axis=1)], axis=0)

    def head_cols(hd):
        return slice(hd * HEAD_DIM, (hd + 1) * HEAD_DIM)

    def scores_stage(c, hd):
        r0 = c * CHUNK
        rows = pl.ds(r0, CHUNK)
        cols = head_cols(hd)
        q = q_s[rows, cols]
        k = k_s[rows, cols]
        f = f_s[rows, cols]
        v = i_s[rows, cols]
        cum = cum_s[rows, cols]

        def cum_row(row):
            return jnp.broadcast_to(cum_s[pl.ds(r0 + row, 1), cols], (SUB, HEAD_DIM))

        bound = [zero_blk] + [cum_row(SUB * j - 1) for j in range(1, N_BLK + 1)]

        q_far = (q * jnp.exp(cum - jnp.concatenate(bound[:N_BLK], axis=0))).astype(BF16)
        k_parts, v_parts = [], []
        for i in FAR_ORDER:
            n = SUB * i
            k_parts.append(k[:n] * jnp.exp(jnp.concatenate([bound[i]] * i, axis=0) - cum[:n]))
            v_parts.append(v[:n])
        k_far = jnp.concatenate(k_parts + [far_pad], axis=0).astype(BF16)
        s_far = _dot_nt(q_far, k_far)

        ref_q2, ref_k2, ref_q4, ref_k4 = [], [], [], []
        for j in range(N_BLK):
            lo_b, hi_b = bound[j], bound[j + 1]
            m1, m3, m5 = (cum_row(SUB * j + r) for r in (1, 3, 5))
            ref_q4.append(jnp.where(sub_row < 4, lo_b, m3))
            ref_k4.append(jnp.where(sub_row < 4, m3, hi_b))
            ref_q2.append(jnp.where(sub_row < 2, lo_b, jnp.where(sub_row < 4, m1, jnp.where(sub_row < 6, m3, m5))))
            ref_k2.append(jnp.where(sub_row < 2, m1, jnp.where(sub_row < 4, m3, jnp.where(sub_row < 6, m5, hi_b))))
        cat = lambda parts: jnp.concatenate(parts, axis=0)
        q2 = q * jnp.exp(cum - cat(ref_q2))
        k2 = k * jnp.exp(cat(ref_k2) - cum)
        q4 = q * jnp.exp(cum - cat(ref_q4))
        k4 = k * jnp.exp(cat(ref_k4) - cum)
        kb = k.astype(BF16)
        s_a = _dot_nt(jnp.concatenate([q, q * f], axis=1).astype(BF16), block_diag(kb, kb))
        s_b = _dot_nt(jnp.concatenate([q2, q4], axis=1).astype(BF16),
                      block_diag(k2.astype(BF16), k4.astype(BF16)))

        total = bound[N_BLK]
        return dict(
            p=jnp.concatenate([s_far, s_a, s_b], axis=1).astype(BF16) * mask_ref[...],
            v_all=jnp.concatenate(v_parts + [far_pad] + [v] * len(NEAR_LEVELS), axis=0).astype(BF16),
            q_dec=(q * jnp.exp(cum)).astype(BF16),
            k_dec=(k * jnp.exp(jnp.concatenate([total] * N_BLK, axis=0) - cum)).astype(BF16),
            vb=v.astype(BF16),
            decay=jnp.exp(total[0:1, :]))

    def state_stage(c, hd, sc):
        st = state_s[hd]
        if c == 0:
            st = jnp.where(seq_tile == 0, 0.0, st)
        o = _dot(sc["p"], sc["v_all"]) + _dot_nt(sc["q_dec"], st.astype(BF16))
        state_s[hd] = st * sc["decay"] + _dot_tn(sc["vb"], sc["k_dec"])
        return o

    def norm_stage(c, hd, o):
        rows = pl.ds(c * CHUNK, CHUNK)
        cols = head_cols(hd)
        o = o * lax.rsqrt(jnp.mean(o * o, axis=-1, keepdims=True) + RMS_EPS) * gn
        merged_s[rows, cols] = (o * gate_s[rows, cols]).astype(BF16)

    quarter = ts // 4

    def pool_rows(qd):
        r0, nr = qd * quarter, quarter
        pos = seq_tile * ts + r0 + lax.broadcasted_iota(jnp.int32, (nr, 1), 0) + 1
        for grp, win in enumerate(POOL_WINDOWS):
            cols = slice(grp * POOL_GROUP_DIM, (grp + 1) * POOL_GROUP_DIM)
            ext = vb_s[r0:r0 + POOL_HALO + nr, cols]
            wsum = ext
            span = 1
            while span < win:
                wsum = wsum + pltpu.roll(wsum, span, 0)
                span *= 2
            cur = ext[POOL_HALO:]
            count = jnp.minimum(pos, win).astype(F32)
            pooled = wsum[POOL_HALO:] / count - cur
            mixed = _dot(pooled.astype(BF16), pw_ref[grp]) * ps_ref[:, cols]
            merged_s[r0:r0 + nr, D_HGRN + grp * POOL_GROUP_DIM:D_HGRN + (grp + 1) * POOL_GROUP_DIM] = (
                mixed.astype(BF16))

    def out_proj(qd, n2):
        rows = slice(qd * quarter, (qd + 1) * quarter)
        cols = slice(n2 * (D_MODEL // 2), (n2 + 1) * (D_MODEL // 2))
        y_s[rows, cols] = _dot(merged_s[rows, :], wout_ref[:, cols])

    def out_ln(qd):
        rows = slice(qd * quarter, (qd + 1) * quarter)
        o_ref[rows, :] = _layernorm(ALPHA * hc_ref[rows, :] + y_s[rows, :], g_ref[...], b_ref[...])

    n_chunks = ts // CHUNK
    n_slots = n_chunks * HGRN_HEADS
    P = functools.partial
    COPY, PROJ, CUM, POOL, OUT, LN = 100, 256, 192, 200, 256, 150
    early = [(0, COPY, P(copy_rows, c)) for c in range(2, n_chunks)] + [(0, 10, pool_halo)]
    blocks = list(range(D_IN_PROJ // col_blk))
    forget_blocks = [j for j in blocks if j * col_blk // D_HGRN == 1]
    first = 4
    early += [(first, COPY, cast_next)]
    early += [(first, 2 * PROJ, P(in_proj, j)) for j in forget_blocks]
    early += [(first, CUM, P(cum_chunks, 2 * i, 2)) for i in range(n_chunks // 2)]
    early += [(first, 2 * PROJ, P(in_proj, j)) for j in blocks if j not in forget_blocks]
    late = []
    for qd in range(4):
        first = (qd + 1) * (n_chunks // 4) * HGRN_HEADS
        late += [(first, POOL, P(pool_rows, qd)), (first, OUT, P(out_proj, qd, 0)),
                 (first, OUT, P(out_proj, qd, 1)), (first, LN, P(out_ln, qd))]
    total = sum(cost for _, cost, _ in early + late)
    copy_rows(0)
    copy_rows(1)
    done = 0
    ch = lambda slot: (slot // HGRN_HEADS, slot % HGRN_HEADS)
    scores, outs = {}, {}
    for slot in range(n_slots + 2):
        if slot < n_slots:
            scores[slot] = scores_stage(*ch(slot))
        if 1 <= slot <= n_slots:
            outs[slot - 1] = state_stage(*ch(slot - 1), scores.pop(slot - 1))
        if slot >= 2:
            norm_stage(*ch(slot - 2), outs.pop(slot - 2))
        while done < total * (slot + 1) / n_slots:
            queue = late if late and late[0][0] <= slot - 1 else early
            if not queue or queue[0][0] > slot - 1:
                break
            _, cost, fn = queue.pop(0)
            fn()
            done += cost
    for _, _, fn in early + late:
        fn()


def _mixer_ln(h2d, batch, seq, w_in, lb, gnorm, pool_w, pool_scale, w_out, g, b):
    ts = MIX_TILE
    assert seq % ts == 0 and ts % CHUNK == 0
    n_seq = seq // ts
    n_tiles = batch * n_seq
    tri, mask = _chunk_constants()
    next_tile = pl.BlockSpec((ts, D_MODEL), lambda i: (jnp.minimum(i, n_tiles - 1), 0))
    done_tile = pl.BlockSpec((ts, D_MODEL), lambda i: (jnp.maximum(i - 1, 0), 0))
    proj_set = [pltpu.VMEM((ts, D_HGRN), F32)] * 6 + [pltpu.VMEM((ts + POOL_HALO, D_POOL), F32)]
    return pl.pallas_call(
        functools.partial(_mixer_ln_kernel, n_seq=n_seq),
        grid=(n_tiles + 1,),
        in_specs=[
            next_tile,
            done_tile,
            _resident(w_in.shape),
            _resident(lb.shape),
            _resident(gnorm.shape),
            _resident(pool_w.shape),
            _resident(pool_scale.shape),
            _resident(w_out.shape),
            _resident(g.shape),
            _resident(b.shape),
            _resident(tri.shape),
            _resident(mask.shape),
        ],
        out_specs=done_tile,
        out_shape=jax.ShapeDtypeStruct(h2d.shape, F32),
        scratch_shapes=proj_set + proj_set + [
            pltpu.VMEM((ts, D_MODEL), BF16),
            pltpu.VMEM((ts, D_HGRN + D_POOL), BF16),
            pltpu.VMEM((ts, D_MODEL), F32),
            pltpu.VMEM((HGRN_HEADS, HEAD_DIM, HEAD_DIM), F32),
        ],
        compiler_params=pltpu.CompilerParams(
            dimension_semantics=("arbitrary",), vmem_limit_bytes=V7X_VMEM_LIMIT),
        name="mixer_ln",
    )(h2d, h2d, w_in, lb, gnorm, pool_w, pool_scale, w_out, g, b, tri, mask)


def _kv_proj_kernel(*refs, n_side):
    mem_ref, wk_ref, wv_ref = refs[:3]
    side_in = refs[3:3 + n_side]
    k_ref, v_ref = refs[3 + n_side:5 + n_side]
    side_out = refs[5 + n_side:5 + 2 * n_side]
    wk_b, wv_b = refs[-2:]

    @pl.when(pl.program_id(0) == 0)
    def _():
        wk_b[...] = wk_ref[...].astype(BF16)
        wv_b[...] = wv_ref[...].astype(BF16)

    mb = mem_ref[...].astype(BF16)
    k_ref[...] = _dot(mb, wk_b[...]).astype(BF16)
    v_ref[...] = _dot(mb, wv_b[...]).astype(BF16)
    for src, dst in zip(side_in, side_out):
        dst[...] = src[...].astype(BF16)


def _kv_proj(mem2d, n_mem, wk, wv, side=(), layer=0):
    rows = mem2d.shape[0]
    n_steps = rows // n_mem
    tile = pl.BlockSpec((n_mem, D_MODEL), lambda i: (i, 0))
    weight = pl.BlockSpec((None, D_MODEL, D_MODEL), lambda i: (layer, 0, 0), pipeline_mode=pl.Buffered(1))
    blocks = [_side_blocks(w, layer, n_steps) for w in side]
    out = pl.pallas_call(
        functools.partial(_kv_proj_kernel, n_side=len(side)),
        grid=(n_steps,),
        in_specs=[tile, weight, weight] + [blk[0] for blk in blocks],
        out_specs=[tile, tile] + [blk[1] for blk in blocks],
        out_shape=[jax.ShapeDtypeStruct((rows, D_MODEL), BF16)] * 2 + [blk[2] for blk in blocks],
        scratch_shapes=[pltpu.VMEM((D_MODEL, D_MODEL), BF16)] * 2,
        compiler_params=pltpu.CompilerParams(
            dimension_semantics=("arbitrary",), vmem_limit_bytes=V7X_VMEM_LIMIT),
        name="kv_proj",
    )(mem2d, wk, wv, *side)
    return out[0], out[1], out[2:]


def _xattn_ln_kernel(h_ref, k_ref, v_ref, wq_ref, wo_ref, g_ref, b_ref, o_ref, y_s, *, n_tiles):
    step = pl.program_id(0)
    ts = h_ref.shape[0]
    ln_rows = ts // XA_LN_PIECES

    def ln_piece(p):
        rows = slice(p * ln_rows, (p + 1) * ln_rows)
        out = _layernorm(y_s[rows, :], g_ref[...], b_ref[...])
        o_ref[rows, :] = out
        return out

    @pl.when(step == 0)
    def _():
        y_s[...] = jnp.zeros_like(y_s)

    @pl.when(step < n_tiles)
    def _():
        h = h_ref[...]
        hb = h.astype(BF16)
        q = _dot(hb, wq_ref[...])
        pieces = list(range(XA_LN_PIECES))
        heads = []
        for hd in range(XA_HEADS):
            cols = slice(hd * XA_HEAD_DIM, (hd + 1) * XA_HEAD_DIM)
            qh = q[:, cols].astype(BF16)
            if pieces:
                take = min(len(pieces), XA_LN_FIRST if hd == 0 else 1)
                zeros = [_zero_after(ln_piece(pieces.pop(0))) for _ in range(take)]
                zero = sum(zeros[1:], zeros[0]).astype(BF16)
                qh = qh + jnp.tile(zero, (ts // V7X_SUBLANES, XA_HEAD_DIM // V7X_LANES))
            s = _dot_nt(qh, k_ref[:, cols]) * (XA_HEAD_DIM ** -0.5)
            s = s - jnp.max(s, axis=-1, keepdims=True)
            p = jnp.exp(s)
            p = p / jnp.sum(p, axis=-1, keepdims=True)
            heads.append(_dot(p.astype(BF16), v_ref[:, cols]).astype(BF16))
        assert not pieces
        y_s[...] = ALPHA * h + _dot(jnp.concatenate(heads, axis=-1), wo_ref[...])

    @pl.when(step == n_tiles)
    def _():
        for p in range(XA_LN_PIECES):
            ln_piece(p)


def _xattn_ln(h2d, batch, seq, k2d, v2d, n_mem, wq, wo, g, b):
    ts = XA_TILE
    assert seq % ts == 0
    n_seq = seq // ts
    n_tiles = batch * n_seq
    attended = lambda i: jnp.minimum(i, n_tiles - 1)
    tile = pl.BlockSpec((ts, D_MODEL), lambda i: (attended(i), 0))
    mem_tile = pl.BlockSpec((n_mem, D_MODEL), lambda i: (attended(i) // n_seq, 0))
    return pl.pallas_call(
        functools.partial(_xattn_ln_kernel, n_tiles=n_tiles),
        grid=(n_tiles + 1,),
        in_specs=[tile, mem_tile, mem_tile, _resident(wq.shape), _resident(wo.shape),
                  _resident(g.shape), _resident(b.shape)],
        out_specs=pl.BlockSpec((ts, D_MODEL), lambda i: (jnp.maximum(i - 1, 0), 0)),
        out_shape=jax.ShapeDtypeStruct(h2d.shape, F32),
        scratch_shapes=[pltpu.VMEM((ts, D_MODEL), F32)],
        compiler_params=pltpu.CompilerParams(
            dimension_semantics=("arbitrary",), vmem_limit_bytes=V7X_VMEM_LIMIT),
        name="xattn_ln",
    )(h2d, k2d, v2d, wq, wo, g, b)


def kernel(x, mem, w_ffn1_in, w_ffn1_out, ln1_g, ln1_b, w_mix_in, hgrn_lb, hgrn_gnorm, pool_w, pool_scale, w_mix_out, ln2_g, ln2_b, xa_wq, xa_wk, xa_wv, xa_wo, ln3_g, ln3_b, w_ffn2_in, w_ffn2_out, ln4_g, ln4_b):
    batch, seq, _ = x.shape
    n_mem = mem.shape[1]
    assert w_ffn1_in.shape[0] == DEPTH == 1
    h = x.reshape(batch * seq, D_MODEL)
    mem2d = mem.reshape(batch * n_mem, D_MODEL)
    for l in range(DEPTH):
        k2d, v2d, (ffn1_in, ffn1_out) = _kv_proj(mem2d, n_mem, xa_wk, xa_wv, side=(w_ffn1_in, w_ffn1_out), layer=l)
        later = (w_mix_in, pool_w.reshape(DEPTH, D_POOL, POOL_GROUP_DIM), w_mix_out, xa_wq, xa_wo,
                 w_ffn2_in, w_ffn2_out)
        h, later = _ffn_ln(h, ffn1_in, ffn1_out, ln1_g[l:l + 1], ln1_b[l:l + 1], side=later, layer=l)
        mix_in, pool, mix_out, wq, wo, ffn2_in, ffn2_out = later
        pool = pool.reshape(len(POOL_WINDOWS), POOL_GROUP_DIM, POOL_GROUP_DIM)
        h = _mixer_ln(h, batch, seq, mix_in, hgrn_lb, hgrn_gnorm[l:l + 1], pool,
                      pool_scale[l:l + 1], mix_out, ln2_g[l:l + 1], ln2_b[l:l + 1])
        h = _xattn_ln(h, batch, seq, k2d, v2d, n_mem, wq, wo, ln3_g[l:l + 1], ln3_b[l:l + 1])
        h, _ = _ffn_ln(h, ffn2_in, ffn2_out, ln4_g[l:l + 1], ln4_b[l:l + 1])
    return h.reshape(batch, seq, D_MODEL)
```

```python
import functools

import numpy as np
import jax
import jax.numpy as jnp
from jax import lax
from jax.experimental import pallas as pl
from jax.experimental.pallas import tpu as pltpu

F32 = jnp.float32
BF16 = jnp.bfloat16

D_MODEL = 1024
DEPTH = 1
D_HGRN = 512
D_POOL = 512
HGRN_HEADS = 4
HEAD_DIM = 128
POOL_WINDOWS = (2, 4, 8, 16)
POOL_GROUP_DIM = 128
D_FF = 2816
D_IN_PROJ = 4 * D_HGRN + D_POOL
XA_HEADS = 4
XA_HEAD_DIM = 256
ALPHA = (2.0 * DEPTH) ** 0.25
LN_EPS = 1e-5
RMS_EPS = 1e-6

V7X_SUBLANES = 8
V7X_LANES = 128
V7X_MXU_DIM = 256
V7X_VMEM_LIMIT = 56 * 1024 * 1024

CHUNK = 64
SUB = V7X_SUBLANES
POOL_HALO = 16
FFN_TILE = 512
FFN_LN_PIECES = 4
MIX_TILE = 512
XA_TILE = 512
XA_LN_PIECES = 4
XA_LN_FIRST = 2
FF_CHUNKS = tuple((c0, min(512, D_FF - c0)) for c0 in range(0, D_FF, 512))


def _dot(a, b):
    return jnp.dot(a, b, preferred_element_type=F32)


def _dot_nt(a, b):
    return lax.dot_general(a, b, (((1,), (1,)), ((), ())), preferred_element_type=F32)


def _dot_tn(a, b):
    return lax.dot_general(a, b, (((0,), (0,)), ((), ())), preferred_element_type=F32)


def _silu(x):
    return x * jax.nn.sigmoid(x)


def _layernorm(y, g, b):
    mu = jnp.mean(y, axis=-1, keepdims=True)
    d = y - mu
    var = jnp.mean(d * d, axis=-1, keepdims=True)
    return d * lax.rsqrt(var + LN_EPS) * g + b


def _zero_after(v):
    sub = V7X_SUBLANES * (4 // v.dtype.itemsize)
    r, c = v.shape
    m = jnp.max(v.reshape(r // sub, sub, c), axis=0)
    m = functools.reduce(jnp.maximum, [m[:, j * V7X_LANES:(j + 1) * V7X_LANES] for j in range(c // V7X_LANES)])
    m = m.astype(F32)
    if sub > V7X_SUBLANES:
        m = jnp.maximum(m[:V7X_SUBLANES], m[V7X_SUBLANES:])
    bits = pltpu.bitcast(m, jnp.uint32)
    return pltpu.bitcast((bits >> 16) >> 16, F32)


def _resident(shape):
    zeros = (0,) * len(shape)
    return pl.BlockSpec(shape, lambda *_: zeros, pipeline_mode=pl.Buffered(1))


def _ffn_ln_kernel(*refs, n_tiles, n_side):
    x_ref, win_ref, wout_ref, g_ref, b_ref = refs[:5]
    side_in = refs[5:5 + n_side]
    o_ref = refs[5 + n_side]
    side_out = refs[6 + n_side:6 + 2 * n_side]
    y_s = refs[-1]
    step = pl.program_id(0)
    ln_rows = FFN_TILE // FFN_LN_PIECES
    gaps = len(FF_CHUNKS) - 1

    def ln_piece(p):
        rows = slice(p * ln_rows, (p + 1) * ln_rows)
        out = _layernorm(y_s[rows, :], g_ref[...], b_ref[...])
        o_ref[rows, :] = out
        return out

    def cast_side(n):
        v = side_in[n][...].astype(BF16)
        side_out[n][...] = v
        return v

    @pl.when(step == 0)
    def _():
        y_s[...] = jnp.zeros_like(y_s)

    @pl.when(step < n_tiles)
    def _():
        x = x_ref[...]
        xb = x.astype(BF16)
        acc = None
        jobs = [(FFN_TILE * D_MODEL // FFN_LN_PIECES * 3, functools.partial(ln_piece, p)) for p in range(FFN_LN_PIECES)]
        jobs += [(side_in[n].shape[0] * side_in[n].shape[1], functools.partial(cast_side, n)) for n in range(n_side)]
        bins = [[0, []] for _ in range(gaps)]
        for cost, job in sorted(jobs, key=lambda cj: -cj[0]):
            target = min(bins, key=lambda bn: bn[0])
            target[0] += cost
            target[1].append(job)
        lhs = xb
        for n, (c0, cw) in enumerate(FF_CHUNKS):
            gate = _dot(lhs, win_ref[:, c0:c0 + cw])
            up = _dot(lhs, win_ref[:, D_FF + c0:D_FF + c0 + cw])
            act = (_silu(gate) * up).astype(BF16)
            part = _dot(act, wout_ref[c0:c0 + cw, :])
            acc = part if acc is None else acc + part
            if n < gaps and bins[n][1]:
                zeros = [_zero_after(job()) for job in bins[n][1]]
                zero = sum(zeros[1:], zeros[0]).astype(BF16)
                lhs = xb + jnp.tile(zero, (FFN_TILE // V7X_SUBLANES, D_MODEL // V7X_LANES))
        y_s[...] = ALPHA * x + 0.5 * acc

    @pl.when(step == n_tiles)
    def _():
        for p in range(FFN_LN_PIECES):
            ln_piece(p)


def _side_blocks(w, layer, n_steps):
    _, r, c = w.shape
    packed_rows = 2 * V7X_SUBLANES
    rows = next(n for n in range(packed_rows, r + 1, packed_rows) if r % n == 0 and r // n <= n_steps)
    index = lambda i: (jnp.minimum(i, r // rows - 1), 0)
    return (pl.BlockSpec((None, rows, c), lambda i: (layer,) + index(i)), pl.BlockSpec((rows, c), index),
            jax.ShapeDtypeStruct((r, c), BF16))


def _ffn_ln(x2d, w_in, w_out, g, b, side=(), layer=0):
    m = x2d.shape[0]
    assert m % FFN_TILE == 0
    n_tiles = m // FFN_TILE
    current = lambda i: (jnp.minimum(i, n_tiles - 1), 0)
    blocks = [_side_blocks(w, layer, n_tiles) for w in side]
    out = pl.pallas_call(
        functools.partial(_ffn_ln_kernel, n_tiles=n_tiles, n_side=len(side)),
        grid=(n_tiles + 1,),
        in_specs=[
            pl.BlockSpec((FFN_TILE, D_MODEL), current),
            _resident((D_MODEL, 2 * D_FF)),
            _resident((D_FF, D_MODEL)),
            _resident((1, D_MODEL)),
            _resident((1, D_MODEL)),
        ] + [blk[0] for blk in blocks],
        out_specs=[pl.BlockSpec((FFN_TILE, D_MODEL), lambda i: (jnp.maximum(i - 1, 0), 0))]
        + [blk[1] for blk in blocks],
        out_shape=[jax.ShapeDtypeStruct((m, D_MODEL), F32)] + [blk[2] for blk in blocks],
        scratch_shapes=[pltpu.VMEM((FFN_TILE, D_MODEL), F32)],
        compiler_params=pltpu.CompilerParams(
            dimension_semantics=("arbitrary",), vmem_limit_bytes=V7X_VMEM_LIMIT),
        name="ffn_ln",
    )(x2d, w_in, w_out, g, b, *side)
    return out[0], out[1:]


N_BLK = CHUNK // SUB
CUM_GROUP = 4
FAR_ORDER = (7, 6, 3, 5, 4, 2, 1)
FAR_COLS = 256
NEAR_LEVELS = (0, 1, 2, 4)
PV_COLS = FAR_COLS + len(NEAR_LEVELS) * CHUNK
assert sorted(FAR_ORDER) == list(range(1, N_BLK)) and SUB * sum(FAR_ORDER) <= FAR_COLS


def _chunk_constants():
    t = np.arange(CHUNK)[:, None]
    s = np.arange(CHUNK)[None, :]
    tri = np.kron(np.eye(CUM_GROUP), (s <= t)).astype(np.float32)
    mask = np.zeros((CHUNK, PV_COLS), np.float32)
    off = 0
    for i in FAR_ORDER:
        mask[:, off:off + SUB * i] = (t // SUB == i)
        off += SUB * i
    for n, b in enumerate(NEAR_LEVELS):
        valid = (t == s) if b == 0 else (((t // b) % 2 == 1) & ((s // b) == (t // b) - 1))
        mask[:, FAR_COLS + n * CHUNK:FAR_COLS + (n + 1) * CHUNK] = valid
    return jnp.asarray(tri, BF16), jnp.asarray(mask, BF16)


def _split3(x):
    hi = x.astype(BF16)
    r = x - hi.astype(F32)
    mid = r.astype(BF16)
    lo = (r - mid.astype(F32)).astype(BF16)
    return hi, mid, lo


def _mixer_ln_kernel(hn_ref, hc_ref, win_ref, lb_ref, gn_ref, pw_ref, ps_ref, wout_ref, g_ref, b_ref,
                     tri_ref, mask_ref, o_ref,
                     q_a, k_a, f_a, i_a, cum_a, gate_a, vb_a,
                     q_s, k_s, f_s, i_s, cum_s, gate_s, vb_s, hb_s, merged_s, y_s, state_s, *, n_seq):
    ts = hn_ref.shape[0]
    step = pl.program_id(0)
    seq_tile_next = step % n_seq
    seq_tile = (step + n_seq - 1) % n_seq
    handoff = ((q_a, q_s), (k_a, k_s), (f_a, f_s), (i_a, i_s), (cum_a, cum_s), (gate_a, gate_s), (vb_a, vb_s))

    @pl.when(step == 0)
    def _():
        state_s[...] = jnp.zeros_like(state_s)
        for src, _ in handoff:
            src[...] = jnp.zeros_like(src)

    col_blk = V7X_MXU_DIM

    def copy_rows(c):
        rows = slice(c * CHUNK, (c + 1) * CHUNK)
        for src, dst in handoff[:-1]:
            dst[rows, :] = src[rows, :]
        if c == 0:
            vb_s[0:POOL_HALO, :] = vb_a[0:POOL_HALO, :]
        prow = slice(POOL_HALO + c * CHUNK, POOL_HALO + (c + 1) * CHUNK)
        vb_s[prow, :] = vb_a[prow, :]

    def pool_halo():
        vb_a[0:POOL_HALO, :] = jnp.where(seq_tile_next == 0, 0.0, vb_s[ts:ts + POOL_HALO, :])

    a = lb_ref[...]
    e = jnp.exp(a - jnp.max(a, axis=0, keepdims=True))
    lower = e[0:1, :] / jnp.sum(e, axis=0, keepdims=True)

    def cast_next():
        hb_s[...] = hn_ref[...].astype(BF16)

    def in_proj(j):
        x = _dot(hb_s[...], win_ref[:, j * col_blk:(j + 1) * col_blk])
        kind, cb = divmod(j * col_blk, D_HGRN)
        cols = slice(cb, cb + col_blk)
        if kind == 0:
            q_a[:, cols] = _silu(x)
        elif kind == 1:
            forget = lower[:, cols] + (1.0 - lower[:, cols]) * jax.nn.sigmoid(x)
            f_a[:, cols] = forget
            k_a[:, cols] = 1.0 - forget
            cum_a[:, cols] = jnp.log(forget)
        elif kind == 2:
            i_a[:, cols] = x
        elif kind == 3:
            gate_a[:, cols] = _silu(x)
        else:
            vb_a[POOL_HALO:POOL_HALO + ts, cols] = x

    def cum_chunks(g):
        rows = slice(g * CUM_GROUP * CHUNK, (g + 1) * CUM_GROUP * CHUNK)
        tri = tri_ref[...]
        hi, mid, lo = _split3(cum_a[rows, :])
        cum_a[rows, :] = _dot(tri, hi) + _dot(tri, mid) + _dot(tri, lo)

    sub_row = lax.broadcasted_iota(jnp.int32, (SUB, HEAD_DIM), 0)
    zero_blk = jnp.zeros((SUB, HEAD_DIM), F32)
    far_pad = jnp.zeros((FAR_COLS - SUB * sum(FAR_ORDER), HEAD_DIM), F32)
    gn = gn_ref[...]
    zero_kb = jnp.zeros((CHUNK, HEAD_DIM), BF16)

    def block_diag(a, b):
        return jnp.concatenate([jnp.concatenate([a, zero_kb], axis=1),
                                jnp.concatenate([zero_kb, b], axis=1)], axis=0)

    def head_cols(hd):
        return slice(hd * HEAD_DIM, (hd + 1) * HEAD_DIM)

    def scores_stage(c, hd):
        r0 = c * CHUNK
        rows = pl.ds(r0, CHUNK)
        cols = head_cols(hd)
        q = q_s[rows, cols]
        k = k_s[rows, cols]
        f = f_s[rows, cols]
        v = i_s[rows, cols]
        cum = cum_s[rows, cols]

        def cum_row(row):
            return jnp.broadcast_to(cum_s[pl.ds(r0 + row, 1), cols], (SUB, HEAD_DIM))

        bound = [zero_blk] + [cum_row(SUB * j - 1) for j in range(1, N_BLK + 1)]

        q_far = (q * jnp.exp(cum - jnp.concatenate(bound[:N_BLK], axis=0))).astype(BF16)
        k_parts, v_parts = [], []
        for i in FAR_ORDER:
            n = SUB * i
            k_parts.append(k[:n] * jnp.exp(jnp.concatenate([bound[i]] * i, axis=0) - cum[:n]))
            v_parts.append(v[:n])
        k_far = jnp.concatenate(k_parts + [far_pad], axis=0).astype(BF16)
        s_far = _dot_nt(q_far, k_far)

        ref_q2, ref_k2, ref_q4, ref_k4 = [], [], [], []
        for j in range(N_BLK):
            lo_b, hi_b = bound[j], bound[j + 1]
            m1, m3, m5 = (cum_row(SUB * j + r) for r in (1, 3, 5))
            ref_q4.append(jnp.where(sub_row < 4, lo_b, m3))
            ref_k4.append(jnp.where(sub_row < 4, m3, hi_b))
            ref_q2.append(jnp.where(sub_row < 2, lo_b, jnp.where(sub_row < 4, m1, jnp.where(sub_row < 6, m3, m5))))
            ref_k2.append(jnp.where(sub_row < 2, m1, jnp.where(sub_row < 4, m3, jnp.where(sub_row < 6, m5, hi_b))))
        cat = lambda parts: jnp.concatenate(parts, axis=0)
        q2 = q * jnp.exp(cum - cat(ref_q2))
        k2 = k * jnp.exp(cat(ref_k2) - cum)
        q4 = q * jnp.exp(cum - cat(ref_q4))
        k4 = k * jnp.exp(cat(ref_k4) - cum)
        kb = k.astype(BF16)
        s_a = _dot_nt(jnp.concatenate([q, q * f], axis=1).astype(BF16), block_diag(kb, kb))
        s_b = _dot_nt(jnp.concatenate([q2, q4], axis=1).astype(BF16),
                      block_diag(k2.astype(BF16), k4.astype(BF16)))

        total = bound[N_BLK]
        return dict(
            p=jnp.concatenate([s_far, s_a, s_b], axis=1).astype(BF16) * mask_ref[...],
            v_all=jnp.concatenate(v_parts + [far_pad] + [v] * len(NEAR_LEVELS), axis=0).astype(BF16),
            q_dec=(q * jnp.exp(cum)).astype(BF16),
            k_dec=(k * jnp.exp(jnp.concatenate([total] * N_BLK, axis=0) - cum)).astype(BF16),
            vb=v.astype(BF16),
            decay=jnp.exp(total[0:1, :]))

    def state_stage(c, hd, sc):
        st = state_s[hd]
        if c == 0:
            st = jnp.where(seq_tile == 0, 0.0, st)
        o = _dot(sc["p"], sc["v_all"]) + _dot_nt(sc["q_dec"], st.astype(BF16))
        state_s[hd] = st * sc["decay"] + _dot_tn(sc["vb"], sc["k_dec"])
        return o

    def norm_stage(c, hd, o):
        rows = pl.ds(c * CHUNK, CHUNK)
        cols = head_cols(hd)
        o = o * lax.rsqrt(jnp.mean(o * o, axis=-1, keepdims=True) + RMS_EPS) * gn
        merged_s[rows, cols] = (o * gate_s[rows, cols]).astype(BF16)

    quarter = ts // 4

    def pool_rows(r0, nr):
        pos = seq_tile * ts + r0 + lax.broadcasted_iota(jnp.int32, (nr, 1), 0) + 1
        for grp, win in enumerate(POOL_WINDOWS):
            cols = slice(grp * POOL_GROUP_DIM, (grp + 1) * POOL_GROUP_DIM)
            ext = vb_s[r0:r0 + POOL_HALO + nr, cols]
            wsum = ext
            span = 1
            while span < win:
                wsum = wsum + pltpu.roll(wsum, span, 0)
                span *= 2
            cur = ext[POOL_HALO:]
            count = jnp.minimum(pos, win).astype(F32)
            pooled = wsum[POOL_HALO:] / count - cur
            mixed = _dot(pooled.astype(BF16), pw_ref[grp]) * ps_ref[:, cols]
            merged_s[r0:r0 + nr, D_HGRN + grp * POOL_GROUP_DIM:D_HGRN + (grp + 1) * POOL_GROUP_DIM] = (
                mixed.astype(BF16))

    def out_proj(r0, nr, n2):
        rows = slice(r0, r0 + nr)
        cols = slice(n2 * (D_MODEL // 2), (n2 + 1) * (D_MODEL // 2))
        y_s[rows, cols] = _dot(merged_s[rows, :], wout_ref[:, cols])

    def out_ln(qd):
        rows = slice(qd * quarter, (qd + 1) * quarter)
        o_ref[rows, :] = _layernorm(ALPHA * hc_ref[rows, :] + y_s[rows, :], g_ref[...], b_ref[...])

    n_chunks = ts // CHUNK
    n_slots = n_chunks * HGRN_HEADS
    P = functools.partial
    COPY, PROJ, CUM, POOL, OUT, LN = 100, 256, 192, 200, 256, 150
    early = [(0, COPY, P(copy_rows, c)) for c in range(2, n_chunks)] + [(0, 10, pool_halo)]
    blocks = list(range(D_IN_PROJ // col_blk))
    forget_blocks = [j for j in blocks if j * col_blk // D_HGRN == 1]
    first = 4
    early += [(first, COPY, cast_next)]
    early += [(first, 2 * PROJ, P(in_proj, j)) for j in forget_blocks]
    early += [(first, 2 * CUM, P(cum_chunks, g)) for g in range(n_chunks // CUM_GROUP)]
    early += [(first, 2 * PROJ, P(in_proj, j)) for j in blocks if j not in forget_blocks]
    late = []
    for r0, nr in ((0, 3 * quarter), (3 * quarter, quarter)):
        first = (r0 + nr) // CHUNK * HGRN_HEADS
        late += [(first, POOL * nr // quarter, P(pool_rows, r0, nr))]
        late += [(first, OUT * nr // quarter, P(out_proj, r0, nr, n2)) for n2 in range(2)]
        late += [(first, LN, P(out_ln, qd)) for qd in range(r0 // quarter, (r0 + nr) // quarter)]
    total = sum(cost for _, cost, _ in early + late)
    copy_rows(0)
    copy_rows(1)
    done = 0
    ch = lambda slot: (slot // HGRN_HEADS, slot % HGRN_HEADS)
    scores, outs = {}, {}
    for slot in range(n_slots + 2):
        if slot < n_slots:
            scores[slot] = scores_stage(*ch(slot))
        if 1 <= slot <= n_slots:
            outs[slot - 1] = state_stage(*ch(slot - 1), scores.pop(slot - 1))
        if slot >= 2:
            norm_stage(*ch(slot - 2), outs.pop(slot - 2))
        while done < total * (slot + 1) / n_slots:
            queue = late if late and late[0][0] <= slot - 1 else early
            if not queue or queue[0][0] > slot - 1:
                break
            _, cost, fn = queue.pop(0)
            fn()
            done += cost
    for _, _, fn in early + late:
        fn()


def _mixer_ln(h2d, batch, seq, w_in, lb, gnorm, pool_w, pool_scale, w_out, g, b):
    ts = MIX_TILE
    assert seq % ts == 0 and ts % CHUNK == 0
    n_seq = seq // ts
    n_tiles = batch * n_seq
    tri, mask = _chunk_constants()
    next_tile = pl.BlockSpec((ts, D_MODEL), lambda i: (jnp.minimum(i, n_tiles - 1), 0))
    done_tile = pl.BlockSpec((ts, D_MODEL), lambda i: (jnp.maximum(i - 1, 0), 0))
    proj_set = [pltpu.VMEM((ts, D_HGRN), F32)] * 6 + [pltpu.VMEM((ts + POOL_HALO, D_POOL), F32)]
    return pl.pallas_call(
        functools.partial(_mixer_ln_kernel, n_seq=n_seq),
        grid=(n_tiles + 1,),
        in_specs=[
            next_tile,
            done_tile,
            _resident(w_in.shape),
            _resident(lb.shape),
            _resident(gnorm.shape),
            _resident(pool_w.shape),
            _resident(pool_scale.shape),
            _resident(w_out.shape),
            _resident(g.shape),
            _resident(b.shape),
            _resident(tri.shape),
            _resident(mask.shape),
        ],
        out_specs=done_tile,
        out_shape=jax.ShapeDtypeStruct(h2d.shape, F32),
        scratch_shapes=proj_set + proj_set + [
            pltpu.VMEM((ts, D_MODEL), BF16),
            pltpu.VMEM((ts, D_HGRN + D_POOL), BF16),
            pltpu.VMEM((ts, D_MODEL), F32),
            pltpu.VMEM((HGRN_HEADS, HEAD_DIM, HEAD_DIM), F32),
        ],
        compiler_params=pltpu.CompilerParams(
            dimension_semantics=("arbitrary",), vmem_limit_bytes=V7X_VMEM_LIMIT),
        name="mixer_ln",
    )(h2d, h2d, w_in, lb, gnorm, pool_w, pool_scale, w_out, g, b, tri, mask)


def _kv_proj_kernel(*refs, n_side):
    mem_ref, wk_ref, wv_ref = refs[:3]
    side_in = refs[3:3 + n_side]
    k_ref, v_ref = refs[3 + n_side:5 + n_side]
    side_out = refs[5 + n_side:5 + 2 * n_side]
    wk_b, wv_b = refs[-2:]

    @pl.when(pl.program_id(0) == 0)
    def _():
        wk_b[...] = wk_ref[...].astype(BF16)
        wv_b[...] = wv_ref[...].astype(BF16)

    mb = mem_ref[...].astype(BF16)
    k_ref[...] = _dot(mb, wk_b[...]).astype(BF16)
    v_ref[...] = _dot(mb, wv_b[...]).astype(BF16)
    for src, dst in zip(side_in, side_out):
        dst[...] = src[...].astype(BF16)


def _kv_proj(mem2d, n_mem, wk, wv, side=(), layer=0):
    rows = mem2d.shape[0]
    n_steps = rows // n_mem
    tile = pl.BlockSpec((n_mem, D_MODEL), lambda i: (i, 0))
    weight = pl.BlockSpec((None, D_MODEL, D_MODEL), lambda i: (layer, 0, 0), pipeline_mode=pl.Buffered(1))
    blocks = [_side_blocks(w, layer, n_steps) for w in side]
    out = pl.pallas_call(
        functools.partial(_kv_proj_kernel, n_side=len(side)),
        grid=(n_steps,),
        in_specs=[tile, weight, weight] + [blk[0] for blk in blocks],
        out_specs=[tile, tile] + [blk[1] for blk in blocks],
        out_shape=[jax.ShapeDtypeStruct((rows, D_MODEL), BF16)] * 2 + [blk[2] for blk in blocks],
        scratch_shapes=[pltpu.VMEM((D_MODEL, D_MODEL), BF16)] * 2,
        compiler_params=pltpu.CompilerParams(
            dimension_semantics=("arbitrary",), vmem_limit_bytes=V7X_VMEM_LIMIT),
        name="kv_proj",
    )(mem2d, wk, wv, *side)
    return out[0], out[1], out[2:]


def _xattn_ln_kernel(h_ref, k_ref, v_ref, wq_ref, wo_ref, g_ref, b_ref, o_ref, y_s, *, n_tiles):
    step = pl.program_id(0)
    ts = h_ref.shape[0]
    ln_rows = ts // XA_LN_PIECES

    def ln_piece(p):
        rows = slice(p * ln_rows, (p + 1) * ln_rows)
        out = _layernorm(y_s[rows, :], g_ref[...], b_ref[...])
        o_ref[rows, :] = out
        return out

    @pl.when(step == 0)
    def _():
        y_s[...] = jnp.zeros_like(y_s)

    @pl.when(step < n_tiles)
    def _():
        h = h_ref[...]
        hb = h.astype(BF16)
        q = _dot(hb, wq_ref[...])
        pieces = list(range(XA_LN_PIECES))
        heads = []
        for hd in range(XA_HEADS):
            cols = slice(hd * XA_HEAD_DIM, (hd + 1) * XA_HEAD_DIM)
            qh = q[:, cols].astype(BF16)
            if pieces:
                take = min(len(pieces), XA_LN_FIRST if hd == 0 else 1)
                zeros = [_zero_after(ln_piece(pieces.pop(0))) for _ in range(take)]
                zero = sum(zeros[1:], zeros[0]).astype(BF16)
                qh = qh + jnp.tile(zero, (ts // V7X_SUBLANES, XA_HEAD_DIM // V7X_LANES))
            s = _dot_nt(qh, k_ref[:, cols]) * (XA_HEAD_DIM ** -0.5)
            s = s - jnp.max(s, axis=-1, keepdims=True)
            p = jnp.exp(s)
            p = p / jnp.sum(p, axis=-1, keepdims=True)
            heads.append(_dot(p.astype(BF16), v_ref[:, cols]).astype(BF16))
        assert not pieces
        y_s[...] = ALPHA * h + _dot(jnp.concatenate(heads, axis=-1), wo_ref[...])

    @pl.when(step == n_tiles)
    def _():
        for p in range(XA_LN_PIECES):
            ln_piece(p)


def _xattn_ln(h2d, batch, seq, k2d, v2d, n_mem, wq, wo, g, b):
    ts = XA_TILE
    assert seq % ts == 0
    n_seq = seq // ts
    n_tiles = batch * n_seq
    attended = lambda i: jnp.minimum(i, n_tiles - 1)
    tile = pl.BlockSpec((ts, D_MODEL), lambda i: (attended(i), 0))
    mem_tile = pl.BlockSpec((n_mem, D_MODEL), lambda i: (attended(i) // n_seq, 0))
    return pl.pallas_call(
        functools.partial(_xattn_ln_kernel, n_tiles=n_tiles),
        grid=(n_tiles + 1,),
        in_specs=[tile, mem_tile, mem_tile, _resident(wq.shape), _resident(wo.shape),
                  _resident(g.shape), _resident(b.shape)],
        out_specs=pl.BlockSpec((ts, D_MODEL), lambda i: (jnp.maximum(i - 1, 0), 0)),
        out_shape=jax.ShapeDtypeStruct(h2d.shape, F32),
        scratch_shapes=[pltpu.VMEM((ts, D_MODEL), F32)],
        compiler_params=pltpu.CompilerParams(
            dimension_semantics=("arbitrary",), vmem_limit_bytes=V7X_VMEM_LIMIT),
        name="xattn_ln",
    )(h2d, k2d, v2d, wq, wo, g, b)


def kernel(x, mem, w_ffn1_in, w_ffn1_out, ln1_g, ln1_b, w_mix_in, hgrn_lb, hgrn_gnorm, pool_w, pool_scale, w_mix_out, ln2_g, ln2_b, xa_wq, xa_wk, xa_wv, xa_wo, ln3_g, ln3_b, w_ffn2_in, w_ffn2_out, ln4_g, ln4_b):
    batch, seq, _ = x.shape
    n_mem = mem.shape[1]
    assert w_ffn1_in.shape[0] == DEPTH == 1
    h = x.reshape(batch * seq, D_MODEL)
    mem2d = mem.reshape(batch * n_mem, D_MODEL)
    for l in range(DEPTH):
        k2d, v2d, (ffn1_in, ffn1_out) = _kv_proj(mem2d, n_mem, xa_wk, xa_wv, side=(w_ffn1_in, w_ffn1_out), layer=l)
        later = (w_mix_in, pool_w.reshape(DEPTH, D_POOL, POOL_GROUP_DIM), w_mix_out, xa_wq, xa_wo,
                 w_ffn2_in, w_ffn2_out)
        h, later = _ffn_ln(h, ffn1_in, ffn1_out, ln1_g[l:l + 1], ln1_b[l:l + 1], side=later, layer=l)
        mix_in, pool, mix_out, wq, wo, ffn2_in, ffn2_out = later
        pool = pool.reshape(len(POOL_WINDOWS), POOL_GROUP_DIM, POOL_GROUP_DIM)
        h = _mixer_ln(h, batch, seq, mix_in, hgrn_lb, hgrn_gnorm[l:l + 1], pool,
                      pool_scale[l:l + 1], mix_out, ln2_g[l:l + 1], ln2_b[l:l + 1])
        h = _xattn_ln(h, batch, seq, k2d, v2d, n_mem, wq, wo, ln3_g[l:l + 1], ln3_b[l:l + 1])
        h, _ = _ffn_ln(h, ffn2_in, ffn2_out, ln4_g[l:l + 1], ln4_b[l:l + 1])
    return h.reshape(batch, seq, D_MODEL)
```

```python
import functools

import numpy as np
import jax
import jax.numpy as jnp
from jax import lax
from jax.experimental import pallas as pl
from jax.experimental.pallas import tpu as pltpu

F32 = jnp.float32
BF16 = jnp.bfloat16

D_MODEL = 1024
DEPTH = 1
D_HGRN = 512
D_POOL = 512
HGRN_HEADS = 4
HEAD_DIM = 128
POOL_WINDOWS = (2, 4, 8, 16)
POOL_GROUP_DIM = 128
D_FF = 2816
D_IN_PROJ = 4 * D_HGRN + D_POOL
XA_HEADS = 4
XA_HEAD_DIM = 256
ALPHA = (2.0 * DEPTH) ** 0.25
LN_EPS = 1e-5
RMS_EPS = 1e-6

V7X_SUBLANES = 8
V7X_LANES = 128
V7X_MXU_DIM = 256
V7X_VMEM_LIMIT = 56 * 1024 * 1024

CHUNK = 64
SUB = V7X_SUBLANES
POOL_HALO = 16
FFN_TILE = 512
FFN_LN_PIECES = 4
MIX_TILE = 512
XA_TILE = 512
XA_LN_PIECES = 4
XA_LN_FIRST = 2
FF_CHUNKS = tuple((c0, min(512, D_FF - c0)) for c0 in range(0, D_FF, 512))


def _dot(a, b):
    return jnp.dot(a, b, preferred_element_type=F32)


def _dot_nt(a, b):
    return lax.dot_general(a, b, (((1,), (1,)), ((), ())), preferred_element_type=F32)


def _dot_tn(a, b):
    return lax.dot_general(a, b, (((0,), (0,)), ((), ())), preferred_element_type=F32)


def _silu(x):
    return x * jax.nn.sigmoid(x)


def _layernorm(y, g, b):
    mu = jnp.mean(y, axis=-1, keepdims=True)
    d = y - mu
    var = jnp.mean(d * d, axis=-1, keepdims=True)
    return d * lax.rsqrt(var + LN_EPS) * g + b


def _zero_after(v):
    sub = V7X_SUBLANES * (4 // v.dtype.itemsize)
    r, c = v.shape
    m = jnp.max(v.reshape(r // sub, sub, c), axis=0)
    m = functools.reduce(jnp.maximum, [m[:, j * V7X_LANES:(j + 1) * V7X_LANES] for j in range(c // V7X_LANES)])
    m = m.astype(F32)
    if sub > V7X_SUBLANES:
        m = jnp.maximum(m[:V7X_SUBLANES], m[V7X_SUBLANES:])
    bits = pltpu.bitcast(m, jnp.uint32)
    return pltpu.bitcast((bits >> 16) >> 16, F32)


def _resident(shape):
    zeros = (0,) * len(shape)
    return pl.BlockSpec(shape, lambda *_: zeros, pipeline_mode=pl.Buffered(1))


def _ffn_ln_kernel(*refs, n_tiles, n_side):
    x_ref, win_ref, wout_ref, g_ref, b_ref = refs[:5]
    side_in = refs[5:5 + n_side]
    o_ref = refs[5 + n_side]
    side_out = refs[6 + n_side:6 + 2 * n_side]
    y_s = refs[-1]
    step = pl.program_id(0)
    ln_rows = FFN_TILE // FFN_LN_PIECES
    gaps = len(FF_CHUNKS) - 1

    def ln_piece(p):
        rows = slice(p * ln_rows, (p + 1) * ln_rows)
        out = _layernorm(y_s[rows, :], g_ref[...], b_ref[...])
        o_ref[rows, :] = out
        return out

    def cast_side(n):
        v = side_in[n][...].astype(BF16)
        side_out[n][...] = v
        return v

    @pl.when(step == 0)
    def _():
        y_s[...] = jnp.zeros_like(y_s)

    @pl.when(step < n_tiles)
    def _():
        x = x_ref[...]
        xb = x.astype(BF16)
        acc = None
        jobs = [(FFN_TILE * D_MODEL // FFN_LN_PIECES * 3, functools.partial(ln_piece, p)) for p in range(FFN_LN_PIECES)]
        jobs += [(side_in[n].shape[0] * side_in[n].shape[1], functools.partial(cast_side, n)) for n in range(n_side)]
        bins = [[0, []] for _ in range(gaps)]
        for cost, job in sorted(jobs, key=lambda cj: -cj[0]):
            target = min(bins, key=lambda bn: bn[0])
            target[0] += cost
            target[1].append(job)
        lhs = xb
        for n, (c0, cw) in enumerate(FF_CHUNKS):
            gate = _dot(lhs, win_ref[:, c0:c0 + cw])
            up = _dot(lhs, win_ref[:, D_FF + c0:D_FF + c0 + cw])
            act = (_silu(gate) * up).astype(BF16)
            part = _dot(act, wout_ref[c0:c0 + cw, :])
            acc = part if acc is None else acc + part
            if n < gaps and bins[n][1]:
                zeros = [_zero_after(job()) for job in bins[n][1]]
                zero = sum(zeros[1:], zeros[0]).astype(BF16)
                lhs = xb + jnp.tile(zero, (FFN_TILE // V7X_SUBLANES, D_MODEL // V7X_LANES))
        y_s[...] = ALPHA * x + 0.5 * acc

    @pl.when(step == n_tiles)
    def _():
        for p in range(FFN_LN_PIECES):
            ln_piece(p)


def _side_blocks(w, layer, n_steps):
    _, r, c = w.shape
    packed_rows = 2 * V7X_SUBLANES
    rows = next(n for n in range(packed_rows, r + 1, packed_rows) if r % n == 0 and r // n <= n_steps)
    index = lambda i: (jnp.minimum(i, r // rows - 1), 0)
    return (pl.BlockSpec((None, rows, c), lambda i: (layer,) + index(i)), pl.BlockSpec((rows, c), index),
            jax.ShapeDtypeStruct((r, c), BF16))


def _ffn_ln(x2d, w_in, w_out, g, b, side=(), layer=0):
    m = x2d.shape[0]
    assert m % FFN_TILE == 0
    n_tiles = m // FFN_TILE
    current = lambda i: (jnp.minimum(i, n_tiles - 1), 0)
    blocks = [_side_blocks(w, layer, n_tiles) for w in side]
    out = pl.pallas_call(
        functools.partial(_ffn_ln_kernel, n_tiles=n_tiles, n_side=len(side)),
        grid=(n_tiles + 1,),
        in_specs=[
            pl.BlockSpec((FFN_TILE, D_MODEL), current),
            _resident((D_MODEL, 2 * D_FF)),
            _resident((D_FF, D_MODEL)),
            _resident((1, D_MODEL)),
            _resident((1, D_MODEL)),
        ] + [blk[0] for blk in blocks],
        out_specs=[pl.BlockSpec((FFN_TILE, D_MODEL), lambda i: (jnp.maximum(i - 1, 0), 0))]
        + [blk[1] for blk in blocks],
        out_shape=[jax.ShapeDtypeStruct((m, D_MODEL), F32)] + [blk[2] for blk in blocks],
        scratch_shapes=[pltpu.VMEM((FFN_TILE, D_MODEL), F32)],
        compiler_params=pltpu.CompilerParams(
            dimension_semantics=("arbitrary",), vmem_limit_bytes=V7X_VMEM_LIMIT),
        name="ffn_ln",
    )(x2d, w_in, w_out, g, b, *side)
    return out[0], out[1:]


N_BLK = CHUNK // SUB
CUM_GROUP = 4
FAR_ORDER = (7, 6, 3, 5, 4, 2, 1)
FAR_COLS = 256
NEAR_LEVELS = (0, 1, 2, 4)
PV_COLS = FAR_COLS + len(NEAR_LEVELS) * CHUNK
assert sorted(FAR_ORDER) == list(range(1, N_BLK)) and SUB * sum(FAR_ORDER) <= FAR_COLS


def _chunk_constants():
    t = np.arange(CHUNK)[:, None]
    s = np.arange(CHUNK)[None, :]
    tri = np.kron(np.eye(CUM_GROUP), (s <= t)).astype(np.float32)
    mask = np.zeros((CHUNK, PV_COLS), np.float32)
    off = 0
    for i in FAR_ORDER:
        mask[:, off:off + SUB * i] = (t // SUB == i)
        off += SUB * i
    for n, b in enumerate(NEAR_LEVELS):
        valid = (t == s) if b == 0 else (((t // b) % 2 == 1) & ((s // b) == (t // b) - 1))
        mask[:, FAR_COLS + n * CHUNK:FAR_COLS + (n + 1) * CHUNK] = valid
    return jnp.asarray(tri, BF16), jnp.asarray(mask, BF16)


def _split3(x):
    hi = x.astype(BF16)
    r = x - hi.astype(F32)
    mid = r.astype(BF16)
    lo = (r - mid.astype(F32)).astype(BF16)
    return hi, mid, lo


def _mixer_ln_kernel(hn_ref, hc_ref, win_ref, lb_ref, gn_ref, pw_ref, ps_ref, wout_ref, g_ref, b_ref,
                     tri_ref, mask_ref, o_ref,
                     q_a, k_a, f_a, i_a, cum_a, gate_a, vb_a,
                     q_s, k_s, f_s, i_s, cum_s, gate_s, vb_s, hb_s, merged_s, y_s, state_s, *, n_seq):
    ts = hn_ref.shape[0]
    step = pl.program_id(0)
    seq_tile_next = step % n_seq
    seq_tile = (step + n_seq - 1) % n_seq
    handoff = ((q_a, q_s), (k_a, k_s), (f_a, f_s), (i_a, i_s), (cum_a, cum_s), (gate_a, gate_s), (vb_a, vb_s))

    @pl.when(step == 0)
    def _():
        state_s[...] = jnp.zeros_like(state_s)
        for src, _ in handoff:
            src[...] = jnp.zeros_like(src)

    col_blk = V7X_MXU_DIM

    def copy_rows(c):
        rows = slice(c * CHUNK, (c + 1) * CHUNK)
        for src, dst in handoff[:-1]:
            dst[rows, :] = src[rows, :]
        if c == 0:
            vb_s[0:POOL_HALO, :] = vb_a[0:POOL_HALO, :]
        prow = slice(POOL_HALO + c * CHUNK, POOL_HALO + (c + 1) * CHUNK)
        vb_s[prow, :] = vb_a[prow, :]

    def pool_halo():
        vb_a[0:POOL_HALO, :] = jnp.where(seq_tile_next == 0, 0.0, vb_s[ts:ts + POOL_HALO, :])

    a = lb_ref[...]
    e = jnp.exp(a - jnp.max(a, axis=0, keepdims=True))
    lower = e[0:1, :] / jnp.sum(e, axis=0, keepdims=True)

    def cast_next():
        hb_s[...] = hn_ref[...].astype(BF16)

    def in_proj(j):
        x = _dot(hb_s[...], win_ref[:, j * col_blk:(j + 1) * col_blk])
        kind, cb = divmod(j * col_blk, D_HGRN)
        cols = slice(cb, cb + col_blk)
        if kind == 0:
            q_a[:, cols] = _silu(x)
        elif kind == 1:
            forget = lower[:, cols] + (1.0 - lower[:, cols]) * jax.nn.sigmoid(x)
            f_a[:, cols] = forget
            k_a[:, cols] = 1.0 - forget
            cum_a[:, cols] = jnp.log(forget)
        elif kind == 2:
            i_a[:, cols] = x
        elif kind == 3:
            gate_a[:, cols] = _silu(x)
        else:
            vb_a[POOL_HALO:POOL_HALO + ts, cols] = x

    def cum_chunks(g):
        rows = slice(g * CUM_GROUP * CHUNK, (g + 1) * CUM_GROUP * CHUNK)
        tri = tri_ref[...]
        hi, mid, lo = _split3(cum_a[rows, :])
        cum_a[rows, :] = _dot(tri, hi) + _dot(tri, mid) + _dot(tri, lo)

    sub_row = lax.broadcasted_iota(jnp.int32, (SUB, HEAD_DIM), 0)
    zero_blk = jnp.zeros((SUB, HEAD_DIM), F32)
    far_pad = jnp.zeros((FAR_COLS - SUB * sum(FAR_ORDER), HEAD_DIM), F32)
    gn = gn_ref[...]
    zero_kb = jnp.zeros((CHUNK, HEAD_DIM), BF16)

    def block_diag(a, b):
        return jnp.concatenate([jnp.concatenate([a, zero_kb], axis=1),
                                jnp.concatenate([zero_kb, b], axis=1)], axis=0)

    def head_cols(hd):
        return slice(hd * HEAD_DIM, (hd + 1) * HEAD_DIM)

    def scores_stage(c, hd):
        r0 = c * CHUNK
        rows = pl.ds(r0, CHUNK)
        cols = head_cols(hd)
        q = q_s[rows, cols]
        k = k_s[rows, cols]
        f = f_s[rows, cols]
        v = i_s[rows, cols]
        cum = cum_s[rows, cols]

        def cum_row(row):
            return jnp.broadcast_to(cum_s[pl.ds(r0 + row, 1), cols], (SUB, HEAD_DIM))

        bound = [zero_blk] + [cum_row(SUB * j - 1) for j in range(1, N_BLK + 1)]

        q_far = (q * jnp.exp(cum - jnp.concatenate(bound[:N_BLK], axis=0))).astype(BF16)
        k_parts, v_parts = [], []
        for i in FAR_ORDER:
            n = SUB * i
            k_parts.append(k[:n] * jnp.exp(jnp.concatenate([bound[i]] * i, axis=0) - cum[:n]))
            v_parts.append(v[:n])
        k_far = jnp.concatenate(k_parts + [far_pad], axis=0).astype(BF16)
        s_far = _dot_nt(q_far, k_far)

        ref_q2, ref_k2, ref_q4, ref_k4 = [], [], [], []
        for j in range(N_BLK):
            lo_b, hi_b = bound[j], bound[j + 1]
            m1, m3, m5 = (cum_row(SUB * j + r) for r in (1, 3, 5))
            ref_q4.append(jnp.where(sub_row < 4, lo_b, m3))
            ref_k4.append(jnp.where(sub_row < 4, m3, hi_b))
            ref_q2.append(jnp.where(sub_row < 2, lo_b, jnp.where(sub_row < 4, m1, jnp.where(sub_row < 6, m3, m5))))
            ref_k2.append(jnp.where(sub_row < 2, m1, jnp.where(sub_row < 4, m3, jnp.where(sub_row < 6, m5, hi_b))))
        cat = lambda parts: jnp.concatenate(parts, axis=0)
        q2 = q * jnp.exp(cum - cat(ref_q2))
        k2 = k * jnp.exp(cat(ref_k2) - cum)
        q4 = q * jnp.exp(cum - cat(ref_q4))
        k4 = k * jnp.exp(cat(ref_k4) - cum)
        kb = k.astype(BF16)
        s_a = _dot_nt(jnp.concatenate([q, q * f], axis=1).astype(BF16), block_diag(kb, kb))
        s_b = _dot_nt(jnp.concatenate([q2, q4], axis=1).astype(BF16),
                      block_diag(k2.astype(BF16), k4.astype(BF16)))

        total = bound[N_BLK]
        return dict(
            p=jnp.concatenate([s_far, s_a, s_b], axis=1).astype(BF16) * mask_ref[...],
            v_all=jnp.concatenate(v_parts + [far_pad] + [v] * len(NEAR_LEVELS), axis=0).astype(BF16),
            q_dec=(q * jnp.exp(cum)).astype(BF16),
            k_dec=(k * jnp.exp(jnp.concatenate([total] * N_BLK, axis=0) - cum)).astype(BF16),
            vb=v.astype(BF16),
            decay=jnp.exp(total[0:1, :]))

    def state_stage(c, hd, sc):
        st = state_s[hd]
        if c == 0:
            st = jnp.where(seq_tile == 0, 0.0, st)
        o = _dot(sc["p"], sc["v_all"]) + _dot_nt(sc["q_dec"], st.astype(BF16))
        state_s[hd] = st * sc["decay"] + _dot_tn(sc["vb"], sc["k_dec"])
        return o

    def norm_stage(c, hd, o):
        rows = pl.ds(c * CHUNK, CHUNK)
        cols = head_cols(hd)
        o = o * lax.rsqrt(jnp.mean(o * o, axis=-1, keepdims=True) + RMS_EPS) * gn
        merged_s[rows, cols] = (o * gate_s[rows, cols]).astype(BF16)

    quarter = ts // 4

    def pool_rows(r0, nr):
        pos = seq_tile * ts + r0 + lax.broadcasted_iota(jnp.int32, (nr, 1), 0) + 1
        for grp, win in enumerate(POOL_WINDOWS):
            cols = slice(grp * POOL_GROUP_DIM, (grp + 1) * POOL_GROUP_DIM)
            ext = vb_s[r0:r0 + POOL_HALO + nr, cols]
            wsum = ext
            span = 1
            while span < win:
                wsum = wsum + pltpu.roll(wsum, span, 0)
                span *= 2
            cur = ext[POOL_HALO:]
            count = jnp.minimum(pos, win).astype(F32)
            pooled = wsum[POOL_HALO:] / count - cur
            mixed = _dot(pooled.astype(BF16), pw_ref[grp]) * ps_ref[:, cols]
            merged_s[r0:r0 + nr, D_HGRN + grp * POOL_GROUP_DIM:D_HGRN + (grp + 1) * POOL_GROUP_DIM] = (
                mixed.astype(BF16))

    def out_proj(r0, nr, n2):
        rows = slice(r0, r0 + nr)
        cols = slice(n2 * (D_MODEL // 2), (n2 + 1) * (D_MODEL // 2))
        y_s[rows, cols] = _dot(merged_s[rows, :], wout_ref[:, cols])

    def out_ln(qd):
        rows = slice(qd * quarter, (qd + 1) * quarter)
        o_ref[rows, :] = _layernorm(ALPHA * hc_ref[rows, :] + y_s[rows, :], g_ref[...], b_ref[...])

    n_chunks = ts // CHUNK
    n_slots = n_chunks * HGRN_HEADS
    P = functools.partial
    COPY, PROJ, CUM, POOL, OUT, LN = 100, 256, 192, 200, 256, 150
    early = [(0, COPY, P(copy_rows, c)) for c in range(2, n_chunks)] + [(0, 10, pool_halo)]
    blocks = list(range(D_IN_PROJ // col_blk))
    forget_blocks = [j for j in blocks if j * col_blk // D_HGRN == 1]
    first = 4
    early += [(first, COPY, cast_next)]
    early += [(first, 2 * PROJ, P(in_proj, j)) for j in forget_blocks]
    early += [(first, 2 * CUM, P(cum_chunks, g)) for g in range(n_chunks // CUM_GROUP)]
    early += [(first, 2 * PROJ, P(in_proj, j)) for j in blocks if j not in forget_blocks]
    late = []
    for r0, nr in ((0, 3 * quarter), (3 * quarter, quarter)):
        first = (r0 + nr) // CHUNK * HGRN_HEADS
        late += [(first, POOL * nr // quarter, P(pool_rows, r0, nr))]
        late += [(first, OUT * nr // quarter, P(out_proj, r0, nr, n2)) for n2 in range(2)]
        late += [(first, LN, P(out_ln, qd)) for qd in range(r0 // quarter, (r0 + nr) // quarter)]
    total = sum(cost for _, cost, _ in early + late)
    copy_rows(0)
    copy_rows(1)
    done = 0
    ch = lambda slot: (slot // HGRN_HEADS, slot % HGRN_HEADS)
    scores, outs = {}, {}
    for slot in range(n_slots + 2):
        if slot < n_slots:
            scores[slot] = scores_stage(*ch(slot))
        if 1 <= slot <= n_slots:
            outs[slot - 1] = state_stage(*ch(slot - 1), scores.pop(slot - 1))
        if slot >= 2:
            norm_stage(*ch(slot - 2), outs.pop(slot - 2))
        while done < total * (slot + 1) / n_slots:
            queue = late if late and late[0][0] <= slot - 1 else early
            if not queue or queue[0][0] > slot - 1:
                break
            _, cost, fn = queue.pop(0)
            fn()
            done += cost
    for _, _, fn in early + late:
        fn()


def _mixer_ln(h2d, batch, seq, w_in, lb, gnorm, pool_w, pool_scale, w_out, g, b):
    ts = MIX_TILE
    assert seq % ts == 0 and ts % CHUNK == 0
    n_seq = seq // ts
    n_tiles = batch * n_seq
    tri, mask = _chunk_constants()
    next_tile = pl.BlockSpec((ts, D_MODEL), lambda i: (jnp.minimum(i, n_tiles - 1), 0))
    done_tile = pl.BlockSpec((ts, D_MODEL), lambda i: (jnp.maximum(i - 1, 0), 0))
    proj_set = [pltpu.VMEM((ts, D_HGRN), F32)] * 6 + [pltpu.VMEM((ts + POOL_HALO, D_POOL), F32)]
    return pl.pallas_call(
        functools.partial(_mixer_ln_kernel, n_seq=n_seq),
        grid=(n_tiles + 1,),
        in_specs=[
            next_tile,
            done_tile,
            _resident(w_in.shape),
            _resident(lb.shape),
            _resident(gnorm.shape),
            _resident(pool_w.shape),
            _resident(pool_scale.shape),
            _resident(w_out.shape),
            _resident(g.shape),
            _resident(b.shape),
            _resident(tri.shape),
            _resident(mask.shape),
        ],
        out_specs=done_tile,
        out_shape=jax.ShapeDtypeStruct(h2d.shape, F32),
        scratch_shapes=proj_set + proj_set + [
            pltpu.VMEM((ts, D_MODEL), BF16),
            pltpu.VMEM((ts, D_HGRN + D_POOL), BF16),
            pltpu.VMEM((ts, D_MODEL), F32),
            pltpu.VMEM((HGRN_HEADS, HEAD_DIM, HEAD_DIM), F32),
        ],
        compiler_params=pltpu.CompilerParams(
            dimension_semantics=("arbitrary",), vmem_limit_bytes=V7X_VMEM_LIMIT),
        name="mixer_ln",
    )(h2d, h2d, w_in, lb, gnorm, pool_w, pool_scale, w_out, g, b, tri, mask)


def _kv_proj_kernel(*refs, n_side):
    mem_ref, wk_ref, wv_ref = refs[:3]
    side_in = refs[3:3 + n_side]
    k_ref, v_ref = refs[3 + n_side:5 + n_side]
    side_out = refs[5 + n_side:5 + 2 * n_side]
    wk_b, wv_b = refs[-2:]

    @pl.when(pl.program_id(0) == 0)
    def _():
        wk_b[...] = wk_ref[...].astype(BF16)
        wv_b[...] = wv_ref[...].astype(BF16)

    mb = mem_ref[...].astype(BF16)
    k_ref[...] = _dot(mb, wk_b[...]).astype(BF16)
    v_ref[...] = _dot(mb, wv_b[...]).astype(BF16)
    for src, dst in zip(side_in, side_out):
        dst[...] = src[...].astype(BF16)


def _kv_proj(mem2d, n_mem, wk, wv, side=(), layer=0):
    rows = mem2d.shape[0]
    n_steps = rows // n_mem
    tile = pl.BlockSpec((n_mem, D_MODEL), lambda i: (i, 0))
    weight = pl.BlockSpec((None, D_MODEL, D_MODEL), lambda i: (layer, 0, 0), pipeline_mode=pl.Buffered(1))
    blocks = [_side_blocks(w, layer, n_steps) for w in side]
    out = pl.pallas_call(
        functools.partial(_kv_proj_kernel, n_side=len(side)),
        grid=(n_steps,),
        in_specs=[tile, weight, weight] + [blk[0] for blk in blocks],
        out_specs=[tile, tile] + [blk[1] for blk in blocks],
        out_shape=[jax.ShapeDtypeStruct((rows, D_MODEL), BF16)] * 2 + [blk[2] for blk in blocks],
        scratch_shapes=[pltpu.VMEM((D_MODEL, D_MODEL), BF16)] * 2,
        compiler_params=pltpu.CompilerParams(
            dimension_semantics=("arbitrary",), vmem_limit_bytes=V7X_VMEM_LIMIT),
        name="kv_proj",
    )(mem2d, wk, wv, *side)
    return out[0], out[1], out[2:]


def _xattn_ln_kernel(h_ref, k_ref, v_ref, wq_ref, wo_ref, g_ref, b_ref, o_ref, y_s, *, n_tiles):
    step = pl.program_id(0)
    ts = h_ref.shape[0]
    ln_rows = ts // XA_LN_PIECES

    def ln_piece(p):
        rows = slice(p * ln_rows, (p + 1) * ln_rows)
        out = _layernorm(y_s[rows, :], g_ref[...], b_ref[...])
        o_ref[rows, :] = out
        return out

    @pl.when(step == 0)
    def _():
        y_s[...] = jnp.zeros_like(y_s)

    @pl.when(step < n_tiles)
    def _():
        h = h_ref[...]
        hb = h.astype(BF16)
        q = _dot(hb, wq_ref[...])
        pieces = list(range(XA_LN_PIECES))
        heads, raw = [], []
        for hd in range(XA_HEADS):
            cols = slice(hd * XA_HEAD_DIM, (hd + 1) * XA_HEAD_DIM)
            qh = q[:, cols].astype(BF16)
            if pieces:
                take = min(len(pieces), XA_LN_FIRST if hd == 0 else 1)
                zeros = [_zero_after(ln_piece(pieces.pop(0))) for _ in range(take)]
                zero = sum(zeros[1:], zeros[0]).astype(BF16)
                qh = qh + jnp.tile(zero, (ts // V7X_SUBLANES, XA_HEAD_DIM // V7X_LANES))
            raw.append(_dot_nt(qh, k_ref[:, cols]))
        for hd in range(XA_HEADS):
            cols = slice(hd * XA_HEAD_DIM, (hd + 1) * XA_HEAD_DIM)
            s = raw[hd] * (XA_HEAD_DIM ** -0.5)
            s = s - jnp.max(s, axis=-1, keepdims=True)
            p = jnp.exp(s)
            p = p / jnp.sum(p, axis=-1, keepdims=True)
            heads.append(_dot(p.astype(BF16), v_ref[:, cols]).astype(BF16))
        assert not pieces
        y_s[...] = ALPHA * h + _dot(jnp.concatenate(heads, axis=-1), wo_ref[...])

    @pl.when(step == n_tiles)
    def _():
        for p in range(XA_LN_PIECES):
            ln_piece(p)


def _xattn_ln(h2d, batch, seq, k2d, v2d, n_mem, wq, wo, g, b):
    ts = XA_TILE
    assert seq % ts == 0
    n_seq = seq // ts
    n_tiles = batch * n_seq
    attended = lambda i: jnp.minimum(i, n_tiles - 1)
    tile = pl.BlockSpec((ts, D_MODEL), lambda i: (attended(i), 0))
    mem_tile = pl.BlockSpec((n_mem, D_MODEL), lambda i: (attended(i) // n_seq, 0))
    return pl.pallas_call(
        functools.partial(_xattn_ln_kernel, n_tiles=n_tiles),
        grid=(n_tiles + 1,),
        in_specs=[tile, mem_tile, mem_tile, _resident(wq.shape), _resident(wo.shape),
                  _resident(g.shape), _resident(b.shape)],
        out_specs=pl.BlockSpec((ts, D_MODEL), lambda i: (jnp.maximum(i - 1, 0), 0)),
        out_shape=jax.ShapeDtypeStruct(h2d.shape, F32),
        scratch_shapes=[pltpu.VMEM((ts, D_MODEL), F32)],
        compiler_params=pltpu.CompilerParams(
            dimension_semantics=("arbitrary",), vmem_limit_bytes=V7X_VMEM_LIMIT),
        name="xattn_ln",
    )(h2d, k2d, v2d, wq, wo, g, b)


def kernel(x, mem, w_ffn1_in, w_ffn1_out, ln1_g, ln1_b, w_mix_in, hgrn_lb, hgrn_gnorm, pool_w, pool_scale, w_mix_out, ln2_g, ln2_b, xa_wq, xa_wk, xa_wv, xa_wo, ln3_g, ln3_b, w_ffn2_in, w_ffn2_out, ln4_g, ln4_b):
    batch, seq, _ = x.shape
    n_mem = mem.shape[1]
    assert w_ffn1_in.shape[0] == DEPTH == 1
    h = x.reshape(batch * seq, D_MODEL)
    mem2d = mem.reshape(batch * n_mem, D_MODEL)
    for l in range(DEPTH):
        k2d, v2d, (ffn1_in, ffn1_out) = _kv_proj(mem2d, n_mem, xa_wk, xa_wv, side=(w_ffn1_in, w_ffn1_out), layer=l)
        later = (w_mix_in, pool_w.reshape(DEPTH, D_POOL, POOL_GROUP_DIM), w_mix_out, xa_wq, xa_wo,
                 w_ffn2_in, w_ffn2_out)
        h, later = _ffn_ln(h, ffn1_in, ffn1_out, ln1_g[l:l + 1], ln1_b[l:l + 1], side=later, layer=l)
        mix_in, pool, mix_out, wq, wo, ffn2_in, ffn2_out = later
        pool = pool.reshape(len(POOL_WINDOWS), POOL_GROUP_DIM, POOL_GROUP_DIM)
        h = _mixer_ln(h, batch, seq, mix_in, hgrn_lb, hgrn_gnorm[l:l + 1], pool,
                      pool_scale[l:l + 1], mix_out, ln2_g[l:l + 1], ln2_b[l:l + 1])
        h = _xattn_ln(h, batch, seq, k2d, v2d, n_mem, wq, wo, ln3_g[l:l + 1], ln3_b[l:l + 1])
        h, _ = _ffn_ln(h, ffn2_in, ffn2_out, ln4_g[l:l + 1], ln4_b[l:l + 1])
    return h.reshape(batch, seq, D_MODEL)
```

```python
import functools

import numpy as np
import jax
import jax.numpy as jnp
from jax import lax
from jax.experimental import pallas as pl
from jax.experimental.pallas import tpu as pltpu

F32 = jnp.float32
BF16 = jnp.bfloat16

D_MODEL = 1024
DEPTH = 1
D_HGRN = 512
D_POOL = 512
HGRN_HEADS = 4
HEAD_DIM = 128
POOL_WINDOWS = (2, 4, 8, 16)
POOL_GROUP_DIM = 128
D_FF = 2816
D_IN_PROJ = 4 * D_HGRN + D_POOL
XA_HEADS = 4
XA_HEAD_DIM = 256
ALPHA = (2.0 * DEPTH) ** 0.25
LN_EPS = 1e-5
RMS_EPS = 1e-6

V7X_SUBLANES = 8
V7X_LANES = 128
V7X_MXU_DIM = 256
V7X_VMEM_LIMIT = 56 * 1024 * 1024

CHUNK = 64
SUB = V7X_SUBLANES
POOL_HALO = 16
FFN_TILE = 512
FFN_LN_PIECES = 4
MIX_TILE = 512
XA_TILE = 512
XA_LN_PIECES = 4
XA_LN_FIRST = 1
FF_CHUNKS = tuple((c0, min(512, D_FF - c0)) for c0 in range(0, D_FF, 512))


def _dot(a, b):
    return jnp.dot(a, b, preferred_element_type=F32)


def _dot_nt(a, b):
    return lax.dot_general(a, b, (((1,), (1,)), ((), ())), preferred_element_type=F32)


def _dot_tn(a, b):
    return lax.dot_general(a, b, (((0,), (0,)), ((), ())), preferred_element_type=F32)


def _silu(x):
    return x * jax.nn.sigmoid(x)


def _layernorm(y, g, b):
    mu = jnp.mean(y, axis=-1, keepdims=True)
    d = y - mu
    var = jnp.mean(d * d, axis=-1, keepdims=True)
    return d * lax.rsqrt(var + LN_EPS) * g + b


def _zero_after(v):
    sub = V7X_SUBLANES * (4 // v.dtype.itemsize)
    r, c = v.shape
    m = jnp.max(v.reshape(r // sub, sub, c), axis=0)
    m = functools.reduce(jnp.maximum, [m[:, j * V7X_LANES:(j + 1) * V7X_LANES] for j in range(c // V7X_LANES)])
    m = m.astype(F32)
    if sub > V7X_SUBLANES:
        m = jnp.maximum(m[:V7X_SUBLANES], m[V7X_SUBLANES:])
    bits = pltpu.bitcast(m, jnp.uint32)
    return pltpu.bitcast((bits >> 16) >> 16, F32)


def _resident(shape):
    zeros = (0,) * len(shape)
    return pl.BlockSpec(shape, lambda *_: zeros, pipeline_mode=pl.Buffered(1))


def _ffn_ln_kernel(*refs, n_tiles, n_side):
    x_ref, win_ref, wout_ref, g_ref, b_ref = refs[:5]
    side_in = refs[5:5 + n_side]
    o_ref = refs[5 + n_side]
    side_out = refs[6 + n_side:6 + 2 * n_side]
    y_s = refs[-1]
    step = pl.program_id(0)
    ln_rows = FFN_TILE // FFN_LN_PIECES
    gaps = len(FF_CHUNKS) - 1

    def ln_piece(p):
        rows = slice(p * ln_rows, (p + 1) * ln_rows)
        out = _layernorm(y_s[rows, :], g_ref[...], b_ref[...])
        o_ref[rows, :] = out
        return out

    def cast_side(n):
        v = side_in[n][...].astype(BF16)
        side_out[n][...] = v
        return v

    @pl.when(step == 0)
    def _():
        y_s[...] = jnp.zeros_like(y_s)

    @pl.when(step < n_tiles)
    def _():
        x = x_ref[...]
        xb = x.astype(BF16)
        acc = None
        jobs = [(FFN_TILE * D_MODEL // FFN_LN_PIECES * 3, functools.partial(ln_piece, p)) for p in range(FFN_LN_PIECES)]
        jobs += [(side_in[n].shape[0] * side_in[n].shape[1], functools.partial(cast_side, n)) for n in range(n_side)]
        bins = [[0, []] for _ in range(gaps)]
        for cost, job in sorted(jobs, key=lambda cj: -cj[0]):
            target = min(bins, key=lambda bn: bn[0])
            target[0] += cost
            target[1].append(job)
        lhs = xb
        for n, (c0, cw) in enumerate(FF_CHUNKS):
            gate = _dot(lhs, win_ref[:, c0:c0 + cw])
            up = _dot(lhs, win_ref[:, D_FF + c0:D_FF + c0 + cw])
            act = (_silu(gate) * up).astype(BF16)
            part = _dot(act, wout_ref[c0:c0 + cw, :])
            acc = part if acc is None else acc + part
            if n < gaps and bins[n][1]:
                zeros = [_zero_after(job()) for job in bins[n][1]]
                zero = sum(zeros[1:], zeros[0]).astype(BF16)
                lhs = xb + jnp.tile(zero, (FFN_TILE // V7X_SUBLANES, D_MODEL // V7X_LANES))
        y_s[...] = ALPHA * x + 0.5 * acc

    @pl.when(step == n_tiles)
    def _():
        for p in range(FFN_LN_PIECES):
            ln_piece(p)


def _side_blocks(w, layer, n_steps):
    _, r, c = w.shape
    packed_rows = 2 * V7X_SUBLANES
    rows = next(n for n in range(packed_rows, r + 1, packed_rows) if r % n == 0 and r // n <= n_steps)
    index = lambda i: (jnp.minimum(i, r // rows - 1), 0)
    return (pl.BlockSpec((None, rows, c), lambda i: (layer,) + index(i)), pl.BlockSpec((rows, c), index),
            jax.ShapeDtypeStruct((r, c), BF16))


def _ffn_ln(x2d, w_in, w_out, g, b, side=(), layer=0):
    m = x2d.shape[0]
    assert m % FFN_TILE == 0
    n_tiles = m // FFN_TILE
    current = lambda i: (jnp.minimum(i, n_tiles - 1), 0)
    blocks = [_side_blocks(w, layer, n_tiles) for w in side]
    out = pl.pallas_call(
        functools.partial(_ffn_ln_kernel, n_tiles=n_tiles, n_side=len(side)),
        grid=(n_tiles + 1,),
        in_specs=[
            pl.BlockSpec((FFN_TILE, D_MODEL), current),
            _resident((D_MODEL, 2 * D_FF)),
            _resident((D_FF, D_MODEL)),
            _resident((1, D_MODEL)),
            _resident((1, D_MODEL)),
        ] + [blk[0] for blk in blocks],
        out_specs=[pl.BlockSpec((FFN_TILE, D_MODEL), lambda i: (jnp.maximum(i - 1, 0), 0))]
        + [blk[1] for blk in blocks],
        out_shape=[jax.ShapeDtypeStruct((m, D_MODEL), F32)] + [blk[2] for blk in blocks],
        scratch_shapes=[pltpu.VMEM((FFN_TILE, D_MODEL), F32)],
        compiler_params=pltpu.CompilerParams(
            dimension_semantics=("arbitrary",), vmem_limit_bytes=V7X_VMEM_LIMIT),
        name="ffn_ln",
    )(x2d, w_in, w_out, g, b, *side)
    return out[0], out[1:]


N_BLK = CHUNK // SUB
CUM_GROUP = 4
FAR_ORDER = (7, 6, 3, 5, 4, 2, 1)
FAR_COLS = 256
NEAR_LEVELS = (0, 1, 2, 4)
PV_COLS = FAR_COLS + len(NEAR_LEVELS) * CHUNK
assert sorted(FAR_ORDER) == list(range(1, N_BLK)) and SUB * sum(FAR_ORDER) <= FAR_COLS


def _chunk_constants():
    t = np.arange(CHUNK)[:, None]
    s = np.arange(CHUNK)[None, :]
    tri = np.kron(np.eye(CUM_GROUP), (s <= t)).astype(np.float32)
    mask = np.zeros((CHUNK, PV_COLS), np.float32)
    off = 0
    for i in FAR_ORDER:
        mask[:, off:off + SUB * i] = (t // SUB == i)
        off += SUB * i
    for n, b in enumerate(NEAR_LEVELS):
        valid = (t == s) if b == 0 else (((t // b) % 2 == 1) & ((s // b) == (t // b) - 1))
        mask[:, FAR_COLS + n * CHUNK:FAR_COLS + (n + 1) * CHUNK] = valid
    return jnp.asarray(tri, BF16), jnp.asarray(mask, BF16)


def _split3(x):
    hi = x.astype(BF16)
    r = x - hi.astype(F32)
    mid = r.astype(BF16)
    lo = (r - mid.astype(F32)).astype(BF16)
    return hi, mid, lo


def _mixer_ln_kernel(hn_ref, hc_ref, win_ref, lb_ref, gn_ref, pw_ref, ps_ref, wout_ref, g_ref, b_ref,
                     tri_ref, mask_ref, o_ref,
                     q_a, k_a, f_a, i_a, cum_a, gate_a, vb_a,
                     q_s, k_s, f_s, i_s, cum_s, gate_s, vb_s, hb_s, merged_s, y_s, state_s, *, n_seq):
    ts = hn_ref.shape[0]
    step = pl.program_id(0)
    seq_tile_next = step % n_seq
    seq_tile = (step + n_seq - 1) % n_seq
    handoff = ((q_a, q_s), (k_a, k_s), (f_a, f_s), (i_a, i_s), (cum_a, cum_s), (gate_a, gate_s), (vb_a, vb_s))

    @pl.when(step == 0)
    def _():
        state_s[...] = jnp.zeros_like(state_s)
        for src, _ in handoff:
            src[...] = jnp.zeros_like(src)

    col_blk = V7X_MXU_DIM

    def copy_rows(c):
        rows = slice(c * CHUNK, (c + 1) * CHUNK)
        for src, dst in handoff[:-1]:
            dst[rows, :] = src[rows, :]
        if c == 0:
            vb_s[0:POOL_HALO, :] = vb_a[0:POOL_HALO, :]
        prow = slice(POOL_HALO + c * CHUNK, POOL_HALO + (c + 1) * CHUNK)
        vb_s[prow, :] = vb_a[prow, :]

    def pool_halo():
        vb_a[0:POOL_HALO, :] = jnp.where(seq_tile_next == 0, 0.0, vb_s[ts:ts + POOL_HALO, :])

    a = lb_ref[...]
    e = jnp.exp(a - jnp.max(a, axis=0, keepdims=True))
    lower = e[0:1, :] / jnp.sum(e, axis=0, keepdims=True)

    def cast_next():
        hb_s[...] = hn_ref[...].astype(BF16)

    def in_proj(j):
        x = _dot(hb_s[...], win_ref[:, j * col_blk:(j + 1) * col_blk])
        kind, cb = divmod(j * col_blk, D_HGRN)
        cols = slice(cb, cb + col_blk)
        if kind == 0:
            q_a[:, cols] = _silu(x)
        elif kind == 1:
            forget = lower[:, cols] + (1.0 - lower[:, cols]) * jax.nn.sigmoid(x)
            f_a[:, cols] = forget
            k_a[:, cols] = 1.0 - forget
            cum_a[:, cols] = jnp.log(forget)
        elif kind == 2:
            i_a[:, cols] = x
        elif kind == 3:
            gate_a[:, cols] = _silu(x)
        else:
            vb_a[POOL_HALO:POOL_HALO + ts, cols] = x

    def cum_chunks(g):
        rows = slice(g * CUM_GROUP * CHUNK, (g + 1) * CUM_GROUP * CHUNK)
        tri = tri_ref[...]
        hi, mid, lo = _split3(cum_a[rows, :])
        cum_a[rows, :] = _dot(tri, hi) + _dot(tri, mid) + _dot(tri, lo)

    sub_row = lax.broadcasted_iota(jnp.int32, (SUB, HEAD_DIM), 0)
    zero_blk = jnp.zeros((SUB, HEAD_DIM), F32)
    far_pad = jnp.zeros((FAR_COLS - SUB * sum(FAR_ORDER), HEAD_DIM), F32)
    gn = gn_ref[...]
    zero_kb = jnp.zeros((CHUNK, HEAD_DIM), BF16)

    def block_diag(blocks):
        n = len(blocks)
        return jnp.concatenate([jnp.concatenate([blk if j == i else zero_kb for j in range(n)], axis=1)
                                for i, blk in enumerate(blocks)], axis=0)

    def head_cols(hd):
        return slice(hd * HEAD_DIM, (hd + 1) * HEAD_DIM)

    def scores_stage(c, hd):
        r0 = c * CHUNK
        rows = pl.ds(r0, CHUNK)
        cols = head_cols(hd)
        q = q_s[rows, cols]
        k = k_s[rows, cols]
        f = f_s[rows, cols]
        v = i_s[rows, cols]
        cum = cum_s[rows, cols]

        def cum_row(row):
            return jnp.broadcast_to(cum_s[pl.ds(r0 + row, 1), cols], (SUB, HEAD_DIM))

        bound = [zero_blk] + [cum_row(SUB * j - 1) for j in range(1, N_BLK + 1)]

        q_far = (q * jnp.exp(cum - jnp.concatenate(bound[:N_BLK], axis=0))).astype(BF16)
        k_parts, v_parts = [], []
        for i in FAR_ORDER:
            n = SUB * i
            k_parts.append(k[:n] * jnp.exp(jnp.concatenate([bound[i]] * i, axis=0) - cum[:n]))
            v_parts.append(v[:n])
        k_far = jnp.concatenate(k_parts + [far_pad], axis=0).astype(BF16)
        s_far = _dot_nt(q_far, k_far)

        ref_q2, ref_k2, ref_q4, ref_k4 = [], [], [], []
        for j in range(N_BLK):
            lo_b, hi_b = bound[j], bound[j + 1]
            m1, m3, m5 = (cum_row(SUB * j + r) for r in (1, 3, 5))
            ref_q4.append(jnp.where(sub_row < 4, lo_b, m3))
            ref_k4.append(jnp.where(sub_row < 4, m3, hi_b))
            ref_q2.append(jnp.where(sub_row < 2, lo_b, jnp.where(sub_row < 4, m1, jnp.where(sub_row < 6, m3, m5))))
            ref_k2.append(jnp.where(sub_row < 2, m1, jnp.where(sub_row < 4, m3, jnp.where(sub_row < 6, m5, hi_b))))
        cat = lambda parts: jnp.concatenate(parts, axis=0)
        q2 = q * jnp.exp(cum - cat(ref_q2))
        k2 = k * jnp.exp(cat(ref_k2) - cum)
        q4 = q * jnp.exp(cum - cat(ref_q4))
        k4 = k * jnp.exp(cat(ref_k4) - cum)
        kb = k.astype(BF16)
        s_near = _dot_nt(jnp.concatenate([q, q * f, q2, q4], axis=1).astype(BF16),
                         block_diag([kb, kb, k2.astype(BF16), k4.astype(BF16)]))

        total = bound[N_BLK]
        return dict(
            p=jnp.concatenate([s_far, s_near], axis=1).astype(BF16) * mask_ref[...],
            v_all=jnp.concatenate(v_parts + [far_pad] + [v] * len(NEAR_LEVELS), axis=0).astype(BF16),
            q_dec=(q * jnp.exp(cum)).astype(BF16),
            k_dec=(k * jnp.exp(jnp.concatenate([total] * N_BLK, axis=0) - cum)).astype(BF16),
            vb=v.astype(BF16),
            decay=jnp.exp(total[0:1, :]))

    def state_stage(c, hd, sc):
        st = state_s[hd]
        if c == 0:
            st = jnp.where(seq_tile == 0, 0.0, st)
        o = _dot(sc["p"], sc["v_all"]) + _dot_nt(sc["q_dec"], st.astype(BF16))
        state_s[hd] = st * sc["decay"] + _dot_tn(sc["vb"], sc["k_dec"])
        return o

    def norm_stage(c, hd, o):
        rows = pl.ds(c * CHUNK, CHUNK)
        cols = head_cols(hd)
        o = o * lax.rsqrt(jnp.mean(o * o, axis=-1, keepdims=True) + RMS_EPS) * gn
        merged_s[rows, cols] = (o * gate_s[rows, cols]).astype(BF16)

    quarter = ts // 4

    def pool_rows(r0, nr):
        pos = seq_tile * ts + r0 + lax.broadcasted_iota(jnp.int32, (nr, 1), 0) + 1
        for grp, win in enumerate(POOL_WINDOWS):
            cols = slice(grp * POOL_GROUP_DIM, (grp + 1) * POOL_GROUP_DIM)
            ext = vb_s[r0:r0 + POOL_HALO + nr, cols]
            wsum = ext
            span = 1
            while span < win:
                wsum = wsum + pltpu.roll(wsum, span, 0)
                span *= 2
            cur = ext[POOL_HALO:]
            count = jnp.minimum(pos, win).astype(F32)
            pooled = wsum[POOL_HALO:] / count - cur
            mixed = _dot(pooled.astype(BF16), pw_ref[grp]) * ps_ref[:, cols]
            merged_s[r0:r0 + nr, D_HGRN + grp * POOL_GROUP_DIM:D_HGRN + (grp + 1) * POOL_GROUP_DIM] = (
                mixed.astype(BF16))

    def out_proj(r0, nr, n2):
        rows = slice(r0, r0 + nr)
        cols = slice(n2 * (D_MODEL // 2), (n2 + 1) * (D_MODEL // 2))
        y_s[rows, cols] = _dot(merged_s[rows, :], wout_ref[:, cols])

    def out_ln(qd):
        rows = slice(qd * quarter, (qd + 1) * quarter)
        o_ref[rows, :] = _layernorm(ALPHA * hc_ref[rows, :] + y_s[rows, :], g_ref[...], b_ref[...])

    n_chunks = ts // CHUNK
    n_slots = n_chunks * HGRN_HEADS
    P = functools.partial
    COPY, PROJ, CUM, POOL, OUT, LN = 100, 256, 192, 200, 256, 150
    early = [(0, COPY, P(copy_rows, c)) for c in range(2, n_chunks)] + [(0, 10, pool_halo)]
    blocks = list(range(D_IN_PROJ // col_blk))
    forget_blocks = [j for j in blocks if j * col_blk // D_HGRN == 1]
    first = 4
    early += [(first, COPY, cast_next)]
    early += [(first, 2 * PROJ, P(in_proj, j)) for j in forget_blocks]
    early += [(first, 2 * CUM, P(cum_chunks, g)) for g in range(n_chunks // CUM_GROUP)]
    early += [(first, 2 * PROJ, P(in_proj, j)) for j in blocks if j not in forget_blocks]
    late = []
    for r0, nr in ((0, 3 * quarter), (3 * quarter, quarter)):
        first = (r0 + nr) // CHUNK * HGRN_HEADS
        late += [(first, POOL * nr // quarter, P(pool_rows, r0, nr))]
        late += [(first, OUT * nr // quarter, P(out_proj, r0, nr, n2)) for n2 in range(2)]
        late += [(first, LN, P(out_ln, qd)) for qd in range(r0 // quarter, (r0 + nr) // quarter)]
    total = sum(cost for _, cost, _ in early + late)
    copy_rows(0)
    copy_rows(1)
    done = 0
    ch = lambda slot: (slot // HGRN_HEADS, slot % HGRN_HEADS)
    scores, outs = {}, {}
    for slot in range(n_slots + 2):
        if slot < n_slots:
            scores[slot] = scores_stage(*ch(slot))
        if 1 <= slot <= n_slots:
            outs[slot - 1] = state_stage(*ch(slot - 1), scores.pop(slot - 1))
        if slot >= 2:
            norm_stage(*ch(slot - 2), outs.pop(slot - 2))
        while done < total * (slot + 1) / n_slots:
            queue = late if late and late[0][0] <= slot - 1 else early
            if not queue or queue[0][0] > slot - 1:
                break
            _, cost, fn = queue.pop(0)
            fn()
            done += cost
    for _, _, fn in early + late:
        fn()


def _mixer_ln(h2d, batch, seq, w_in, lb, gnorm, pool_w, pool_scale, w_out, g, b):
    ts = MIX_TILE
    assert seq % ts == 0 and ts % CHUNK == 0
    n_seq = seq // ts
    n_tiles = batch * n_seq
    tri, mask = _chunk_constants()
    next_tile = pl.BlockSpec((ts, D_MODEL), lambda i: (jnp.minimum(i, n_tiles - 1), 0))
    done_tile = pl.BlockSpec((ts, D_MODEL), lambda i: (jnp.maximum(i - 1, 0), 0))
    proj_set = [pltpu.VMEM((ts, D_HGRN), F32)] * 6 + [pltpu.VMEM((ts + POOL_HALO, D_POOL), F32)]
    return pl.pallas_call(
        functools.partial(_mixer_ln_kernel, n_seq=n_seq),
        grid=(n_tiles + 1,),
        in_specs=[
            next_tile,
            done_tile,
            _resident(w_in.shape),
            _resident(lb.shape),
            _resident(gnorm.shape),
            _resident(pool_w.shape),
            _resident(pool_scale.shape),
            _resident(w_out.shape),
            _resident(g.shape),
            _resident(b.shape),
            _resident(tri.shape),
            _resident(mask.shape),
        ],
        out_specs=done_tile,
        out_shape=jax.ShapeDtypeStruct(h2d.shape, F32),
        scratch_shapes=proj_set + proj_set + [
            pltpu.VMEM((ts, D_MODEL), BF16),
            pltpu.VMEM((ts, D_HGRN + D_POOL), BF16),
            pltpu.VMEM((ts, D_MODEL), F32),
            pltpu.VMEM((HGRN_HEADS, HEAD_DIM, HEAD_DIM), F32),
        ],
        compiler_params=pltpu.CompilerParams(
            dimension_semantics=("arbitrary",), vmem_limit_bytes=V7X_VMEM_LIMIT),
        name="mixer_ln",
    )(h2d, h2d, w_in, lb, gnorm, pool_w, pool_scale, w_out, g, b, tri, mask)


def _kv_proj_kernel(*refs, n_side):
    mem_ref, wk_ref, wv_ref = refs[:3]
    side_in = refs[3:3 + n_side]
    k_ref, v_ref = refs[3 + n_side:5 + n_side]
    side_out = refs[5 + n_side:5 + 2 * n_side]
    wk_b, wv_b = refs[-2:]

    @pl.when(pl.program_id(0) == 0)
    def _():
        wk_b[...] = wk_ref[...].astype(BF16)
        wv_b[...] = wv_ref[...].astype(BF16)

    mb = mem_ref[...].astype(BF16)
    k_ref[...] = _dot(mb, wk_b[...]).astype(BF16)
    v_ref[...] = _dot(mb, wv_b[...]).astype(BF16)
    for src, dst in zip(side_in, side_out):
        dst[...] = src[...].astype(BF16)


def _kv_proj(mem2d, n_mem, wk, wv, side=(), layer=0):
    rows = mem2d.shape[0]
    n_steps = rows // n_mem
    tile = pl.BlockSpec((n_mem, D_MODEL), lambda i: (i, 0))
    weight = pl.BlockSpec((None, D_MODEL, D_MODEL), lambda i: (layer, 0, 0), pipeline_mode=pl.Buffered(1))
    blocks = [_side_blocks(w, layer, n_steps) for w in side]
    out = pl.pallas_call(
        functools.partial(_kv_proj_kernel, n_side=len(side)),
        grid=(n_steps,),
        in_specs=[tile, weight, weight] + [blk[0] for blk in blocks],
        out_specs=[tile, tile] + [blk[1] for blk in blocks],
        out_shape=[jax.ShapeDtypeStruct((rows, D_MODEL), BF16)] * 2 + [blk[2] for blk in blocks],
        scratch_shapes=[pltpu.VMEM((D_MODEL, D_MODEL), BF16)] * 2,
        compiler_params=pltpu.CompilerParams(
            dimension_semantics=("arbitrary",), vmem_limit_bytes=V7X_VMEM_LIMIT),
        name="kv_proj",
    )(mem2d, wk, wv, *side)
    return out[0], out[1], out[2:]


def _xattn_ln_kernel(*refs, n_tiles, n_side):
    h_ref, k_ref, v_ref, wq_ref, wo_ref, g_ref, b_ref = refs[:7]
    side_in = refs[7:7 + n_side]
    o_ref = refs[7 + n_side]
    side_out = refs[8 + n_side:8 + 2 * n_side]
    y_s = refs[-1]
    step = pl.program_id(0)
    ts = h_ref.shape[0]
    ln_rows = ts // XA_LN_PIECES

    def ln_piece(p):
        rows = slice(p * ln_rows, (p + 1) * ln_rows)
        out = _layernorm(y_s[rows, :], g_ref[...], b_ref[...])
        o_ref[rows, :] = out
        return out

    @pl.when(step == 0)
    def _():
        y_s[...] = jnp.zeros_like(y_s)

    @pl.when(step < n_tiles)
    def _():
        h = h_ref[...]
        hb = h.astype(BF16)
        q = _dot(hb, wq_ref[...])
        pieces = list(range(XA_LN_PIECES))
        heads, raw = [], []
        for hd in range(XA_HEADS):
            cols = slice(hd * XA_HEAD_DIM, (hd + 1) * XA_HEAD_DIM)
            qh = q[:, cols].astype(BF16)
            if pieces:
                take = min(len(pieces), XA_LN_FIRST if hd == 0 else 1)
                zeros = [_zero_after(ln_piece(pieces.pop(0))) for _ in range(take)]
                zero = sum(zeros[1:], zeros[0]).astype(BF16)
                qh = qh + jnp.tile(zero, (ts // V7X_SUBLANES, XA_HEAD_DIM // V7X_LANES))
            raw.append(_dot_nt(qh, k_ref[:, cols]))
        for hd in range(XA_HEADS):
            cols = slice(hd * XA_HEAD_DIM, (hd + 1) * XA_HEAD_DIM)
            s = raw[hd] * (XA_HEAD_DIM ** -0.5)
            s = s - jnp.max(s, axis=-1, keepdims=True)
            p = jnp.exp(s)
            p = p / jnp.sum(p, axis=-1, keepdims=True)
            heads.append(_dot(p.astype(BF16), v_ref[:, cols]).astype(BF16))
        assert not pieces
        casts = list(range(n_side))
        y = None
        for hd in range(XA_HEADS):
            rows = slice(hd * XA_HEAD_DIM, (hd + 1) * XA_HEAD_DIM)
            part = _dot(heads[hd], wo_ref[rows, :])
            y = part if y is None else y + part
            if casts and hd < XA_HEADS - 1:
                n = casts.pop(0)
                v = side_in[n][...].astype(BF16)
                side_out[n][...] = v
                zero = _zero_after(v).astype(BF16)
                heads[hd + 1] = heads[hd + 1] + jnp.tile(zero, (ts // V7X_SUBLANES, XA_HEAD_DIM // V7X_LANES))
        assert not casts
        y_s[...] = ALPHA * h + y

    @pl.when(step == n_tiles)
    def _():
        for p in range(XA_LN_PIECES):
            ln_piece(p)


def _xattn_ln(h2d, batch, seq, k2d, v2d, n_mem, wq, wo, g, b, side=(), layer=0):
    ts = XA_TILE
    assert seq % ts == 0 and len(side) < XA_HEADS
    n_seq = seq // ts
    n_tiles = batch * n_seq
    attended = lambda i: jnp.minimum(i, n_tiles - 1)
    tile = pl.BlockSpec((ts, D_MODEL), lambda i: (attended(i), 0))
    mem_tile = pl.BlockSpec((n_mem, D_MODEL), lambda i: (attended(i) // n_seq, 0))
    blocks = [_side_blocks(w, layer, n_tiles) for w in side]
    out = pl.pallas_call(
        functools.partial(_xattn_ln_kernel, n_tiles=n_tiles, n_side=len(side)),
        grid=(n_tiles + 1,),
        in_specs=[tile, mem_tile, mem_tile, _resident(wq.shape), _resident(wo.shape),
                  _resident(g.shape), _resident(b.shape)] + [blk[0] for blk in blocks],
        out_specs=[pl.BlockSpec((ts, D_MODEL), lambda i: (jnp.maximum(i - 1, 0), 0))] + [blk[1] for blk in blocks],
        out_shape=[jax.ShapeDtypeStruct(h2d.shape, F32)] + [blk[2] for blk in blocks],
        scratch_shapes=[pltpu.VMEM((ts, D_MODEL), F32)],
        compiler_params=pltpu.CompilerParams(
            dimension_semantics=("arbitrary",), vmem_limit_bytes=V7X_VMEM_LIMIT),
        name="xattn_ln",
    )(h2d, k2d, v2d, wq, wo, g, b, *side)
    return out[0], out[1:]


def kernel(x, mem, w_ffn1_in, w_ffn1_out, ln1_g, ln1_b, w_mix_in, hgrn_lb, hgrn_gnorm, pool_w, pool_scale, w_mix_out, ln2_g, ln2_b, xa_wq, xa_wk, xa_wv, xa_wo, ln3_g, ln3_b, w_ffn2_in, w_ffn2_out, ln4_g, ln4_b):
    batch, seq, _ = x.shape
    n_mem = mem.shape[1]
    assert w_ffn1_in.shape[0] == DEPTH == 1
    h = x.reshape(batch * seq, D_MODEL)
    mem2d = mem.reshape(batch * n_mem, D_MODEL)
    for l in range(DEPTH):
        k2d, v2d, (ffn1_in, ffn1_out) = _kv_proj(mem2d, n_mem, xa_wk, xa_wv, side=(w_ffn1_in, w_ffn1_out), layer=l)
        later = (w_mix_in, pool_w.reshape(DEPTH, D_POOL, POOL_GROUP_DIM), w_mix_out, xa_wq, xa_wo)
        h, later = _ffn_ln(h, ffn1_in, ffn1_out, ln1_g[l:l + 1], ln1_b[l:l + 1], side=later, layer=l)
        mix_in, pool, mix_out, wq, wo = later
        pool = pool.reshape(len(POOL_WINDOWS), POOL_GROUP_DIM, POOL_GROUP_DIM)
        h = _mixer_ln(h, batch, seq, mix_in, hgrn_lb, hgrn_gnorm[l:l + 1], pool,
                      pool_scale[l:l + 1], mix_out, ln2_g[l:l + 1], ln2_b[l:l + 1])
        h, (ffn2_in, ffn2_out) = _xattn_ln(h, batch, seq, k2d, v2d, n_mem, wq, wo, ln3_g[l:l + 1], ln3_b[l:l + 1],
                                           side=(w_ffn2_in, w_ffn2_out), layer=l)
        h, _ = _ffn_ln(h, ffn2_in, ffn2_out, ln4_g[l:l + 1], ln4_b[l:l + 1])
    return h.reshape(batch, seq, D_MODEL)
```

```python
import functools

import numpy as np
import jax
import jax.numpy as jnp
from jax import lax
from jax.experimental import pallas as pl
from jax.experimental.pallas import tpu as pltpu

F32 = jnp.float32
BF16 = jnp.bfloat16

D_MODEL = 1024
DEPTH = 1
D_HGRN = 512
D_POOL = 512
HGRN_HEADS = 4
HEAD_DIM = 128
POOL_WINDOWS = (2, 4, 8, 16)
POOL_GROUP_DIM = 128
D_FF = 2816
D_IN_PROJ = 4 * D_HGRN + D_POOL
XA_HEADS = 4
XA_HEAD_DIM = 256
ALPHA = (2.0 * DEPTH) ** 0.25
LN_EPS = 1e-5
RMS_EPS = 1e-6

V7X_SUBLANES = 8
V7X_LANES = 128
V7X_MXU_DIM = 256
V7X_VMEM_LIMIT = 56 * 1024 * 1024

CHUNK = 64
SUB = V7X_SUBLANES
POOL_HALO = 16
FFN_TILE = 512
FFN_LN_PIECES = 4
MIX_TILE = 512
XA_TILE = 512
XA_LN_PIECES = 4
XA_LN_FIRST = 1
FF_CHUNKS = tuple((c0, min(512, D_FF - c0)) for c0 in range(0, D_FF, 512))


def _dot(a, b):
    return jnp.dot(a, b, preferred_element_type=F32)


def _dot_nt(a, b):
    return lax.dot_general(a, b, (((1,), (1,)), ((), ())), preferred_element_type=F32)


def _dot_tn(a, b):
    return lax.dot_general(a, b, (((0,), (0,)), ((), ())), preferred_element_type=F32)


def _silu(x):
    return x * jax.nn.sigmoid(x)


def _layernorm(y, g, b):
    mu = jnp.mean(y, axis=-1, keepdims=True)
    d = y - mu
    var = jnp.mean(d * d, axis=-1, keepdims=True)
    return d * lax.rsqrt(var + LN_EPS) * g + b


def _zero_after(v):
    sub = V7X_SUBLANES * (4 // v.dtype.itemsize)
    r, c = v.shape
    m = jnp.max(v.reshape(r // sub, sub, c), axis=0)
    m = functools.reduce(jnp.maximum, [m[:, j * V7X_LANES:(j + 1) * V7X_LANES] for j in range(c // V7X_LANES)])
    m = m.astype(F32)
    if sub > V7X_SUBLANES:
        m = jnp.maximum(m[:V7X_SUBLANES], m[V7X_SUBLANES:])
    bits = pltpu.bitcast(m, jnp.uint32)
    return pltpu.bitcast((bits >> 16) >> 16, F32)


def _resident(shape):
    zeros = (0,) * len(shape)
    return pl.BlockSpec(shape, lambda *_: zeros, pipeline_mode=pl.Buffered(1))


def _ffn_ln_kernel(*refs, n_tiles, n_side):
    x_ref, win_ref, wout_ref, g_ref, b_ref = refs[:5]
    side_in = refs[5:5 + n_side]
    o_ref = refs[5 + n_side]
    side_out = refs[6 + n_side:6 + 2 * n_side]
    y_s = refs[-1]
    step = pl.program_id(0)
    ln_rows = FFN_TILE // FFN_LN_PIECES
    gaps = len(FF_CHUNKS) - 1

    def ln_piece(p):
        rows = slice(p * ln_rows, (p + 1) * ln_rows)
        out = _layernorm(y_s[rows, :], g_ref[...], b_ref[...])
        o_ref[rows, :] = out
        return out

    def cast_side(n):
        v = side_in[n][...].astype(BF16)
        side_out[n][...] = v
        return v

    @pl.when(step == 0)
    def _():
        y_s[...] = jnp.zeros_like(y_s)

    @pl.when(step < n_tiles)
    def _():
        x = x_ref[...]
        xb = x.astype(BF16)
        acc = None
        jobs = [(FFN_TILE * D_MODEL // FFN_LN_PIECES * 3, functools.partial(ln_piece, p)) for p in range(FFN_LN_PIECES)]
        jobs += [(side_in[n].shape[0] * side_in[n].shape[1], functools.partial(cast_side, n)) for n in range(n_side)]
        bins = [[0, []] for _ in range(gaps)]
        for cost, job in sorted(jobs, key=lambda cj: -cj[0]):
            target = min(bins, key=lambda bn: bn[0])
            target[0] += cost
            target[1].append(job)
        lhs = xb
        for n, (c0, cw) in enumerate(FF_CHUNKS):
            gate = _dot(lhs, win_ref[:, c0:c0 + cw])
            up = _dot(lhs, win_ref[:, D_FF + c0:D_FF + c0 + cw])
            act = (_silu(gate) * up).astype(BF16)
            part = _dot(act, wout_ref[c0:c0 + cw, :])
            acc = part if acc is None else acc + part
            if n < gaps and bins[n][1]:
                zeros = [_zero_after(job()) for job in bins[n][1]]
                zero = sum(zeros[1:], zeros[0]).astype(BF16)
                lhs = xb + jnp.tile(zero, (FFN_TILE // V7X_SUBLANES, D_MODEL // V7X_LANES))
        y_s[...] = ALPHA * x + 0.5 * acc

    @pl.when(step == n_tiles)
    def _():
        for p in range(FFN_LN_PIECES):
            ln_piece(p)


def _side_blocks(w, layer, n_steps):
    _, r, c = w.shape
    packed_rows = 2 * V7X_SUBLANES
    rows = next(n for n in range(packed_rows, r + 1, packed_rows) if r % n == 0 and r // n <= n_steps)
    index = lambda i: (jnp.minimum(i, r // rows - 1), 0)
    return (pl.BlockSpec((None, rows, c), lambda i: (layer,) + index(i)), pl.BlockSpec((rows, c), index),
            jax.ShapeDtypeStruct((r, c), BF16))


def _ffn_ln(x2d, w_in, w_out, g, b, side=(), layer=0):
    m = x2d.shape[0]
    assert m % FFN_TILE == 0
    n_tiles = m // FFN_TILE
    current = lambda i: (jnp.minimum(i, n_tiles - 1), 0)
    blocks = [_side_blocks(w, layer, n_tiles) for w in side]
    out = pl.pallas_call(
        functools.partial(_ffn_ln_kernel, n_tiles=n_tiles, n_side=len(side)),
        grid=(n_tiles + 1,),
        in_specs=[
            pl.BlockSpec((FFN_TILE, D_MODEL), current),
            _resident((D_MODEL, 2 * D_FF)),
            _resident((D_FF, D_MODEL)),
            _resident((1, D_MODEL)),
            _resident((1, D_MODEL)),
        ] + [blk[0] for blk in blocks],
        out_specs=[pl.BlockSpec((FFN_TILE, D_MODEL), lambda i: (jnp.maximum(i - 1, 0), 0))]
        + [blk[1] for blk in blocks],
        out_shape=[jax.ShapeDtypeStruct((m, D_MODEL), F32)] + [blk[2] for blk in blocks],
        scratch_shapes=[pltpu.VMEM((FFN_TILE, D_MODEL), F32)],
        compiler_params=pltpu.CompilerParams(
            dimension_semantics=("arbitrary",), vmem_limit_bytes=V7X_VMEM_LIMIT),
        name="ffn_ln",
    )(x2d, w_in, w_out, g, b, *side)
    return out[0], out[1:]


N_BLK = CHUNK // SUB
CUM_GROUP = 4
FAR_ORDER = (7, 6, 3, 5, 4, 2, 1)
FAR_COLS = 256
NEAR_LEVELS = (0, 1, 2, 4)
assert sorted(FAR_ORDER) == list(range(1, N_BLK)) and SUB * sum(FAR_ORDER) <= FAR_COLS


def _chunk_constants():
    t = np.arange(CHUNK)[:, None]
    s = np.arange(CHUNK)[None, :]
    tri = np.kron(np.eye(CUM_GROUP), (s <= t)).astype(np.float32)
    far = np.zeros((CHUNK, FAR_COLS), np.float32)
    off = 0
    for i in FAR_ORDER:
        far[:, off:off + SUB * i] = (t // SUB == i)
        off += SUB * i
    near = np.stack([(t == s) if b == 0 else (((t // b) % 2 == 1) & ((s // b) == (t // b) - 1))
                     for b in NEAR_LEVELS]).astype(np.float32)
    return jnp.asarray(tri, BF16), jnp.asarray(far, BF16), jnp.asarray(near, F32)


def _split3(x):
    hi = x.astype(BF16)
    r = x - hi.astype(F32)
    mid = r.astype(BF16)
    lo = (r - mid.astype(F32)).astype(BF16)
    return hi, mid, lo


def _mixer_ln_kernel(hn_ref, hc_ref, win_ref, lb_ref, gn_ref, pw_ref, ps_ref, wout_ref, g_ref, b_ref,
                     tri_ref, mask_ref, near_mask_ref, o_ref,
                     q_a, k_a, f_a, i_a, cum_a, gate_a, vb_a,
                     q_s, k_s, f_s, i_s, cum_s, gate_s, vb_s, hb_s, merged_s, y_s, state_s, *, n_seq):
    ts = hn_ref.shape[0]
    step = pl.program_id(0)
    seq_tile_next = step % n_seq
    seq_tile = (step + n_seq - 1) % n_seq
    handoff = ((q_a, q_s), (k_a, k_s), (f_a, f_s), (i_a, i_s), (cum_a, cum_s), (gate_a, gate_s), (vb_a, vb_s))

    @pl.when(step == 0)
    def _():
        state_s[...] = jnp.zeros_like(state_s)
        for src, _ in handoff:
            src[...] = jnp.zeros_like(src)

    col_blk = V7X_MXU_DIM

    def copy_rows(c):
        rows = slice(c * CHUNK, (c + 1) * CHUNK)
        for src, dst in handoff[:-1]:
            dst[rows, :] = src[rows, :]
        if c == 0:
            vb_s[0:POOL_HALO, :] = vb_a[0:POOL_HALO, :]
        prow = slice(POOL_HALO + c * CHUNK, POOL_HALO + (c + 1) * CHUNK)
        vb_s[prow, :] = vb_a[prow, :]

    def pool_halo():
        vb_a[0:POOL_HALO, :] = jnp.where(seq_tile_next == 0, 0.0, vb_s[ts:ts + POOL_HALO, :])

    a = lb_ref[...]
    e = jnp.exp(a - jnp.max(a, axis=0, keepdims=True))
    lower = e[0:1, :] / jnp.sum(e, axis=0, keepdims=True)

    def cast_next():
        hb_s[...] = hn_ref[...].astype(BF16)

    def in_proj(j):
        x = _dot(hb_s[...], win_ref[:, j * col_blk:(j + 1) * col_blk])
        kind, cb = divmod(j * col_blk, D_HGRN)
        cols = slice(cb, cb + col_blk)
        if kind == 0:
            q_a[:, cols] = _silu(x)
        elif kind == 1:
            forget = lower[:, cols] + (1.0 - lower[:, cols]) * jax.nn.sigmoid(x)
            f_a[:, cols] = forget
            k_a[:, cols] = 1.0 - forget
            cum_a[:, cols] = jnp.log(forget)
        elif kind == 2:
            i_a[:, cols] = x
        elif kind == 3:
            gate_a[:, cols] = _silu(x)
        else:
            vb_a[POOL_HALO:POOL_HALO + ts, cols] = x

    def cum_chunks(g):
        rows = slice(g * CUM_GROUP * CHUNK, (g + 1) * CUM_GROUP * CHUNK)
        tri = tri_ref[...]
        hi, mid, lo = _split3(cum_a[rows, :])
        cum_a[rows, :] = _dot(tri, hi) + _dot(tri, mid) + _dot(tri, lo)

    sub_row = lax.broadcasted_iota(jnp.int32, (SUB, HEAD_DIM), 0)
    zero_blk = jnp.zeros((SUB, HEAD_DIM), F32)
    far_pad = jnp.zeros((FAR_COLS - SUB * sum(FAR_ORDER), HEAD_DIM), F32)
    gn = gn_ref[...]
    def head_cols(hd):
        return slice(hd * HEAD_DIM, (hd + 1) * HEAD_DIM)

    def scores_stage(c, hd):
        r0 = c * CHUNK
        rows = pl.ds(r0, CHUNK)
        cols = head_cols(hd)
        q = q_s[rows, cols]
        k = k_s[rows, cols]
        f = f_s[rows, cols]
        v = i_s[rows, cols]
        cum = cum_s[rows, cols]

        def cum_row(row):
            return jnp.broadcast_to(cum_s[pl.ds(r0 + row, 1), cols], (SUB, HEAD_DIM))

        bound = [zero_blk] + [cum_row(SUB * j - 1) for j in range(1, N_BLK + 1)]

        q_far = (q * jnp.exp(cum - jnp.concatenate(bound[:N_BLK], axis=0))).astype(BF16)
        k_parts, v_parts = [], []
        for i in FAR_ORDER:
            n = SUB * i
            k_parts.append(k[:n] * jnp.exp(jnp.concatenate([bound[i]] * i, axis=0) - cum[:n]))
            v_parts.append(v[:n])
        k_far = jnp.concatenate(k_parts + [far_pad], axis=0).astype(BF16)
        s_far = _dot_nt(q_far, k_far)

        ref_q2, ref_k2, ref_q4, ref_k4 = [], [], [], []
        for j in range(N_BLK):
            lo_b, hi_b = bound[j], bound[j + 1]
            m1, m3, m5 = (cum_row(SUB * j + r) for r in (1, 3, 5))
            ref_q4.append(jnp.where(sub_row < 4, lo_b, m3))
            ref_k4.append(jnp.where(sub_row < 4, m3, hi_b))
            ref_q2.append(jnp.where(sub_row < 2, lo_b, jnp.where(sub_row < 4, m1, jnp.where(sub_row < 6, m3, m5))))
            ref_k2.append(jnp.where(sub_row < 2, m1, jnp.where(sub_row < 4, m3, jnp.where(sub_row < 6, m5, hi_b))))
        cat = lambda parts: jnp.concatenate(parts, axis=0)
        q2 = q * jnp.exp(cum - cat(ref_q2))
        k2 = k * jnp.exp(cat(ref_k2) - cum)
        q4 = q * jnp.exp(cum - cat(ref_q4))
        k4 = k * jnp.exp(cat(ref_k4) - cum)
        kb = k.astype(BF16)
        s_01 = _dot_nt(jnp.concatenate([q, q * f], axis=0).astype(BF16), kb)
        groups = (s_01[:CHUNK], s_01[CHUNK:], _dot_nt(q2.astype(BF16), k2.astype(BF16)),
                  _dot_nt(q4.astype(BF16), k4.astype(BF16)))
        s_near = sum(g * near_mask_ref[n] for n, g in enumerate(groups))

        total = bound[N_BLK]
        vb = v.astype(BF16)
        return dict(
            p_far=s_far.astype(BF16) * mask_ref[...],
            v_far=jnp.concatenate(v_parts + [far_pad], axis=0).astype(BF16),
            p_near=s_near.astype(BF16),
            q_dec=(q * jnp.exp(cum)).astype(BF16),
            k_dec=(k * jnp.exp(jnp.concatenate([total] * N_BLK, axis=0) - cum)).astype(BF16),
            vb=vb,
            decay=jnp.exp(total[0:1, :]))

    def state_stage(c, hd, sc):
        st = state_s[hd]
        if c == 0:
            st = jnp.where(seq_tile == 0, 0.0, st)
        o = (_dot(sc["p_far"], sc["v_far"]) + _dot(sc["p_near"], sc["vb"])
             + _dot_nt(sc["q_dec"], st.astype(BF16)))
        state_s[hd] = st * sc["decay"] + _dot_tn(sc["vb"], sc["k_dec"])
        return o

    def norm_stage(c, hd, o):
        rows = pl.ds(c * CHUNK, CHUNK)
        cols = head_cols(hd)
        o = o * lax.rsqrt(jnp.mean(o * o, axis=-1, keepdims=True) + RMS_EPS) * gn
        merged_s[rows, cols] = (o * gate_s[rows, cols]).astype(BF16)

    quarter = ts // 4

    def pool_rows(r0, nr):
        pos = seq_tile * ts + r0 + lax.broadcasted_iota(jnp.int32, (nr, 1), 0) + 1
        for grp, win in enumerate(POOL_WINDOWS):
            cols = slice(grp * POOL_GROUP_DIM, (grp + 1) * POOL_GROUP_DIM)
            ext = vb_s[r0:r0 + POOL_HALO + nr, cols]
            wsum = ext
            span = 1
            while span < win:
                wsum = wsum + pltpu.roll(wsum, span, 0)
                span *= 2
            cur = ext[POOL_HALO:]
            count = jnp.minimum(pos, win).astype(F32)
            pooled = wsum[POOL_HALO:] / count - cur
            mixed = _dot(pooled.astype(BF16), pw_ref[grp]) * ps_ref[:, cols]
            merged_s[r0:r0 + nr, D_HGRN + grp * POOL_GROUP_DIM:D_HGRN + (grp + 1) * POOL_GROUP_DIM] = (
                mixed.astype(BF16))

    def out_proj(r0, nr, n2):
        rows = slice(r0, r0 + nr)
        cols = slice(n2 * (D_MODEL // 2), (n2 + 1) * (D_MODEL // 2))
        y_s[rows, cols] = _dot(merged_s[rows, :], wout_ref[:, cols])

    def out_ln(qd):
        rows = slice(qd * quarter, (qd + 1) * quarter)
        o_ref[rows, :] = _layernorm(ALPHA * hc_ref[rows, :] + y_s[rows, :], g_ref[...], b_ref[...])

    n_chunks = ts // CHUNK
    n_slots = n_chunks * HGRN_HEADS
    P = functools.partial
    COPY, PROJ, CUM, POOL, OUT, LN = 100, 256, 192, 200, 256, 150
    early = [(0, COPY, P(copy_rows, c)) for c in range(2, n_chunks)] + [(0, 10, pool_halo)]
    blocks = list(range(D_IN_PROJ // col_blk))
    forget_blocks = [j for j in blocks if j * col_blk // D_HGRN == 1]
    first = 4
    early += [(first, COPY, cast_next)]
    early += [(first, 2 * PROJ, P(in_proj, j)) for j in forget_blocks]
    early += [(first, 2 * CUM, P(cum_chunks, g)) for g in range(n_chunks // CUM_GROUP)]
    early += [(first, 2 * PROJ, P(in_proj, j)) for j in blocks if j not in forget_blocks]
    late = []
    for r0, nr in ((0, 3 * quarter), (3 * quarter, quarter)):
        first = (r0 + nr) // CHUNK * HGRN_HEADS
        late += [(first, POOL * nr // quarter, P(pool_rows, r0, nr))]
        late += [(first, OUT * nr // quarter, P(out_proj, r0, nr, n2)) for n2 in range(2)]
        late += [(first, LN, P(out_ln, qd)) for qd in range(r0 // quarter, (r0 + nr) // quarter)]
    total = sum(cost for _, cost, _ in early + late)
    copy_rows(0)
    copy_rows(1)
    done = 0
    ch = lambda slot: (slot // HGRN_HEADS, slot % HGRN_HEADS)
    scores, outs = {}, {}
    for slot in range(n_slots + 2):
        if slot < n_slots:
            scores[slot] = scores_stage(*ch(slot))
        if 1 <= slot <= n_slots:
            outs[slot - 1] = state_stage(*ch(slot - 1), scores.pop(slot - 1))
        if slot >= 2:
            norm_stage(*ch(slot - 2), outs.pop(slot - 2))
        while done < total * (slot + 1) / n_slots:
            queue = late if late and late[0][0] <= slot - 1 else early
            if not queue or queue[0][0] > slot - 1:
                break
            _, cost, fn = queue.pop(0)
            fn()
            done += cost
    for _, _, fn in early + late:
        fn()


def _mixer_ln(h2d, batch, seq, w_in, lb, gnorm, pool_w, pool_scale, w_out, g, b):
    ts = MIX_TILE
    assert seq % ts == 0 and ts % CHUNK == 0
    n_seq = seq // ts
    n_tiles = batch * n_seq
    tri, mask, near_mask = _chunk_constants()
    next_tile = pl.BlockSpec((ts, D_MODEL), lambda i: (jnp.minimum(i, n_tiles - 1), 0))
    done_tile = pl.BlockSpec((ts, D_MODEL), lambda i: (jnp.maximum(i - 1, 0), 0))
    proj_set = [pltpu.VMEM((ts, D_HGRN), F32)] * 6 + [pltpu.VMEM((ts + POOL_HALO, D_POOL), F32)]
    return pl.pallas_call(
        functools.partial(_mixer_ln_kernel, n_seq=n_seq),
        grid=(n_tiles + 1,),
        in_specs=[
            next_tile,
            done_tile,
            _resident(w_in.shape),
            _resident(lb.shape),
            _resident(gnorm.shape),
            _resident(pool_w.shape),
            _resident(pool_scale.shape),
            _resident(w_out.shape),
            _resident(g.shape),
            _resident(b.shape),
            _resident(tri.shape),
            _resident(mask.shape),
            _resident(near_mask.shape),
        ],
        out_specs=done_tile,
        out_shape=jax.ShapeDtypeStruct(h2d.shape, F32),
        scratch_shapes=proj_set + proj_set + [
            pltpu.VMEM((ts, D_MODEL), BF16),
            pltpu.VMEM((ts, D_HGRN + D_POOL), BF16),
            pltpu.VMEM((ts, D_MODEL), F32),
            pltpu.VMEM((HGRN_HEADS, HEAD_DIM, HEAD_DIM), F32),
        ],
        compiler_params=pltpu.CompilerParams(
            dimension_semantics=("arbitrary",), vmem_limit_bytes=V7X_VMEM_LIMIT),
        name="mixer_ln",
    )(h2d, h2d, w_in, lb, gnorm, pool_w, pool_scale, w_out, g, b, tri, mask, near_mask)


def _kv_proj_kernel(*refs, n_side):
    mem_ref, wk_ref, wv_ref = refs[:3]
    side_in = refs[3:3 + n_side]
    k_ref, v_ref = refs[3 + n_side:5 + n_side]
    side_out = refs[5 + n_side:5 + 2 * n_side]
    wk_b, wv_b = refs[-2:]

    @pl.when(pl.program_id(0) == 0)
    def _():
        wk_b[...] = wk_ref[...].astype(BF16)
        wv_b[...] = wv_ref[...].astype(BF16)

    mb = mem_ref[...].astype(BF16)
    k_ref[...] = _dot(mb, wk_b[...]).astype(BF16)
    v_ref[...] = _dot(mb, wv_b[...]).astype(BF16)
    for src, dst in zip(side_in, side_out):
        dst[...] = src[...].astype(BF16)


def _kv_proj(mem2d, n_mem, wk, wv, side=(), layer=0):
    rows = mem2d.shape[0]
    n_steps = rows // n_mem
    tile = pl.BlockSpec((n_mem, D_MODEL), lambda i: (i, 0))
    weight = pl.BlockSpec((None, D_MODEL, D_MODEL), lambda i: (layer, 0, 0), pipeline_mode=pl.Buffered(1))
    blocks = [_side_blocks(w, layer, n_steps) for w in side]
    out = pl.pallas_call(
        functools.partial(_kv_proj_kernel, n_side=len(side)),
        grid=(n_steps,),
        in_specs=[tile, weight, weight] + [blk[0] for blk in blocks],
        out_specs=[tile, tile] + [blk[1] for blk in blocks],
        out_shape=[jax.ShapeDtypeStruct((rows, D_MODEL), BF16)] * 2 + [blk[2] for blk in blocks],
        scratch_shapes=[pltpu.VMEM((D_MODEL, D_MODEL), BF16)] * 2,
        compiler_params=pltpu.CompilerParams(
            dimension_semantics=("arbitrary",), vmem_limit_bytes=V7X_VMEM_LIMIT),
        name="kv_proj",
    )(mem2d, wk, wv, *side)
    return out[0], out[1], out[2:]


def _xattn_ln_kernel(*refs, n_tiles, n_side):
    h_ref, k_ref, v_ref, wq_ref, wo_ref, g_ref, b_ref = refs[:7]
    side_in = refs[7:7 + n_side]
    o_ref = refs[7 + n_side]
    side_out = refs[8 + n_side:8 + 2 * n_side]
    y_s = refs[-1]
    step = pl.program_id(0)
    ts = h_ref.shape[0]
    ln_rows = ts // XA_LN_PIECES

    def ln_piece(p):
        rows = slice(p * ln_rows, (p + 1) * ln_rows)
        out = _layernorm(y_s[rows, :], g_ref[...], b_ref[...])
        o_ref[rows, :] = out
        return out

    @pl.when(step == 0)
    def _():
        y_s[...] = jnp.zeros_like(y_s)

    @pl.when(step < n_tiles)
    def _():
        h = h_ref[...]
        hb = h.astype(BF16)
        q = _dot(hb, wq_ref[...])
        pieces = list(range(XA_LN_PIECES))
        heads, raw = [], []
        for hd in range(XA_HEADS):
            cols = slice(hd * XA_HEAD_DIM, (hd + 1) * XA_HEAD_DIM)
            qh = q[:, cols].astype(BF16)
            if pieces:
                take = min(len(pieces), XA_LN_FIRST if hd == 0 else 1)
                zeros = [_zero_after(ln_piece(pieces.pop(0))) for _ in range(take)]
                zero = sum(zeros[1:], zeros[0]).astype(BF16)
                qh = qh + jnp.tile(zero, (ts // V7X_SUBLANES, XA_HEAD_DIM // V7X_LANES))
            raw.append(_dot_nt(qh, k_ref[:, cols]))
        for hd in range(XA_HEADS):
            cols = slice(hd * XA_HEAD_DIM, (hd + 1) * XA_HEAD_DIM)
            s = raw[hd] * (XA_HEAD_DIM ** -0.5)
            s = s - jnp.max(s, axis=-1, keepdims=True)
            p = jnp.exp(s)
            p = p / jnp.sum(p, axis=-1, keepdims=True)
            heads.append(_dot(p.astype(BF16), v_ref[:, cols]).astype(BF16))
        assert not pieces
        casts = list(range(n_side))
        y = None
        for hd in range(XA_HEADS):
            rows = slice(hd * XA_HEAD_DIM, (hd + 1) * XA_HEAD_DIM)
            part = _dot(heads[hd], wo_ref[rows, :])
            y = part if y is None else y + part
            if casts and hd < XA_HEADS - 1:
                n = casts.pop(0)
                v = side_in[n][...].astype(BF16)
                side_out[n][...] = v
                zero = _zero_after(v).astype(BF16)
                heads[hd + 1] = heads[hd + 1] + jnp.tile(zero, (ts // V7X_SUBLANES, XA_HEAD_DIM // V7X_LANES))
        assert not casts
        y_s[...] = ALPHA * h + y

    @pl.when(step == n_tiles)
    def _():
        for p in range(XA_LN_PIECES):
            ln_piece(p)


def _xattn_ln(h2d, batch, seq, k2d, v2d, n_mem, wq, wo, g, b, side=(), layer=0):
    ts = XA_TILE
    assert seq % ts == 0 and len(side) < XA_HEADS
    n_seq = seq // ts
    n_tiles = batch * n_seq
    attended = lambda i: jnp.minimum(i, n_tiles - 1)
    tile = pl.BlockSpec((ts, D_MODEL), lambda i: (attended(i), 0))
    mem_tile = pl.BlockSpec((n_mem, D_MODEL), lambda i: (attended(i) // n_seq, 0))
    blocks = [_side_blocks(w, layer, n_tiles) for w in side]
    out = pl.pallas_call(
        functools.partial(_xattn_ln_kernel, n_tiles=n_tiles, n_side=len(side)),
        grid=(n_tiles + 1,),
        in_specs=[tile, mem_tile, mem_tile, _resident(wq.shape), _resident(wo.shape),
                  _resident(g.shape), _resident(b.shape)] + [blk[0] for blk in blocks],
        out_specs=[pl.BlockSpec((ts, D_MODEL), lambda i: (jnp.maximum(i - 1, 0), 0))] + [blk[1] for blk in blocks],
        out_shape=[jax.ShapeDtypeStruct(h2d.shape, F32)] + [blk[2] for blk in blocks],
        scratch_shapes=[pltpu.VMEM((ts, D_MODEL), F32)],
        compiler_params=pltpu.CompilerParams(
            dimension_semantics=("arbitrary",), vmem_limit_bytes=V7X_VMEM_LIMIT),
        name="xattn_ln",
    )(h2d, k2d, v2d, wq, wo, g, b, *side)
    return out[0], out[1:]


def kernel(x, mem, w_ffn1_in, w_ffn1_out, ln1_g, ln1_b, w_mix_in, hgrn_lb, hgrn_gnorm, pool_w, pool_scale, w_mix_out, ln2_g, ln2_b, xa_wq, xa_wk, xa_wv, xa_wo, ln3_g, ln3_b, w_ffn2_in, w_ffn2_out, ln4_g, ln4_b):
    batch, seq, _ = x.shape
    n_mem = mem.shape[1]
    assert w_ffn1_in.shape[0] == DEPTH == 1
    h = x.reshape(batch * seq, D_MODEL)
    mem2d = mem.reshape(batch * n_mem, D_MODEL)
    for l in range(DEPTH):
        k2d, v2d, (ffn1_in, ffn1_out) = _kv_proj(mem2d, n_mem, xa_wk, xa_wv, side=(w_ffn1_in, w_ffn1_out), layer=l)
        later = (w_mix_in, pool_w.reshape(DEPTH, D_POOL, POOL_GROUP_DIM), w_mix_out, xa_wq, xa_wo)
        h, later = _ffn_ln(h, ffn1_in, ffn1_out, ln1_g[l:l + 1], ln1_b[l:l + 1], side=later, layer=l)
        mix_in, pool, mix_out, wq, wo = later
        pool = pool.reshape(len(POOL_WINDOWS), POOL_GROUP_DIM, POOL_GROUP_DIM)
        h = _mixer_ln(h, batch, seq, mix_in, hgrn_lb, hgrn_gnorm[l:l + 1], pool,
                      pool_scale[l:l + 1], mix_out, ln2_g[l:l + 1], ln2_b[l:l + 1])
        h, (ffn2_in, ffn2_out) = _xattn_ln(h, batch, seq, k2d, v2d, n_mem, wq, wo, ln3_g[l:l + 1], ln3_b[l:l + 1],
                                           side=(w_ffn2_in, w_ffn2_out), layer=l)
        h, _ = _ffn_ln(h, ffn2_in, ffn2_out, ln4_g[l:l + 1], ln4_b[l:l + 1])
    return h.reshape(batch, seq, D_MODEL)
```

```python
import functools

import numpy as np
import jax
import jax.numpy as jnp
from jax import lax
from jax.experimental import pallas as pl
from jax.experimental.pallas import tpu as pltpu

F32 = jnp.float32
BF16 = jnp.bfloat16

D_MODEL = 1024
DEPTH = 1
D_HGRN = 512
D_POOL = 512
HGRN_HEADS = 4
HEAD_DIM = 128
POOL_WINDOWS = (2, 4, 8, 16)
POOL_GROUP_DIM = 128
D_FF = 2816
D_IN_PROJ = 4 * D_HGRN + D_POOL
XA_HEADS = 4
XA_HEAD_DIM = 256
ALPHA = (2.0 * DEPTH) ** 0.25
LN_EPS = 1e-5
RMS_EPS = 1e-6

V7X_SUBLANES = 8
V7X_LANES = 128
V7X_MXU_DIM = 256
V7X_VMEM_LIMIT = 56 * 1024 * 1024

CHUNK = 64
SUB = V7X_SUBLANES
POOL_HALO = 16
FFN_TILE = 512
FFN_LN_PIECES = 4
MIX_TILE = 512
XA_TILE = 512
XA_LN_PIECES = 4
XA_LN_FIRST = 1
FF_CHUNKS = tuple((c0, min(512, D_FF - c0)) for c0 in range(0, D_FF, 512))


def _dot(a, b):
    return jnp.dot(a, b, preferred_element_type=F32)


def _dot_nt(a, b):
    return lax.dot_general(a, b, (((1,), (1,)), ((), ())), preferred_element_type=F32)


def _dot_tn(a, b):
    return lax.dot_general(a, b, (((0,), (0,)), ((), ())), preferred_element_type=F32)


def _silu(x):
    return x * jax.nn.sigmoid(x)


def _layernorm(y, g, b):
    mu = jnp.mean(y, axis=-1, keepdims=True)
    d = y - mu
    var = jnp.mean(d * d, axis=-1, keepdims=True)
    return d * lax.rsqrt(var + LN_EPS) * g + b


def _zero_after(v):
    sub = V7X_SUBLANES * (4 // v.dtype.itemsize)
    r, c = v.shape
    m = jnp.max(v.reshape(r // sub, sub, c), axis=0)
    m = functools.reduce(jnp.maximum, [m[:, j * V7X_LANES:(j + 1) * V7X_LANES] for j in range(c // V7X_LANES)])
    m = m.astype(F32)
    if sub > V7X_SUBLANES:
        m = jnp.maximum(m[:V7X_SUBLANES], m[V7X_SUBLANES:])
    bits = pltpu.bitcast(m, jnp.uint32)
    return pltpu.bitcast((bits >> 16) >> 16, F32)


def _resident(shape):
    zeros = (0,) * len(shape)
    return pl.BlockSpec(shape, lambda *_: zeros, pipeline_mode=pl.Buffered(1))


def _ffn_ln_kernel(*refs, n_tiles, n_side):
    x_ref, win_ref, wout_ref, g_ref, b_ref = refs[:5]
    side_in = refs[5:5 + n_side]
    o_ref = refs[5 + n_side]
    side_out = refs[6 + n_side:6 + 2 * n_side]
    y_s = refs[-1]
    step = pl.program_id(0)
    ln_rows = FFN_TILE // FFN_LN_PIECES
    gaps = len(FF_CHUNKS) - 1

    def ln_piece(p):
        rows = slice(p * ln_rows, (p + 1) * ln_rows)
        out = _layernorm(y_s[rows, :], g_ref[...], b_ref[...])
        o_ref[rows, :] = out
        return out

    def cast_side(n):
        v = side_in[n][...].astype(BF16)
        side_out[n][...] = v
        return v

    @pl.when(step == 0)
    def _():
        y_s[...] = jnp.zeros_like(y_s)

    @pl.when(step < n_tiles)
    def _():
        x = x_ref[...]
        xb = x.astype(BF16)
        acc = None
        jobs = [(FFN_TILE * D_MODEL // FFN_LN_PIECES * 3, functools.partial(ln_piece, p)) for p in range(FFN_LN_PIECES)]
        jobs += [(side_in[n].shape[0] * side_in[n].shape[1], functools.partial(cast_side, n)) for n in range(n_side)]
        bins = [[0, []] for _ in range(gaps)]
        for cost, job in sorted(jobs, key=lambda cj: -cj[0]):
            target = min(bins, key=lambda bn: bn[0])
            target[0] += cost
            target[1].append(job)
        lhs = xb
        for n, (c0, cw) in enumerate(FF_CHUNKS):
            gate = _dot(lhs, win_ref[:, c0:c0 + cw])
            up = _dot(lhs, win_ref[:, D_FF + c0:D_FF + c0 + cw])
            act = (_silu(gate) * up).astype(BF16)
            part = _dot(act, wout_ref[c0:c0 + cw, :])
            acc = part if acc is None else acc + part
            if n < gaps and bins[n][1]:
                zeros = [_zero_after(job()) for job in bins[n][1]]
                zero = sum(zeros[1:], zeros[0]).astype(BF16)
                lhs = xb + jnp.tile(zero, (FFN_TILE // V7X_SUBLANES, D_MODEL // V7X_LANES))
        y_s[...] = ALPHA * x + 0.5 * acc

    @pl.when(step == n_tiles)
    def _():
        for p in range(FFN_LN_PIECES):
            ln_piece(p)


def _side_blocks(w, layer, n_steps):
    _, r, c = w.shape
    packed_rows = 2 * V7X_SUBLANES
    rows = next(n for n in range(packed_rows, r + 1, packed_rows) if r % n == 0 and r // n <= n_steps)
    index = lambda i: (jnp.minimum(i, r // rows - 1), 0)
    return (pl.BlockSpec((None, rows, c), lambda i: (layer,) + index(i)), pl.BlockSpec((rows, c), index),
            jax.ShapeDtypeStruct((r, c), BF16))


def _ffn_ln(x2d, w_in, w_out, g, b, side=(), layer=0):
    m = x2d.shape[0]
    assert m % FFN_TILE == 0
    n_tiles = m // FFN_TILE
    current = lambda i: (jnp.minimum(i, n_tiles - 1), 0)
    blocks = [_side_blocks(w, layer, n_tiles) for w in side]
    out = pl.pallas_call(
        functools.partial(_ffn_ln_kernel, n_tiles=n_tiles, n_side=len(side)),
        grid=(n_tiles + 1,),
        in_specs=[
            pl.BlockSpec((FFN_TILE, D_MODEL), current),
            _resident((D_MODEL, 2 * D_FF)),
            _resident((D_FF, D_MODEL)),
            _resident((1, D_MODEL)),
            _resident((1, D_MODEL)),
        ] + [blk[0] for blk in blocks],
        out_specs=[pl.BlockSpec((FFN_TILE, D_MODEL), lambda i: (jnp.maximum(i - 1, 0), 0))]
        + [blk[1] for blk in blocks],
        out_shape=[jax.ShapeDtypeStruct((m, D_MODEL), F32)] + [blk[2] for blk in blocks],
        scratch_shapes=[pltpu.VMEM((FFN_TILE, D_MODEL), F32)],
        compiler_params=pltpu.CompilerParams(
            dimension_semantics=("arbitrary",), vmem_limit_bytes=V7X_VMEM_LIMIT),
        name="ffn_ln",
    )(x2d, w_in, w_out, g, b, *side)
    return out[0], out[1:]


N_BLK = CHUNK // SUB
CUM_GROUP = 4
FAR_ORDER = (7, 6, 3, 5, 4, 2, 1)
FAR_COLS = 256
NEAR_LEVELS = (0, 1, 2, 4)
assert sorted(FAR_ORDER) == list(range(1, N_BLK)) and SUB * sum(FAR_ORDER) <= FAR_COLS


def _chunk_constants():
    t = np.arange(CHUNK)[:, None]
    s = np.arange(CHUNK)[None, :]
    tri = np.kron(np.eye(CUM_GROUP), (s <= t)).astype(np.float32)
    far = np.zeros((CHUNK, FAR_COLS), np.float32)
    off = 0
    for i in FAR_ORDER:
        far[:, off:off + SUB * i] = (t // SUB == i)
        off += SUB * i
    near = np.stack([(t == s) if b == 0 else (((t // b) % 2 == 1) & ((s // b) == (t // b) - 1))
                     for b in NEAR_LEVELS]).astype(np.float32)
    return jnp.asarray(tri, BF16), jnp.asarray(far, BF16), jnp.asarray(near, F32)


def _split3(x):
    hi = x.astype(BF16)
    r = x - hi.astype(F32)
    mid = r.astype(BF16)
    lo = (r - mid.astype(F32)).astype(BF16)
    return hi, mid, lo


def _mixer_ln_kernel(hn_ref, hc_ref, win_ref, lb_ref, gn_ref, pw_ref, ps_ref, wout_ref, g_ref, b_ref,
                     tri_ref, mask_ref, near_mask_ref, o_ref,
                     q_a, k_a, f_a, i_a, cum_a, gate_a, vb_a,
                     q_s, k_s, f_s, i_s, cum_s, gate_s, vb_s, hb_s, merged_s, y_s, state_s, *, n_seq):
    ts = hn_ref.shape[0]
    step = pl.program_id(0)
    seq_tile_next = step % n_seq
    seq_tile = (step + n_seq - 1) % n_seq
    handoff = ((q_a, q_s), (k_a, k_s), (f_a, f_s), (i_a, i_s), (cum_a, cum_s), (gate_a, gate_s), (vb_a, vb_s))

    @pl.when(step == 0)
    def _():
        state_s[...] = jnp.zeros_like(state_s)
        for src, _ in handoff:
            src[...] = jnp.zeros_like(src)

    col_blk = D_HGRN

    def copy_rows(c):
        rows = slice(c * CHUNK, (c + 1) * CHUNK)
        for src, dst in handoff[:-1]:
            dst[rows, :] = src[rows, :]
        if c == 0:
            vb_s[0:POOL_HALO, :] = vb_a[0:POOL_HALO, :]
        prow = slice(POOL_HALO + c * CHUNK, POOL_HALO + (c + 1) * CHUNK)
        vb_s[prow, :] = vb_a[prow, :]

    def pool_halo():
        vb_a[0:POOL_HALO, :] = jnp.where(seq_tile_next == 0, 0.0, vb_s[ts:ts + POOL_HALO, :])

    a = lb_ref[...]
    e = jnp.exp(a - jnp.max(a, axis=0, keepdims=True))
    lower = e[0:1, :] / jnp.sum(e, axis=0, keepdims=True)

    def cast_next():
        hb_s[...] = hn_ref[...].astype(BF16)

    def in_proj(j):
        x = _dot(hb_s[...], win_ref[:, j * col_blk:(j + 1) * col_blk])
        kind, cb = divmod(j * col_blk, D_HGRN)
        cols = slice(cb, cb + col_blk)
        if kind == 0:
            q_a[:, cols] = _silu(x)
        elif kind == 1:
            forget = lower[:, cols] + (1.0 - lower[:, cols]) * jax.nn.sigmoid(x)
            f_a[:, cols] = forget
            k_a[:, cols] = 1.0 - forget
            cum_a[:, cols] = jnp.log(forget)
        elif kind == 2:
            i_a[:, cols] = x
        elif kind == 3:
            gate_a[:, cols] = _silu(x)
        else:
            vb_a[POOL_HALO:POOL_HALO + ts, cols] = x

    def cum_chunks(g):
        rows = slice(g * CUM_GROUP * CHUNK, (g + 1) * CUM_GROUP * CHUNK)
        tri = tri_ref[...]
        hi, mid, lo = _split3(cum_a[rows, :])
        cum_a[rows, :] = _dot(tri, hi) + _dot(tri, mid) + _dot(tri, lo)

    sub_row = lax.broadcasted_iota(jnp.int32, (SUB, HEAD_DIM), 0)
    zero_blk = jnp.zeros((SUB, HEAD_DIM), F32)
    far_pad = jnp.zeros((FAR_COLS - SUB * sum(FAR_ORDER), HEAD_DIM), F32)
    gn = gn_ref[...]
    def head_cols(hd):
        return slice(hd * HEAD_DIM, (hd + 1) * HEAD_DIM)

    def scores_stage(c, hd):
        r0 = c * CHUNK
        rows = pl.ds(r0, CHUNK)
        cols = head_cols(hd)
        q = q_s[rows, cols]
        k = k_s[rows, cols]
        f = f_s[rows, cols]
        v = i_s[rows, cols]
        cum = cum_s[rows, cols]

        def cum_row(row):
            return jnp.broadcast_to(cum_s[pl.ds(r0 + row, 1), cols], (SUB, HEAD_DIM))

        bound = [zero_blk] + [cum_row(SUB * j - 1) for j in range(1, N_BLK + 1)]

        q_far = (q * jnp.exp(cum - jnp.concatenate(bound[:N_BLK], axis=0))).astype(BF16)
        k_parts, v_parts = [], []
        for i in FAR_ORDER:
            n = SUB * i
            k_parts.append(k[:n] * jnp.exp(jnp.concatenate([bound[i]] * i, axis=0) - cum[:n]))
            v_parts.append(v[:n])
        k_far = jnp.concatenate(k_parts + [far_pad], axis=0).astype(BF16)
        s_far = _dot_nt(q_far, k_far)

        ref_q2, ref_k2, ref_q4, ref_k4 = [], [], [], []
        for j in range(N_BLK):
            lo_b, hi_b = bound[j], bound[j + 1]
            m1, m3, m5 = (cum_row(SUB * j + r) for r in (1, 3, 5))
            ref_q4.append(jnp.where(sub_row < 4, lo_b, m3))
            ref_k4.append(jnp.where(sub_row < 4, m3, hi_b))
            ref_q2.append(jnp.where(sub_row < 2, lo_b, jnp.where(sub_row < 4, m1, jnp.where(sub_row < 6, m3, m5))))
            ref_k2.append(jnp.where(sub_row < 2, m1, jnp.where(sub_row < 4, m3, jnp.where(sub_row < 6, m5, hi_b))))
        cat = lambda parts: jnp.concatenate(parts, axis=0)
        q2 = q * jnp.exp(cum - cat(ref_q2))
        k2 = k * jnp.exp(cat(ref_k2) - cum)
        q4 = q * jnp.exp(cum - cat(ref_q4))
        k4 = k * jnp.exp(cat(ref_k4) - cum)
        kb = k.astype(BF16)
        s_01 = _dot_nt(jnp.concatenate([q, q * f], axis=0).astype(BF16), kb)
        groups = (s_01[:CHUNK], s_01[CHUNK:], _dot_nt(q2.astype(BF16), k2.astype(BF16)),
                  _dot_nt(q4.astype(BF16), k4.astype(BF16)))
        s_near = sum(g * near_mask_ref[n] for n, g in enumerate(groups))

        total = bound[N_BLK]
        vb = v.astype(BF16)
        return dict(
            p_far=s_far.astype(BF16) * mask_ref[...],
            v_far=jnp.concatenate(v_parts + [far_pad], axis=0).astype(BF16),
            p_near=s_near.astype(BF16),
            q_dec=(q * jnp.exp(cum)).astype(BF16),
            k_dec=(k * jnp.exp(jnp.concatenate([total] * N_BLK, axis=0) - cum)).astype(BF16),
            vb=vb,
            decay=jnp.exp(total[0:1, :]))

    def state_stage(c, hd, sc):
        st = state_s[hd]
        if c == 0:
            st = jnp.where(seq_tile == 0, 0.0, st)
        o = (_dot(sc["p_far"], sc["v_far"]) + _dot(sc["p_near"], sc["vb"])
             + _dot_nt(sc["q_dec"], st.astype(BF16)))
        state_s[hd] = st * sc["decay"] + _dot_tn(sc["vb"], sc["k_dec"])
        return o

    def norm_stage(c, hd, o):
        rows = pl.ds(c * CHUNK, CHUNK)
        cols = head_cols(hd)
        o = o * lax.rsqrt(jnp.mean(o * o, axis=-1, keepdims=True) + RMS_EPS) * gn
        merged_s[rows, cols] = (o * gate_s[rows, cols]).astype(BF16)

    quarter = ts // 4

    def pool_rows(r0, nr):
        pos = seq_tile * ts + r0 + lax.broadcasted_iota(jnp.int32, (nr, 1), 0) + 1
        for grp, win in enumerate(POOL_WINDOWS):
            cols = slice(grp * POOL_GROUP_DIM, (grp + 1) * POOL_GROUP_DIM)
            ext = vb_s[r0:r0 + POOL_HALO + nr, cols]
            wsum = ext
            span = 1
            while span < win:
                wsum = wsum + pltpu.roll(wsum, span, 0)
                span *= 2
            cur = ext[POOL_HALO:]
            count = jnp.minimum(pos, win).astype(F32)
            pooled = wsum[POOL_HALO:] / count - cur
            mixed = _dot(pooled.astype(BF16), pw_ref[grp]) * ps_ref[:, cols]
            merged_s[r0:r0 + nr, D_HGRN + grp * POOL_GROUP_DIM:D_HGRN + (grp + 1) * POOL_GROUP_DIM] = (
                mixed.astype(BF16))

    def out_proj(r0, nr, n2):
        rows = slice(r0, r0 + nr)
        cols = slice(n2 * (D_MODEL // 2), (n2 + 1) * (D_MODEL // 2))
        y_s[rows, cols] = _dot(merged_s[rows, :], wout_ref[:, cols])

    def out_ln(qd):
        rows = slice(qd * quarter, (qd + 1) * quarter)
        o_ref[rows, :] = _layernorm(ALPHA * hc_ref[rows, :] + y_s[rows, :], g_ref[...], b_ref[...])

    n_chunks = ts // CHUNK
    n_slots = n_chunks * HGRN_HEADS
    P = functools.partial
    COPY, PROJ, CUM, POOL, OUT, LN = 100, 256, 192, 200, 256, 150
    early = [(0, COPY, P(copy_rows, c)) for c in range(2, n_chunks)] + [(0, 10, pool_halo)]
    blocks = list(range(D_IN_PROJ // col_blk))
    forget_blocks = [j for j in blocks if j * col_blk // D_HGRN == 1]
    first = 4
    early += [(first, COPY, cast_next)]
    early += [(first, 4 * PROJ, P(in_proj, j)) for j in forget_blocks]
    early += [(first, 2 * CUM, P(cum_chunks, g)) for g in range(n_chunks // CUM_GROUP)]
    early += [(first, 4 * PROJ, P(in_proj, j)) for j in blocks if j not in forget_blocks]
    late = []
    for r0, nr in ((0, 3 * quarter), (3 * quarter, quarter)):
        first = (r0 + nr) // CHUNK * HGRN_HEADS
        late += [(first, POOL * nr // quarter, P(pool_rows, r0, nr))]
        late += [(first, OUT * nr // quarter, P(out_proj, r0, nr, n2)) for n2 in range(2)]
        late += [(first, LN, P(out_ln, qd)) for qd in range(r0 // quarter, (r0 + nr) // quarter)]
    total = sum(cost for _, cost, _ in early + late)
    copy_rows(0)
    copy_rows(1)
    done = 0
    ch = lambda slot: (slot // HGRN_HEADS, slot % HGRN_HEADS)
    scores, outs = {}, {}
    for slot in range(n_slots + 2):
        if slot < n_slots:
            scores[slot] = scores_stage(*ch(slot))
        if 1 <= slot <= n_slots:
            outs[slot - 1] = state_stage(*ch(slot - 1), scores.pop(slot - 1))
        if slot >= 2:
            norm_stage(*ch(slot - 2), outs.pop(slot - 2))
        while done < total * (slot + 1) / n_slots:
            queue = late if late and late[0][0] <= slot - 1 else early
            if not queue or queue[0][0] > slot - 1:
                break
            _, cost, fn = queue.pop(0)
            fn()
            done += cost
    for _, _, fn in early + late:
        fn()


def _mixer_ln(h2d, batch, seq, w_in, lb, gnorm, pool_w, pool_scale, w_out, g, b):
    ts = MIX_TILE
    assert seq % ts == 0 and ts % CHUNK == 0
    n_seq = seq // ts
    n_tiles = batch * n_seq
    tri, mask, near_mask = _chunk_constants()
    next_tile = pl.BlockSpec((ts, D_MODEL), lambda i: (jnp.minimum(i, n_tiles - 1), 0))
    done_tile = pl.BlockSpec((ts, D_MODEL), lambda i: (jnp.maximum(i - 1, 0), 0))
    proj_set = [pltpu.VMEM((ts, D_HGRN), F32)] * 6 + [pltpu.VMEM((ts + POOL_HALO, D_POOL), F32)]
    return pl.pallas_call(
        functools.partial(_mixer_ln_kernel, n_seq=n_seq),
        grid=(n_tiles + 1,),
        in_specs=[
            next_tile,
            done_tile,
            _resident(w_in.shape),
            _resident(lb.shape),
            _resident(gnorm.shape),
            _resident(pool_w.shape),
            _resident(pool_scale.shape),
            _resident(w_out.shape),
            _resident(g.shape),
            _resident(b.shape),
            _resident(tri.shape),
            _resident(mask.shape),
            _resident(near_mask.shape),
        ],
        out_specs=done_tile,
        out_shape=jax.ShapeDtypeStruct(h2d.shape, F32),
        scratch_shapes=proj_set + proj_set + [
            pltpu.VMEM((ts, D_MODEL), BF16),
            pltpu.VMEM((ts, D_HGRN + D_POOL), BF16),
            pltpu.VMEM((ts, D_MODEL), F32),
            pltpu.VMEM((HGRN_HEADS, HEAD_DIM, HEAD_DIM), F32),
        ],
        compiler_params=pltpu.CompilerParams(
            dimension_semantics=("arbitrary",), vmem_limit_bytes=V7X_VMEM_LIMIT),
        name="mixer_ln",
    )(h2d, h2d, w_in, lb, gnorm, pool_w, pool_scale, w_out, g, b, tri, mask, near_mask)


def _kv_proj_kernel(*refs, n_side):
    mem_ref, wk_ref, wv_ref = refs[:3]
    side_in = refs[3:3 + n_side]
    k_ref, v_ref = refs[3 + n_side:5 + n_side]
    side_out = refs[5 + n_side:5 + 2 * n_side]
    wk_b, wv_b = refs[-2:]

    @pl.when(pl.program_id(0) == 0)
    def _():
        wk_b[...] = wk_ref[...].astype(BF16)
        wv_b[...] = wv_ref[...].astype(BF16)

    mb = mem_ref[...].astype(BF16)
    k_ref[...] = _dot(mb, wk_b[...]).astype(BF16)
    v_ref[...] = _dot(mb, wv_b[...]).astype(BF16)
    for src, dst in zip(side_in, side_out):
        dst[...] = src[...].astype(BF16)


def _kv_proj(mem2d, n_mem, wk, wv, side=(), layer=0):
    rows = mem2d.shape[0]
    n_steps = rows // n_mem
    tile = pl.BlockSpec((n_mem, D_MODEL), lambda i: (i, 0))
    weight = pl.BlockSpec((None, D_MODEL, D_MODEL), lambda i: (layer, 0, 0), pipeline_mode=pl.Buffered(1))
    blocks = [_side_blocks(w, layer, n_steps) for w in side]
    out = pl.pallas_call(
        functools.partial(_kv_proj_kernel, n_side=len(side)),
        grid=(n_steps,),
        in_specs=[tile, weight, weight] + [blk[0] for blk in blocks],
        out_specs=[tile, tile] + [blk[1] for blk in blocks],
        out_shape=[jax.ShapeDtypeStruct((rows, D_MODEL), BF16)] * 2 + [blk[2] for blk in blocks],
        scratch_shapes=[pltpu.VMEM((D_MODEL, D_MODEL), BF16)] * 2,
        compiler_params=pltpu.CompilerParams(
            dimension_semantics=("arbitrary",), vmem_limit_bytes=V7X_VMEM_LIMIT),
        name="kv_proj",
    )(mem2d, wk, wv, *side)
    return out[0], out[1], out[2:]


def _xattn_ln_kernel(*refs, n_tiles, n_side):
    h_ref, k_ref, v_ref, wq_ref, wo_ref, g_ref, b_ref = refs[:7]
    side_in = refs[7:7 + n_side]
    o_ref = refs[7 + n_side]
    side_out = refs[8 + n_side:8 + 2 * n_side]
    y_s = refs[-1]
    step = pl.program_id(0)
    ts = h_ref.shape[0]
    ln_rows = ts // XA_LN_PIECES

    def ln_piece(p):
        rows = slice(p * ln_rows, (p + 1) * ln_rows)
        out = _layernorm(y_s[rows, :], g_ref[...], b_ref[...])
        o_ref[rows, :] = out
        return out

    @pl.when(step == 0)
    def _():
        y_s[...] = jnp.zeros_like(y_s)

    @pl.when(step < n_tiles)
    def _():
        h = h_ref[...]
        hb = h.astype(BF16)
        q = _dot(hb, wq_ref[...])
        pieces = list(range(XA_LN_PIECES))
        heads, raw = [], []
        for hd in range(XA_HEADS):
            cols = slice(hd * XA_HEAD_DIM, (hd + 1) * XA_HEAD_DIM)
            qh = q[:, cols].astype(BF16)
            if pieces:
                take = min(len(pieces), XA_LN_FIRST if hd == 0 else 1)
                zeros = [_zero_after(ln_piece(pieces.pop(0))) for _ in range(take)]
                zero = sum(zeros[1:], zeros[0]).astype(BF16)
                qh = qh + jnp.tile(zero, (ts // V7X_SUBLANES, XA_HEAD_DIM // V7X_LANES))
            raw.append(_dot_nt(qh, k_ref[:, cols]))
        for hd in range(XA_HEADS):
            cols = slice(hd * XA_HEAD_DIM, (hd + 1) * XA_HEAD_DIM)
            s = raw[hd] * (XA_HEAD_DIM ** -0.5)
            s = s - jnp.max(s, axis=-1, keepdims=True)
            p = jnp.exp(s)
            p = p / jnp.sum(p, axis=-1, keepdims=True)
            heads.append(_dot(p.astype(BF16), v_ref[:, cols]).astype(BF16))
        assert not pieces
        casts = list(range(n_side))
        y = None
        for hd in range(XA_HEADS):
            rows = slice(hd * XA_HEAD_DIM, (hd + 1) * XA_HEAD_DIM)
            part = _dot(heads[hd], wo_ref[rows, :])
            y = part if y is None else y + part
            if casts and hd < XA_HEADS - 1:
                n = casts.pop(0)
                v = side_in[n][...].astype(BF16)
                side_out[n][...] = v
                zero = _zero_after(v).astype(BF16)
                heads[hd + 1] = heads[hd + 1] + jnp.tile(zero, (ts // V7X_SUBLANES, XA_HEAD_DIM // V7X_LANES))
        assert not casts
        y_s[...] = ALPHA * h + y

    @pl.when(step == n_tiles)
    def _():
        for p in range(XA_LN_PIECES):
            ln_piece(p)


def _xattn_ln(h2d, batch, seq, k2d, v2d, n_mem, wq, wo, g, b, side=(), layer=0):
    ts = XA_TILE
    assert seq % ts == 0 and len(side) < XA_HEADS
    n_seq = seq // ts
    n_tiles = batch * n_seq
    attended = lambda i: jnp.minimum(i, n_tiles - 1)
    tile = pl.BlockSpec((ts, D_MODEL), lambda i: (attended(i), 0))
    mem_tile = pl.BlockSpec((n_mem, D_MODEL), lambda i: (attended(i) // n_seq, 0))
    blocks = [_side_blocks(w, layer, n_tiles) for w in side]
    out = pl.pallas_call(
        functools.partial(_xattn_ln_kernel, n_tiles=n_tiles, n_side=len(side)),
        grid=(n_tiles + 1,),
        in_specs=[tile, mem_tile, mem_tile, _resident(wq.shape), _resident(wo.shape),
                  _resident(g.shape), _resident(b.shape)] + [blk[0] for blk in blocks],
        out_specs=[pl.BlockSpec((ts, D_MODEL), lambda i: (jnp.maximum(i - 1, 0), 0))] + [blk[1] for blk in blocks],
        out_shape=[jax.ShapeDtypeStruct(h2d.shape, F32)] + [blk[2] for blk in blocks],
        scratch_shapes=[pltpu.VMEM((ts, D_MODEL), F32)],
        compiler_params=pltpu.CompilerParams(
            dimension_semantics=("arbitrary",), vmem_limit_bytes=V7X_VMEM_LIMIT),
        name="xattn_ln",
    )(h2d, k2d, v2d, wq, wo, g, b, *side)
    return out[0], out[1:]


def kernel(x, mem, w_ffn1_in, w_ffn1_out, ln1_g, ln1_b, w_mix_in, hgrn_lb, hgrn_gnorm, pool_w, pool_scale, w_mix_out, ln2_g, ln2_b, xa_wq, xa_wk, xa_wv, xa_wo, ln3_g, ln3_b, w_ffn2_in, w_ffn2_out, ln4_g, ln4_b):
    batch, seq, _ = x.shape
    n_mem = mem.shape[1]
    assert w_ffn1_in.shape[0] == DEPTH == 1
    h = x.reshape(batch * seq, D_MODEL)
    mem2d = mem.reshape(batch * n_mem, D_MODEL)
    for l in range(DEPTH):
        k2d, v2d, (ffn1_in, ffn1_out) = _kv_proj(mem2d, n_mem, xa_wk, xa_wv, side=(w_ffn1_in, w_ffn1_out), layer=l)
        later = (w_mix_in, pool_w.reshape(DEPTH, D_POOL, POOL_GROUP_DIM), w_mix_out, xa_wq, xa_wo)
        h, later = _ffn_ln(h, ffn1_in, ffn1_out, ln1_g[l:l + 1], ln1_b[l:l + 1], side=later, layer=l)
        mix_in, pool, mix_out, wq, wo = later
        pool = pool.reshape(len(POOL_WINDOWS), POOL_GROUP_DIM, POOL_GROUP_DIM)
        h = _mixer_ln(h, batch, seq, mix_in, hgrn_lb, hgrn_gnorm[l:l + 1], pool,
                      pool_scale[l:l + 1], mix_out, ln2_g[l:l + 1], ln2_b[l:l + 1])
        h, (ffn2_in, ffn2_out) = _xattn_ln(h, batch, seq, k2d, v2d, n_mem, wq, wo, ln3_g[l:l + 1], ln3_b[l:l + 1],
                                           side=(w_ffn2_in, w_ffn2_out), layer=l)
        h, _ = _ffn_ln(h, ffn2_in, ffn2_out, ln4_g[l:l + 1], ln4_b[l:l + 1])
    return h.reshape(batch, seq, D_MODEL)
```

```python
import functools

import numpy as np
import jax
import jax.numpy as jnp
from jax import lax
from jax.experimental import pallas as pl
from jax.experimental.pallas import tpu as pltpu

F32 = jnp.float32
BF16 = jnp.bfloat16

D_MODEL = 1024
DEPTH = 1
D_HGRN = 512
D_POOL = 512
HGRN_HEADS = 4
HEAD_DIM = 128
POOL_WINDOWS = (2, 4, 8, 16)
POOL_GROUP_DIM = 128
D_FF = 2816
D_IN_PROJ = 4 * D_HGRN + D_POOL
XA_HEADS = 4
XA_HEAD_DIM = 256
ALPHA = (2.0 * DEPTH) ** 0.25
LN_EPS = 1e-5
RMS_EPS = 1e-6

V7X_SUBLANES = 8
V7X_LANES = 128
V7X_MXU_DIM = 256
V7X_VMEM_LIMIT = 56 * 1024 * 1024

CHUNK = 64
SUB = V7X_SUBLANES
POOL_HALO = 16
FFN_TILE = 512
FFN_LN_PIECES = 4
MIX_TILE = 512
XA_TILE = 512
XA_LN_PIECES = 4
XA_LN_FIRST = 1
FF_CHUNKS = tuple((c0, min(512, D_FF - c0)) for c0 in range(0, D_FF, 512))


def _dot(a, b):
    return jnp.dot(a, b, preferred_element_type=F32)


def _dot_nt(a, b):
    return lax.dot_general(a, b, (((1,), (1,)), ((), ())), preferred_element_type=F32)


def _dot_tn(a, b):
    return lax.dot_general(a, b, (((0,), (0,)), ((), ())), preferred_element_type=F32)


def _silu(x):
    return x * jax.nn.sigmoid(x)


def _layernorm(y, g, b):
    mu = jnp.mean(y, axis=-1, keepdims=True)
    d = y - mu
    var = jnp.mean(d * d, axis=-1, keepdims=True)
    return d * lax.rsqrt(var + LN_EPS) * g + b


def _zero_after(v):
    sub = V7X_SUBLANES * (4 // v.dtype.itemsize)
    r, c = v.shape
    m = jnp.max(v.reshape(r // sub, sub, c), axis=0)
    m = functools.reduce(jnp.maximum, [m[:, j * V7X_LANES:(j + 1) * V7X_LANES] for j in range(c // V7X_LANES)])
    m = m.astype(F32)
    if sub > V7X_SUBLANES:
        m = jnp.maximum(m[:V7X_SUBLANES], m[V7X_SUBLANES:])
    bits = pltpu.bitcast(m, jnp.uint32)
    return pltpu.bitcast((bits >> 16) >> 16, F32)


def _resident(shape):
    zeros = (0,) * len(shape)
    return pl.BlockSpec(shape, lambda *_: zeros, pipeline_mode=pl.Buffered(1))


def _ffn_ln_kernel(*refs, n_tiles, n_side):
    x_ref, win_ref, wout_ref, g_ref, b_ref = refs[:5]
    side_in = refs[5:5 + n_side]
    o_ref = refs[5 + n_side]
    side_out = refs[6 + n_side:6 + 2 * n_side]
    y_s = refs[-1]
    step = pl.program_id(0)
    ln_rows = FFN_TILE // FFN_LN_PIECES
    gaps = len(FF_CHUNKS) - 1

    def ln_piece(p):
        rows = slice(p * ln_rows, (p + 1) * ln_rows)
        out = _layernorm(y_s[rows, :], g_ref[...], b_ref[...])
        o_ref[rows, :] = out
        return out

    def cast_side(n):
        v = side_in[n][...].astype(BF16)
        side_out[n][...] = v
        return v

    @pl.when(step == 0)
    def _():
        y_s[...] = jnp.zeros_like(y_s)

    @pl.when(step < n_tiles)
    def _():
        x = x_ref[...]
        xb = x.astype(BF16)
        acc = None
        jobs = [(FFN_TILE * D_MODEL // FFN_LN_PIECES * 3, functools.partial(ln_piece, p)) for p in range(FFN_LN_PIECES)]
        jobs += [(side_in[n].shape[0] * side_in[n].shape[1], functools.partial(cast_side, n)) for n in range(n_side)]
        bins = [[0, []] for _ in range(gaps)]
        for cost, job in sorted(jobs, key=lambda cj: -cj[0]):
            target = min(bins, key=lambda bn: bn[0])
            target[0] += cost
            target[1].append(job)
        lhs = xb
        for n, (c0, cw) in enumerate(FF_CHUNKS):
            gate = _dot(lhs, win_ref[:, c0:c0 + cw])
            up = _dot(lhs, win_ref[:, D_FF + c0:D_FF + c0 + cw])
            act = (_silu(gate) * up).astype(BF16)
            part = _dot(act, wout_ref[c0:c0 + cw, :])
            acc = part if acc is None else acc + part
            if n < gaps and bins[n][1]:
                zeros = [_zero_after(job()) for job in bins[n][1]]
                zero = sum(zeros[1:], zeros[0]).astype(BF16)
                lhs = xb + jnp.tile(zero, (FFN_TILE // V7X_SUBLANES, D_MODEL // V7X_LANES))
        y_s[...] = ALPHA * x + 0.5 * acc

    @pl.when(step == n_tiles)
    def _():
        for p in range(FFN_LN_PIECES):
            ln_piece(p)


def _side_blocks(w, layer, n_steps):
    _, r, c = w.shape
    packed_rows = 2 * V7X_SUBLANES
    rows = next(n for n in range(packed_rows, r + 1, packed_rows) if r % n == 0 and r // n <= n_steps)
    index = lambda i: (jnp.minimum(i, r // rows - 1), 0)
    return (pl.BlockSpec((None, rows, c), lambda i: (layer,) + index(i)), pl.BlockSpec((rows, c), index),
            jax.ShapeDtypeStruct((r, c), BF16))


def _ffn_ln(x2d, w_in, w_out, g, b, side=(), layer=0):
    m = x2d.shape[0]
    assert m % FFN_TILE == 0
    n_tiles = m // FFN_TILE
    current = lambda i: (jnp.minimum(i, n_tiles - 1), 0)
    blocks = [_side_blocks(w, layer, n_tiles) for w in side]
    out = pl.pallas_call(
        functools.partial(_ffn_ln_kernel, n_tiles=n_tiles, n_side=len(side)),
        grid=(n_tiles + 1,),
        in_specs=[
            pl.BlockSpec((FFN_TILE, D_MODEL), current),
            _resident((D_MODEL, 2 * D_FF)),
            _resident((D_FF, D_MODEL)),
            _resident((1, D_MODEL)),
            _resident((1, D_MODEL)),
        ] + [blk[0] for blk in blocks],
        out_specs=[pl.BlockSpec((FFN_TILE, D_MODEL), lambda i: (jnp.maximum(i - 1, 0), 0))]
        + [blk[1] for blk in blocks],
        out_shape=[jax.ShapeDtypeStruct((m, D_MODEL), F32)] + [blk[2] for blk in blocks],
        scratch_shapes=[pltpu.VMEM((FFN_TILE, D_MODEL), F32)],
        compiler_params=pltpu.CompilerParams(
            dimension_semantics=("arbitrary",), vmem_limit_bytes=V7X_VMEM_LIMIT),
        name="ffn_ln",
    )(x2d, w_in, w_out, g, b, *side)
    return out[0], out[1:]


N_BLK = CHUNK // SUB
CUM_GROUP = 4
FAR_ORDER = (7, 6, 3, 5, 4, 2, 1)
FAR_COLS = 256
NEAR_LEVELS = (0, 1, 2, 4)
assert sorted(FAR_ORDER) == list(range(1, N_BLK)) and SUB * sum(FAR_ORDER) <= FAR_COLS


def _chunk_constants():
    t = np.arange(CHUNK)[:, None]
    s = np.arange(CHUNK)[None, :]
    tri = np.kron(np.eye(CUM_GROUP), (s <= t)).astype(np.float32)
    far = np.zeros((CHUNK, FAR_COLS), np.float32)
    off = 0
    for i in FAR_ORDER:
        far[:, off:off + SUB * i] = (t // SUB == i)
        off += SUB * i
    near = np.stack([(t == s) if b == 0 else (((t // b) % 2 == 1) & ((s // b) == (t // b) - 1))
                     for b in NEAR_LEVELS]).astype(np.float32)
    return jnp.asarray(tri, BF16), jnp.asarray(far, BF16), jnp.asarray(near, F32)


def _split3(x):
    hi = x.astype(BF16)
    r = x - hi.astype(F32)
    mid = r.astype(BF16)
    lo = (r - mid.astype(F32)).astype(BF16)
    return hi, mid, lo


def _mixer_ln_kernel(hn_ref, hc_ref, win_ref, lb_ref, gn_ref, pw_ref, ps_ref, wout_ref, g_ref, b_ref,
                     tri_ref, mask_ref, near_mask_ref, o_ref,
                     q_a, k_a, f_a, i_a, cum_a, gate_a, vb_a,
                     q_s, k_s, f_s, i_s, cum_s, gate_s, vb_s, hb_s, merged_s, y_s, state_s, *, n_seq):
    ts = hn_ref.shape[0]
    step = pl.program_id(0)
    seq_tile_next = step % n_seq
    seq_tile = (step + n_seq - 1) % n_seq
    handoff = ((q_a, q_s), (k_a, k_s), (f_a, f_s), (i_a, i_s), (cum_a, cum_s), (gate_a, gate_s), (vb_a, vb_s))

    @pl.when(step == 0)
    def _():
        state_s[...] = jnp.zeros_like(state_s)
        for src, _ in handoff:
            src[...] = jnp.zeros_like(src)

    col_blk = V7X_MXU_DIM

    def copy_rows(c):
        rows = slice(c * CHUNK, (c + 1) * CHUNK)
        for src, dst in handoff[:-1]:
            dst[rows, :] = src[rows, :]
        if c == 0:
            vb_s[0:POOL_HALO, :] = vb_a[0:POOL_HALO, :]
        prow = slice(POOL_HALO + c * CHUNK, POOL_HALO + (c + 1) * CHUNK)
        vb_s[prow, :] = vb_a[prow, :]

    def pool_halo():
        vb_a[0:POOL_HALO, :] = jnp.where(seq_tile_next == 0, 0.0, vb_s[ts:ts + POOL_HALO, :])

    a = lb_ref[...]
    e = jnp.exp(a - jnp.max(a, axis=0, keepdims=True))
    lower = e[0:1, :] / jnp.sum(e, axis=0, keepdims=True)

    def cast_next():
        hb_s[...] = hn_ref[...].astype(BF16)

    def in_proj(j):
        x = _dot(hb_s[...], win_ref[:, j * col_blk:(j + 1) * col_blk])
        kind, cb = divmod(j * col_blk, D_HGRN)
        cols = slice(cb, cb + col_blk)
        if kind == 0:
            q_a[:, cols] = _silu(x)
        elif kind == 1:
            forget = lower[:, cols] + (1.0 - lower[:, cols]) * jax.nn.sigmoid(x)
            f_a[:, cols] = forget
            k_a[:, cols] = 1.0 - forget
            cum_a[:, cols] = jnp.log(forget)
        elif kind == 2:
            i_a[:, cols] = x
        elif kind == 3:
            gate_a[:, cols] = _silu(x)
        else:
            vb_a[POOL_HALO:POOL_HALO + ts, cols] = x

    def cum_chunks(g):
        rows = slice(g * CUM_GROUP * CHUNK, (g + 1) * CUM_GROUP * CHUNK)
        tri = tri_ref[...]
        hi, mid, lo = _split3(cum_a[rows, :])
        cum_a[rows, :] = _dot(tri, hi) + _dot(tri, mid) + _dot(tri, lo)

    sub_row = lax.broadcasted_iota(jnp.int32, (SUB, HEAD_DIM), 0)
    zero_blk = jnp.zeros((SUB, HEAD_DIM), F32)
    far_pad = jnp.zeros((FAR_COLS - SUB * sum(FAR_ORDER), HEAD_DIM), F32)
    gn = gn_ref[...]

    def head_cols(hd):
        return slice(hd * HEAD_DIM, (hd + 1) * HEAD_DIM)

    def scores_stage(c, hd):
        r0 = c * CHUNK
        rows = pl.ds(r0, CHUNK)
        cols = head_cols(hd)
        q = q_s[rows, cols]
        k = k_s[rows, cols]
        f = f_s[rows, cols]
        v = i_s[rows, cols]
        cum = cum_s[rows, cols]

        def cum_row(row):
            return jnp.broadcast_to(cum_s[pl.ds(r0 + row, 1), cols], (SUB, HEAD_DIM))

        bound = [zero_blk] + [cum_row(SUB * j - 1) for j in range(1, N_BLK + 1)]

        q_far = (q * jnp.exp(cum - jnp.concatenate(bound[:N_BLK], axis=0))).astype(BF16)
        k_parts, v_parts = [], []
        for i in FAR_ORDER:
            n = SUB * i
            k_parts.append(k[:n] * jnp.exp(jnp.concatenate([bound[i]] * i, axis=0) - cum[:n]))
            v_parts.append(v[:n])
        k_far = jnp.concatenate(k_parts + [far_pad], axis=0).astype(BF16)
        s_far = _dot_nt(q_far, k_far)

        ref_q2, ref_k2, ref_q4, ref_k4 = [], [], [], []
        for j in range(N_BLK):
            lo_b, hi_b = bound[j], bound[j + 1]
            m1, m3, m5 = (cum_row(SUB * j + r) for r in (1, 3, 5))
            ref_q4.append(jnp.where(sub_row < 4, lo_b, m3))
            ref_k4.append(jnp.where(sub_row < 4, m3, hi_b))
            ref_q2.append(jnp.where(sub_row < 2, lo_b, jnp.where(sub_row < 4, m1, jnp.where(sub_row < 6, m3, m5))))
            ref_k2.append(jnp.where(sub_row < 2, m1, jnp.where(sub_row < 4, m3, jnp.where(sub_row < 6, m5, hi_b))))
        cat = lambda parts: jnp.concatenate(parts, axis=0)
        q2 = q * jnp.exp(cum - cat(ref_q2))
        k2 = k * jnp.exp(cat(ref_k2) - cum)
        q4 = q * jnp.exp(cum - cat(ref_q4))
        k4 = k * jnp.exp(cat(ref_k4) - cum)
        kb = k.astype(BF16)
        s_01 = _dot_nt(jnp.concatenate([q, q * f], axis=0).astype(BF16), kb)
        groups = (s_01[:CHUNK], s_01[CHUNK:], _dot_nt(q2.astype(BF16), k2.astype(BF16)),
                  _dot_nt(q4.astype(BF16), k4.astype(BF16)))
        s_near = sum(g * near_mask_ref[n] for n, g in enumerate(groups))

        total = bound[N_BLK]
        vb = v.astype(BF16)
        return dict(
            p_far=s_far.astype(BF16) * mask_ref[...],
            v_far=jnp.concatenate(v_parts + [far_pad], axis=0).astype(BF16),
            p_near=s_near.astype(BF16),
            q_dec=(q * jnp.exp(cum)).astype(BF16),
            k_dec=(k * jnp.exp(jnp.concatenate([total] * N_BLK, axis=0) - cum)).astype(BF16),
            vb=vb,
            decay=jnp.exp(total[0:1, :]))

    def state_stage(c, hd, sc):
        st = state_s[hd]
        if c == 0:
            st = jnp.where(seq_tile == 0, 0.0, st)
        o = (_dot(sc["p_far"], sc["v_far"]) + _dot(sc["p_near"], sc["vb"])
             + _dot_nt(sc["q_dec"], st.astype(BF16)))
        state_s[hd] = st * sc["decay"] + _dot_tn(sc["vb"], sc["k_dec"])
        return o

    def norm_stage(c, hd, o):
        rows = pl.ds(c * CHUNK, CHUNK)
        cols = head_cols(hd)
        o = o * lax.rsqrt(jnp.mean(o * o, axis=-1, keepdims=True) + RMS_EPS) * gn
        merged_s[rows, cols] = (o * gate_s[rows, cols]).astype(BF16)

    quarter = ts // 4

    def pool_rows(r0, nr):
        pos = seq_tile * ts + r0 + lax.broadcasted_iota(jnp.int32, (nr, 1), 0) + 1
        for grp, win in enumerate(POOL_WINDOWS):
            cols = slice(grp * POOL_GROUP_DIM, (grp + 1) * POOL_GROUP_DIM)
            ext = vb_s[r0:r0 + POOL_HALO + nr, cols]
            wsum = ext
            span = 1
            while span < win:
                wsum = wsum + pltpu.roll(wsum, span, 0)
                span *= 2
            cur = ext[POOL_HALO:]
            count = jnp.minimum(pos, win).astype(F32)
            pooled = wsum[POOL_HALO:] / count - cur
            mixed = _dot(pooled.astype(BF16), pw_ref[grp]) * ps_ref[:, cols]
            merged_s[r0:r0 + nr, D_HGRN + grp * POOL_GROUP_DIM:D_HGRN + (grp + 1) * POOL_GROUP_DIM] = (
                mixed.astype(BF16))

    def out_proj(r0, nr, n2):
        rows = slice(r0, r0 + nr)
        cols = slice(n2 * (D_MODEL // 4), (n2 + 1) * (D_MODEL // 4))
        y_s[rows, cols] = _dot(merged_s[rows, :], wout_ref[:, cols])

    eighth = ts // 8

    def out_ln(qd):
        rows = slice(qd * eighth, (qd + 1) * eighth)
        o_ref[rows, :] = _layernorm(ALPHA * hc_ref[rows, :] + y_s[rows, :], g_ref[...], b_ref[...])

    n_chunks = ts // CHUNK
    n_slots = n_chunks * HGRN_HEADS
    P = functools.partial
    COPY, PROJ, CUM, POOL, OUT, LN = 100, 256, 192, 200, 256, 150
    early = [(0, COPY, P(copy_rows, c)) for c in range(2, n_chunks)] + [(0, 10, pool_halo)]
    blocks = list(range(D_IN_PROJ // col_blk))
    forget_blocks = [j for j in blocks if j * col_blk // D_HGRN == 1]
    first = 4
    early += [(first, COPY, cast_next)]
    early += [(first, 2 * PROJ, P(in_proj, j)) for j in forget_blocks]
    early += [(first, 2 * CUM, P(cum_chunks, g)) for g in range(n_chunks // CUM_GROUP)]
    early += [(first, 2 * PROJ, P(in_proj, j)) for j in blocks if j not in forget_blocks]
    late = []
    for r0, nr in ((0, 3 * quarter), (3 * quarter, quarter)):
        first = (r0 + nr) // CHUNK * HGRN_HEADS
        late += [(first, POOL * nr // quarter, P(pool_rows, r0, nr))]
        late += [(first, OUT * nr // quarter // 2, P(out_proj, r0, nr, n2)) for n2 in range(4)]
        late += [(first, LN // 2, P(out_ln, qd)) for qd in range(r0 // eighth, (r0 + nr) // eighth)]
    total = sum(cost for _, cost, _ in early + late)
    copy_rows(0)
    copy_rows(1)
    done = 0
    ch = lambda slot: (slot // HGRN_HEADS, slot % HGRN_HEADS)
    scores, outs = {}, {}
    lead = 2
    for slot in range(n_slots + lead + 1):
        if slot < n_slots:
            scores[slot] = scores_stage(*ch(slot))
        if 0 <= slot - lead < n_slots:
            outs[slot - lead] = state_stage(*ch(slot - lead), scores.pop(slot - lead))
        if 0 <= slot - lead - 1 < n_slots:
            norm_stage(*ch(slot - lead - 1), outs.pop(slot - lead - 1))
        while done < total * (slot + 1) / n_slots:
            queue = late if late and late[0][0] <= slot - lead else early
            if not queue or queue[0][0] > slot - lead:
                break
            _, cost, fn = queue.pop(0)
            fn()
            done += cost
    for _, _, fn in early + late:
        fn()


def _mixer_ln(h2d, batch, seq, w_in, lb, gnorm, pool_w, pool_scale, w_out, g, b):
    ts = MIX_TILE
    assert seq % ts == 0 and ts % CHUNK == 0
    n_seq = seq // ts
    n_tiles = batch * n_seq
    tri, mask, near_mask = _chunk_constants()
    next_tile = pl.BlockSpec((ts, D_MODEL), lambda i: (jnp.minimum(i, n_tiles - 1), 0))
    done_tile = pl.BlockSpec((ts, D_MODEL), lambda i: (jnp.maximum(i - 1, 0), 0))
    proj_set = [pltpu.VMEM((ts, D_HGRN), F32)] * 6 + [pltpu.VMEM((ts + POOL_HALO, D_POOL), F32)]
    return pl.pallas_call(
        functools.partial(_mixer_ln_kernel, n_seq=n_seq),
        grid=(n_tiles + 1,),
        in_specs=[
            next_tile,
            done_tile,
            _resident(w_in.shape),
            _resident(lb.shape),
            _resident(gnorm.shape),
            _resident(pool_w.shape),
            _resident(pool_scale.shape),
            _resident(w_out.shape),
            _resident(g.shape),
            _resident(b.shape),
            _resident(tri.shape),
            _resident(mask.shape),
            _resident(near_mask.shape),
        ],
        out_specs=done_tile,
        out_shape=jax.ShapeDtypeStruct(h2d.shape, F32),
        scratch_shapes=proj_set + proj_set + [
            pltpu.VMEM((ts, D_MODEL), BF16),
            pltpu.VMEM((ts, D_HGRN + D_POOL), BF16),
            pltpu.VMEM((ts, D_MODEL), F32),
            pltpu.VMEM((HGRN_HEADS, HEAD_DIM, HEAD_DIM), F32),
        ],
        compiler_params=pltpu.CompilerParams(
            dimension_semantics=("arbitrary",), vmem_limit_bytes=V7X_VMEM_LIMIT),
        name="mixer_ln",
    )(h2d, h2d, w_in, lb, gnorm, pool_w, pool_scale, w_out, g, b, tri, mask, near_mask)


def _kv_proj_kernel(*refs, n_side):
    mem_ref, wk_ref, wv_ref = refs[:3]
    side_in = refs[3:3 + n_side]
    k_ref, v_ref = refs[3 + n_side:5 + n_side]
    side_out = refs[5 + n_side:5 + 2 * n_side]
    wk_b, wv_b = refs[-2:]

    @pl.when(pl.program_id(0) == 0)
    def _():
        wk_b[...] = wk_ref[...].astype(BF16)
        wv_b[...] = wv_ref[...].astype(BF16)

    mb = mem_ref[...].astype(BF16)
    k_ref[...] = _dot(mb, wk_b[...]).astype(BF16)
    v_ref[...] = _dot(mb, wv_b[...]).astype(BF16)
    for src, dst in zip(side_in, side_out):
        dst[...] = src[...].astype(BF16)


def _kv_proj(mem2d, n_mem, wk, wv, side=(), layer=0):
    rows = mem2d.shape[0]
    n_steps = rows // n_mem
    tile = pl.BlockSpec((n_mem, D_MODEL), lambda i: (i, 0))
    weight = pl.BlockSpec((None, D_MODEL, D_MODEL), lambda i: (layer, 0, 0), pipeline_mode=pl.Buffered(1))
    blocks = [_side_blocks(w, layer, n_steps) for w in side]
    out = pl.pallas_call(
        functools.partial(_kv_proj_kernel, n_side=len(side)),
        grid=(n_steps,),
        in_specs=[tile, weight, weight] + [blk[0] for blk in blocks],
        out_specs=[tile, tile] + [blk[1] for blk in blocks],
        out_shape=[jax.ShapeDtypeStruct((rows, D_MODEL), BF16)] * 2 + [blk[2] for blk in blocks],
        scratch_shapes=[pltpu.VMEM((D_MODEL, D_MODEL), BF16)] * 2,
        compiler_params=pltpu.CompilerParams(
            dimension_semantics=("arbitrary",), vmem_limit_bytes=V7X_VMEM_LIMIT),
        name="kv_proj",
    )(mem2d, wk, wv, *side)
    return out[0], out[1], out[2:]


def _xattn_ln_kernel(*refs, n_tiles, n_side):
    h_ref, k_ref, v_ref, wq_ref, wo_ref, g_ref, b_ref = refs[:7]
    side_in = refs[7:7 + n_side]
    o_ref = refs[7 + n_side]
    side_out = refs[8 + n_side:8 + 2 * n_side]
    y_s = refs[-1]
    step = pl.program_id(0)
    ts = h_ref.shape[0]
    ln_rows = ts // XA_LN_PIECES

    def ln_piece(p):
        rows = slice(p * ln_rows, (p + 1) * ln_rows)
        out = _layernorm(y_s[rows, :], g_ref[...], b_ref[...])
        o_ref[rows, :] = out
        return out

    @pl.when(step == 0)
    def _():
        y_s[...] = jnp.zeros_like(y_s)

    @pl.when(step < n_tiles)
    def _():
        h = h_ref[...]
        hb = h.astype(BF16)
        q = _dot(hb, wq_ref[...])
        pieces = list(range(XA_LN_PIECES))
        heads, raw = [], []
        for hd in range(XA_HEADS):
            cols = slice(hd * XA_HEAD_DIM, (hd + 1) * XA_HEAD_DIM)
            qh = q[:, cols].astype(BF16)
            if pieces:
                take = min(len(pieces), XA_LN_FIRST if hd == 0 else 1)
                zeros = [_zero_after(ln_piece(pieces.pop(0))) for _ in range(take)]
                zero = sum(zeros[1:], zeros[0]).astype(BF16)
                qh = qh + jnp.tile(zero, (ts // V7X_SUBLANES, XA_HEAD_DIM // V7X_LANES))
            raw.append(_dot_nt(qh, k_ref[:, cols]))
        for hd in range(XA_HEADS):
            cols = slice(hd * XA_HEAD_DIM, (hd + 1) * XA_HEAD_DIM)
            s = raw[hd] * (XA_HEAD_DIM ** -0.5)
            s = s - jnp.max(s, axis=-1, keepdims=True)
            p = jnp.exp(s)
            p = p / jnp.sum(p, axis=-1, keepdims=True)
            heads.append(_dot(p.astype(BF16), v_ref[:, cols]).astype(BF16))
        assert not pieces
        casts = list(range(n_side))
        y = None
        for hd in range(XA_HEADS):
            rows = slice(hd * XA_HEAD_DIM, (hd + 1) * XA_HEAD_DIM)
            part = _dot(heads[hd], wo_ref[rows, :])
            y = part if y is None else y + part
            if casts and hd < XA_HEADS - 1:
                n = casts.pop(0)
                v = side_in[n][...].astype(BF16)
                side_out[n][...] = v
                zero = _zero_after(v).astype(BF16)
                heads[hd + 1] = heads[hd + 1] + jnp.tile(zero, (ts // V7X_SUBLANES, XA_HEAD_DIM // V7X_LANES))
        assert not casts
        y_s[...] = ALPHA * h + y

    @pl.when(step == n_tiles)
    def _():
        for p in range(XA_LN_PIECES):
            ln_piece(p)


def _xattn_ln(h2d, batch, seq, k2d, v2d, n_mem, wq, wo, g, b, side=(), layer=0):
    ts = XA_TILE
    assert seq % ts == 0 and len(side) < XA_HEADS
    n_seq = seq // ts
    n_tiles = batch * n_seq
    attended = lambda i: jnp.minimum(i, n_tiles - 1)
    tile = pl.BlockSpec((ts, D_MODEL), lambda i: (attended(i), 0))
    mem_tile = pl.BlockSpec((n_mem, D_MODEL), lambda i: (attended(i) // n_seq, 0))
    blocks = [_side_blocks(w, layer, n_tiles) for w in side]
    out = pl.pallas_call(
        functools.partial(_xattn_ln_kernel, n_tiles=n_tiles, n_side=len(side)),
        grid=(n_tiles + 1,),
        in_specs=[tile, mem_tile, mem_tile, _resident(wq.shape), _resident(wo.shape),
                  _resident(g.shape), _resident(b.shape)] + [blk[0] for blk in blocks],
        out_specs=[pl.BlockSpec((ts, D_MODEL), lambda i: (jnp.maximum(i - 1, 0), 0))] + [blk[1] for blk in blocks],
        out_shape=[jax.ShapeDtypeStruct(h2d.shape, F32)] + [blk[2] for blk in blocks],
        scratch_shapes=[pltpu.VMEM((ts, D_MODEL), F32)],
        compiler_params=pltpu.CompilerParams(
            dimension_semantics=("arbitrary",), vmem_limit_bytes=V7X_VMEM_LIMIT),
        name="xattn_ln",
    )(h2d, k2d, v2d, wq, wo, g, b, *side)
    return out[0], out[1:]


def kernel(x, mem, w_ffn1_in, w_ffn1_out, ln1_g, ln1_b, w_mix_in, hgrn_lb, hgrn_gnorm, pool_w, pool_scale, w_mix_out, ln2_g, ln2_b, xa_wq, xa_wk, xa_wv, xa_wo, ln3_g, ln3_b, w_ffn2_in, w_ffn2_out, ln4_g, ln4_b):
    batch, seq, _ = x.shape
    n_mem = mem.shape[1]
    assert w_ffn1_in.shape[0] == DEPTH == 1
    h = x.reshape(batch * seq, D_MODEL)
    mem2d = mem.reshape(batch * n_mem, D_MODEL)
    for l in range(DEPTH):
        k2d, v2d, (ffn1_in, ffn1_out) = _kv_proj(mem2d, n_mem, xa_wk, xa_wv, side=(w_ffn1_in, w_ffn1_out), layer=l)
        later = (w_mix_in, pool_w.reshape(DEPTH, D_POOL, POOL_GROUP_DIM), w_mix_out, xa_wq, xa_wo)
        h, later = _ffn_ln(h, ffn1_in, ffn1_out, ln1_g[l:l + 1], ln1_b[l:l + 1], side=later, layer=l)
        mix_in, pool, mix_out, wq, wo = later
        pool = pool.reshape(len(POOL_WINDOWS), POOL_GROUP_DIM, POOL_GROUP_DIM)
        h = _mixer_ln(h, batch, seq, mix_in, hgrn_lb, hgrn_gnorm[l:l + 1], pool,
                      pool_scale[l:l + 1], mix_out, ln2_g[l:l + 1], ln2_b[l:l + 1])
        h, (ffn2_in, ffn2_out) = _xattn_ln(h, batch, seq, k2d, v2d, n_mem, wq, wo, ln3_g[l:l + 1], ln3_b[l:l + 1],
                                           side=(w_ffn2_in, w_ffn2_out), layer=l)
        h, _ = _ffn_ln(h, ffn2_in, ffn2_out, ln4_g[l:l + 1], ln4_b[l:l + 1])
    return h.reshape(batch, seq, D_MODEL)
```

```python
import functools

import numpy as np
import jax
import jax.numpy as jnp
from jax import lax
from jax.experimental import pallas as pl
from jax.experimental.pallas import tpu as pltpu

F32 = jnp.float32
BF16 = jnp.bfloat16

D_MODEL = 1024
DEPTH = 1
D_HGRN = 512
D_POOL = 512
HGRN_HEADS = 4
HEAD_DIM = 128
POOL_WINDOWS = (2, 4, 8, 16)
POOL_GROUP_DIM = 128
D_FF = 2816
D_IN_PROJ = 4 * D_HGRN + D_POOL
XA_HEADS = 4
XA_HEAD_DIM = 256
ALPHA = (2.0 * DEPTH) ** 0.25
LN_EPS = 1e-5
RMS_EPS = 1e-6

V7X_SUBLANES = 8
V7X_LANES = 128
V7X_MXU_DIM = 256
V7X_VMEM_LIMIT = 56 * 1024 * 1024

CHUNK = 64
SUB = V7X_SUBLANES
POOL_HALO = 16
FFN_TILE = 512
FFN_LN_PIECES = 4
MIX_TILE = 512
XA_TILE = 512
XA_LN_PIECES = 4
XA_LN_FIRST = 1
FF_CHUNKS = tuple((c0, min(512, D_FF - c0)) for c0 in range(0, D_FF, 512))


def _dot(a, b):
    return jnp.dot(a, b, preferred_element_type=F32)


def _dot_nt(a, b):
    return lax.dot_general(a, b, (((1,), (1,)), ((), ())), preferred_element_type=F32)


def _dot_tn(a, b):
    return lax.dot_general(a, b, (((0,), (0,)), ((), ())), preferred_element_type=F32)


def _silu(x):
    return x * jax.nn.sigmoid(x)


def _layernorm(y, g, b):
    mu = jnp.mean(y, axis=-1, keepdims=True)
    d = y - mu
    var = jnp.mean(d * d, axis=-1, keepdims=True)
    return d * lax.rsqrt(var + LN_EPS) * g + b


def _zero_after(v):
    sub = V7X_SUBLANES * (4 // v.dtype.itemsize)
    r, c = v.shape
    m = jnp.max(v.reshape(r // sub, sub, c), axis=0)
    m = functools.reduce(jnp.maximum, [m[:, j * V7X_LANES:(j + 1) * V7X_LANES] for j in range(c // V7X_LANES)])
    m = m.astype(F32)
    if sub > V7X_SUBLANES:
        m = jnp.maximum(m[:V7X_SUBLANES], m[V7X_SUBLANES:])
    bits = pltpu.bitcast(m, jnp.uint32)
    return pltpu.bitcast((bits >> 16) >> 16, F32)


def _resident(shape):
    zeros = (0,) * len(shape)
    return pl.BlockSpec(shape, lambda *_: zeros, pipeline_mode=pl.Buffered(1))


def _ffn_ln_kernel(*refs, n_tiles, n_side):
    x_ref, win_ref, wout_ref, g_ref, b_ref = refs[:5]
    side_in = refs[5:5 + n_side]
    o_ref = refs[5 + n_side]
    side_out = refs[6 + n_side:6 + 2 * n_side]
    y_s = refs[-1]
    step = pl.program_id(0)
    ln_rows = FFN_TILE // FFN_LN_PIECES
    gaps = len(FF_CHUNKS) - 1

    def ln_piece(p):
        rows = slice(p * ln_rows, (p + 1) * ln_rows)
        out = _layernorm(y_s[rows, :], g_ref[...], b_ref[...])
        o_ref[rows, :] = out
        return out

    def cast_side(n):
        v = side_in[n][...].astype(BF16)
        side_out[n][...] = v
        return v

    @pl.when(step == 0)
    def _():
        y_s[...] = jnp.zeros_like(y_s)

    @pl.when(step < n_tiles)
    def _():
        x = x_ref[...]
        xb = x.astype(BF16)
        acc = None
        jobs = [(FFN_TILE * D_MODEL // FFN_LN_PIECES * 3, functools.partial(ln_piece, p)) for p in range(FFN_LN_PIECES)]
        jobs += [(side_in[n].shape[0] * side_in[n].shape[1], functools.partial(cast_side, n)) for n in range(n_side)]
        bins = [[0, []] for _ in range(gaps)]
        for cost, job in sorted(jobs, key=lambda cj: -cj[0]):
            target = min(bins, key=lambda bn: bn[0])
            target[0] += cost
            target[1].append(job)
        lhs = xb
        for n, (c0, cw) in enumerate(FF_CHUNKS):
            gate = _dot(lhs, win_ref[:, c0:c0 + cw])
            up = _dot(lhs, win_ref[:, D_FF + c0:D_FF + c0 + cw])
            act = (_silu(gate) * up).astype(BF16)
            part = _dot(act, wout_ref[c0:c0 + cw, :])
            acc = part if acc is None else acc + part
            if n < gaps and bins[n][1]:
                zeros = [_zero_after(job()) for job in bins[n][1]]
                zero = sum(zeros[1:], zeros[0]).astype(BF16)
                lhs = xb + jnp.tile(zero, (FFN_TILE // V7X_SUBLANES, D_MODEL // V7X_LANES))
        y_s[...] = ALPHA * x + 0.5 * acc

    @pl.when(step == n_tiles)
    def _():
        for p in range(FFN_LN_PIECES):
            ln_piece(p)


def _side_blocks(w, layer, n_steps):
    _, r, c = w.shape
    packed_rows = 2 * V7X_SUBLANES
    rows = next(n for n in range(packed_rows, r + 1, packed_rows) if r % n == 0 and r // n <= n_steps)
    index = lambda i: (jnp.minimum(i, r // rows - 1), 0)
    return (pl.BlockSpec((None, rows, c), lambda i: (layer,) + index(i)), pl.BlockSpec((rows, c), index),
            jax.ShapeDtypeStruct((r, c), BF16))


def _ffn_ln(x2d, w_in, w_out, g, b, side=(), layer=0):
    m = x2d.shape[0]
    assert m % FFN_TILE == 0
    n_tiles = m // FFN_TILE
    current = lambda i: (jnp.minimum(i, n_tiles - 1), 0)
    blocks = [_side_blocks(w, layer, n_tiles) for w in side]
    out = pl.pallas_call(
        functools.partial(_ffn_ln_kernel, n_tiles=n_tiles, n_side=len(side)),
        grid=(n_tiles + 1,),
        in_specs=[
            pl.BlockSpec((FFN_TILE, D_MODEL), current),
            _resident((D_MODEL, 2 * D_FF)),
            _resident((D_FF, D_MODEL)),
            _resident((1, D_MODEL)),
            _resident((1, D_MODEL)),
        ] + [blk[0] for blk in blocks],
        out_specs=[pl.BlockSpec((FFN_TILE, D_MODEL), lambda i: (jnp.maximum(i - 1, 0), 0))]
        + [blk[1] for blk in blocks],
        out_shape=[jax.ShapeDtypeStruct((m, D_MODEL), F32)] + [blk[2] for blk in blocks],
        scratch_shapes=[pltpu.VMEM((FFN_TILE, D_MODEL), F32)],
        compiler_params=pltpu.CompilerParams(
            dimension_semantics=("arbitrary",), vmem_limit_bytes=V7X_VMEM_LIMIT),
        name="ffn_ln",
    )(x2d, w_in, w_out, g, b, *side)
    return out[0], out[1:]


N_BLK = CHUNK // SUB
CUM_GROUP = 4
FAR_ORDER = (7, 6, 3, 5, 4, 2, 1)
FAR_COLS = 256
NEAR_LEVELS = (0, 1, 2, 4)
assert sorted(FAR_ORDER) == list(range(1, N_BLK)) and SUB * sum(FAR_ORDER) <= FAR_COLS


def _chunk_constants():
    t = np.arange(CHUNK)[:, None]
    s = np.arange(CHUNK)[None, :]
    tri = np.kron(np.eye(CUM_GROUP), (s <= t)).astype(np.float32)
    far = np.zeros((CHUNK, FAR_COLS), np.float32)
    off = 0
    for i in FAR_ORDER:
        far[:, off:off + SUB * i] = (t // SUB == i)
        off += SUB * i
    near = np.stack([(t == s) if b == 0 else (((t // b) % 2 == 1) & ((s // b) == (t // b) - 1))
                     for b in NEAR_LEVELS]).astype(np.float32)
    return jnp.asarray(tri, BF16), jnp.asarray(far, BF16), jnp.asarray(near, F32)


def _split3(x):
    hi = x.astype(BF16)
    r = x - hi.astype(F32)
    mid = r.astype(BF16)
    lo = (r - mid.astype(F32)).astype(BF16)
    return hi, mid, lo


def _mixer_ln_kernel(hn_ref, hc_ref, win_ref, lb_ref, gn_ref, pw_ref, ps_ref, wout_ref, g_ref, b_ref,
                     tri_ref, mask_ref, near_mask_ref, o_ref,
                     q_a, k_a, f_a, i_a, cum_a, gate_a, vb_a,
                     q_s, k_s, f_s, i_s, cum_s, gate_s, vb_s, hb_s, merged_s, y_s, state_s, *, n_seq):
    ts = hn_ref.shape[0]
    step = pl.program_id(0)
    seq_tile_next = step % n_seq
    seq_tile = (step + n_seq - 1) % n_seq
    handoff = ((q_a, q_s), (k_a, k_s), (f_a, f_s), (i_a, i_s), (cum_a, cum_s), (gate_a, gate_s), (vb_a, vb_s))

    @pl.when(step == 0)
    def _():
        state_s[...] = jnp.zeros_like(state_s)
        for src, _ in handoff:
            src[...] = jnp.zeros_like(src)

    col_blk = V7X_MXU_DIM

    def copy_rows(c):
        rows = slice(c * CHUNK, (c + 1) * CHUNK)
        for src, dst in handoff[:-1]:
            dst[rows, :] = src[rows, :]
        if c == 0:
            vb_s[0:POOL_HALO, :] = vb_a[0:POOL_HALO, :]
        prow = slice(POOL_HALO + c * CHUNK, POOL_HALO + (c + 1) * CHUNK)
        vb_s[prow, :] = vb_a[prow, :]

    def pool_halo():
        vb_a[0:POOL_HALO, :] = jnp.where(seq_tile_next == 0, 0.0, vb_s[ts:ts + POOL_HALO, :])

    a = lb_ref[...]
    e = jnp.exp(a - jnp.max(a, axis=0, keepdims=True))
    lower = e[0:1, :] / jnp.sum(e, axis=0, keepdims=True)

    def cast_next():
        hb_s[...] = hn_ref[...].astype(BF16)

    def in_proj(j):
        x = _dot(hb_s[...], win_ref[:, j * col_blk:(j + 1) * col_blk])
        kind, cb = divmod(j * col_blk, D_HGRN)
        cols = slice(cb, cb + col_blk)
        if kind == 0:
            q_a[:, cols] = _silu(x)
        elif kind == 1:
            forget = lower[:, cols] + (1.0 - lower[:, cols]) * jax.nn.sigmoid(x)
            f_a[:, cols] = forget
            k_a[:, cols] = 1.0 - forget
            cum_a[:, cols] = jnp.log(forget)
        elif kind == 2:
            i_a[:, cols] = x
        elif kind == 3:
            gate_a[:, cols] = _silu(x)
        else:
            vb_a[POOL_HALO:POOL_HALO + ts, cols] = x

    def cum_chunks(g):
        rows = slice(g * CUM_GROUP * CHUNK, (g + 1) * CUM_GROUP * CHUNK)
        tri = tri_ref[...]
        hi, mid, lo = _split3(cum_a[rows, :])
        cum_a[rows, :] = _dot(tri, hi) + _dot(tri, mid) + _dot(tri, lo)

    sub_row = lax.broadcasted_iota(jnp.int32, (SUB, HEAD_DIM), 0)
    zero_blk = jnp.zeros((SUB, HEAD_DIM), F32)
    far_pad = jnp.zeros((FAR_COLS - SUB * sum(FAR_ORDER), HEAD_DIM), F32)
    gn = gn_ref[...]

    def head_cols(hd):
        return slice(hd * HEAD_DIM, (hd + 1) * HEAD_DIM)

    def scores_stage(c, hd):
        r0 = c * CHUNK
        rows = pl.ds(r0, CHUNK)
        cols = head_cols(hd)
        q = q_s[rows, cols]
        k = k_s[rows, cols]
        f = f_s[rows, cols]
        v = i_s[rows, cols]
        cum = cum_s[rows, cols]

        def cum_row(row):
            return jnp.broadcast_to(cum_s[pl.ds(r0 + row, 1), cols], (SUB, HEAD_DIM))

        bound = [zero_blk] + [cum_row(SUB * j - 1) for j in range(1, N_BLK + 1)]

        q_far = (q * jnp.exp(cum - jnp.concatenate(bound[:N_BLK], axis=0))).astype(BF16)
        k_parts, v_parts = [], []
        for i in FAR_ORDER:
            n = SUB * i
            k_parts.append(k[:n] * jnp.exp(jnp.concatenate([bound[i]] * i, axis=0) - cum[:n]))
            v_parts.append(v[:n])
        k_far = jnp.concatenate(k_parts + [far_pad], axis=0).astype(BF16)
        s_far = _dot_nt(q_far, k_far)

        ref_q2, ref_k2, ref_q4, ref_k4 = [], [], [], []
        for j in range(N_BLK):
            lo_b, hi_b = bound[j], bound[j + 1]
            m1, m3, m5 = (cum_row(SUB * j + r) for r in (1, 3, 5))
            ref_q4.append(jnp.where(sub_row < 4, lo_b, m3))
            ref_k4.append(jnp.where(sub_row < 4, m3, hi_b))
            ref_q2.append(jnp.where(sub_row < 2, lo_b, jnp.where(sub_row < 4, m1, jnp.where(sub_row < 6, m3, m5))))
            ref_k2.append(jnp.where(sub_row < 2, m1, jnp.where(sub_row < 4, m3, jnp.where(sub_row < 6, m5, hi_b))))
        cat = lambda parts: jnp.concatenate(parts, axis=0)
        q2 = q * jnp.exp(cum - cat(ref_q2))
        k2 = k * jnp.exp(cat(ref_k2) - cum)
        q4 = q * jnp.exp(cum - cat(ref_q4))
        k4 = k * jnp.exp(cat(ref_k4) - cum)
        kb = k.astype(BF16)
        s_01 = _dot_nt(jnp.concatenate([q, q * f], axis=0).astype(BF16), kb)
        groups = (s_01[:CHUNK], s_01[CHUNK:], _dot_nt(q2.astype(BF16), k2.astype(BF16)),
                  _dot_nt(q4.astype(BF16), k4.astype(BF16)))
        s_near = sum(g * near_mask_ref[n] for n, g in enumerate(groups))

        total = bound[N_BLK]
        vb = v.astype(BF16)
        return dict(
            p_far=s_far.astype(BF16) * mask_ref[...],
            v_far=jnp.concatenate(v_parts + [far_pad], axis=0).astype(BF16),
            p_near=s_near.astype(BF16),
            q_dec=(q * jnp.exp(cum)).astype(BF16),
            k_dec=(k * jnp.exp(jnp.concatenate([total] * N_BLK, axis=0) - cum)).astype(BF16),
            vb=vb,
            decay=jnp.exp(total[0:1, :]))

    def state_stage(c, hd, sc):
        st = state_s[hd]
        if c == 0:
            st = jnp.where(seq_tile == 0, 0.0, st)
        o = (_dot(sc["p_far"], sc["v_far"]) + _dot(sc["p_near"], sc["vb"])
             + _dot_nt(sc["q_dec"], st.astype(BF16)))
        state_s[hd] = st * sc["decay"] + _dot_tn(sc["vb"], sc["k_dec"])
        return o

    def norm_stage(c, hd, o):
        rows = pl.ds(c * CHUNK, CHUNK)
        cols = head_cols(hd)
        o = o * lax.rsqrt(jnp.mean(o * o, axis=-1, keepdims=True) + RMS_EPS) * gn
        merged_s[rows, cols] = (o * gate_s[rows, cols]).astype(BF16)

    quarter = ts // 4

    def pool_rows(r0, nr):
        pos = seq_tile * ts + r0 + lax.broadcasted_iota(jnp.int32, (nr, 1), 0) + 1
        for grp, win in enumerate(POOL_WINDOWS):
            cols = slice(grp * POOL_GROUP_DIM, (grp + 1) * POOL_GROUP_DIM)
            ext = vb_s[r0:r0 + POOL_HALO + nr, cols]
            wsum = ext
            span = 1
            while span < win:
                wsum = wsum + pltpu.roll(wsum, span, 0)
                span *= 2
            cur = ext[POOL_HALO:]
            count = jnp.minimum(pos, win).astype(F32)
            pooled = wsum[POOL_HALO:] / count - cur
            mixed = _dot(pooled.astype(BF16), pw_ref[grp]) * ps_ref[:, cols]
            merged_s[r0:r0 + nr, D_HGRN + grp * POOL_GROUP_DIM:D_HGRN + (grp + 1) * POOL_GROUP_DIM] = (
                mixed.astype(BF16))

    def out_proj(r0, nr, n2):
        rows = slice(r0, r0 + nr)
        cols = slice(n2 * (D_MODEL // 4), (n2 + 1) * (D_MODEL // 4))
        y_s[rows, cols] = _dot(merged_s[rows, :], wout_ref[:, cols])

    eighth = ts // 8

    def out_ln(qd):
        rows = slice(qd * eighth, (qd + 1) * eighth)
        o_ref[rows, :] = _layernorm(ALPHA * hc_ref[rows, :] + y_s[rows, :], g_ref[...], b_ref[...])

    n_chunks = ts // CHUNK
    n_slots = n_chunks * HGRN_HEADS
    P = functools.partial
    COPY, PROJ, CUM, POOL, OUT, LN = 100, 256, 192, 200, 256, 150
    early = [(0, COPY, P(copy_rows, c)) for c in range(2, n_chunks)] + [(0, 10, pool_halo)]
    blocks = list(range(D_IN_PROJ // col_blk))
    forget_blocks = [j for j in blocks if j * col_blk // D_HGRN == 1]
    first = 0
    early += [(first, COPY, cast_next)]
    early += [(first, 2 * PROJ, P(in_proj, j)) for j in forget_blocks]
    early += [(first, 2 * CUM, P(cum_chunks, g)) for g in range(n_chunks // CUM_GROUP)]
    early += [(first, 2 * PROJ, P(in_proj, j)) for j in blocks if j not in forget_blocks]
    late = []
    for r0, nr in ((0, 3 * quarter), (3 * quarter, quarter)):
        first = (r0 + nr) // CHUNK * HGRN_HEADS
        late += [(first, POOL * nr // quarter, P(pool_rows, r0, nr))]
        late += [(first, OUT * nr // quarter // 2, P(out_proj, r0, nr, n2)) for n2 in range(4)]
        late += [(first, LN // 2, P(out_ln, qd)) for qd in range(r0 // eighth, (r0 + nr) // eighth)]
    total = sum(cost for _, cost, _ in early + late)
    copy_rows(0)
    copy_rows(1)
    done = 0
    ch = lambda slot: (slot // HGRN_HEADS, slot % HGRN_HEADS)
    scores, outs = {}, {}
    lead = 2
    for slot in range(n_slots + lead + 1):
        if slot < n_slots:
            scores[slot] = scores_stage(*ch(slot))
        if 0 <= slot - lead < n_slots:
            outs[slot - lead] = state_stage(*ch(slot - lead), scores.pop(slot - lead))
        if 0 <= slot - lead - 1 < n_slots:
            norm_stage(*ch(slot - lead - 1), outs.pop(slot - lead - 1))
        while done < total * (slot + 1) / n_slots:
            queue = late if late and late[0][0] <= slot - lead else early
            if not queue or queue[0][0] > slot - lead:
                break
            _, cost, fn = queue.pop(0)
            fn()
            done += cost
    for _, _, fn in early + late:
        fn()


def _mixer_ln(h2d, batch, seq, w_in, lb, gnorm, pool_w, pool_scale, w_out, g, b):
    ts = MIX_TILE
    assert seq % ts == 0 and ts % CHUNK == 0
    n_seq = seq // ts
    n_tiles = batch * n_seq
    tri, mask, near_mask = _chunk_constants()
    next_tile = pl.BlockSpec((ts, D_MODEL), lambda i: (jnp.minimum(i, n_tiles - 1), 0))
    done_tile = pl.BlockSpec((ts, D_MODEL), lambda i: (jnp.maximum(i - 1, 0), 0))
    proj_set = [pltpu.VMEM((ts, D_HGRN), F32)] * 6 + [pltpu.VMEM((ts + POOL_HALO, D_POOL), F32)]
    return pl.pallas_call(
        functools.partial(_mixer_ln_kernel, n_seq=n_seq),
        grid=(n_tiles + 1,),
        in_specs=[
            next_tile,
            done_tile,
            _resident(w_in.shape),
            _resident(lb.shape),
            _resident(gnorm.shape),
            _resident(pool_w.shape),
            _resident(pool_scale.shape),
            _resident(w_out.shape),
            _resident(g.shape),
            _resident(b.shape),
            _resident(tri.shape),
            _resident(mask.shape),
            _resident(near_mask.shape),
        ],
        out_specs=done_tile,
        out_shape=jax.ShapeDtypeStruct(h2d.shape, F32),
        scratch_shapes=proj_set + proj_set + [
            pltpu.VMEM((ts, D_MODEL), BF16),
            pltpu.VMEM((ts, D_HGRN + D_POOL), BF16),
            pltpu.VMEM((ts, D_MODEL), F32),
            pltpu.VMEM((HGRN_HEADS, HEAD_DIM, HEAD_DIM), F32),
        ],
        compiler_params=pltpu.CompilerParams(
            dimension_semantics=("arbitrary",), vmem_limit_bytes=V7X_VMEM_LIMIT),
        name="mixer_ln",
    )(h2d, h2d, w_in, lb, gnorm, pool_w, pool_scale, w_out, g, b, tri, mask, near_mask)


def _kv_proj_kernel(*refs, n_side):
    mem_ref, wk_ref, wv_ref = refs[:3]
    side_in = refs[3:3 + n_side]
    k_ref, v_ref = refs[3 + n_side:5 + n_side]
    side_out = refs[5 + n_side:5 + 2 * n_side]
    wk_b, wv_b = refs[-2:]

    @pl.when(pl.program_id(0) == 0)
    def _():
        wk_b[...] = wk_ref[...].astype(BF16)
        wv_b[...] = wv_ref[...].astype(BF16)

    mb = mem_ref[...].astype(BF16)
    k_ref[...] = _dot(mb, wk_b[...]).astype(BF16)
    v_ref[...] = _dot(mb, wv_b[...]).astype(BF16)
    for src, dst in zip(side_in, side_out):
        dst[...] = src[...].astype(BF16)


def _kv_proj(mem2d, n_mem, wk, wv, side=(), layer=0):
    rows = mem2d.shape[0]
    n_steps = rows // n_mem
    tile = pl.BlockSpec((n_mem, D_MODEL), lambda i: (i, 0))
    weight = pl.BlockSpec((None, D_MODEL, D_MODEL), lambda i: (layer, 0, 0), pipeline_mode=pl.Buffered(1))
    blocks = [_side_blocks(w, layer, n_steps) for w in side]
    out = pl.pallas_call(
        functools.partial(_kv_proj_kernel, n_side=len(side)),
        grid=(n_steps,),
        in_specs=[tile, weight, weight] + [blk[0] for blk in blocks],
        out_specs=[tile, tile] + [blk[1] for blk in blocks],
        out_shape=[jax.ShapeDtypeStruct((rows, D_MODEL), BF16)] * 2 + [blk[2] for blk in blocks],
        scratch_shapes=[pltpu.VMEM((D_MODEL, D_MODEL), BF16)] * 2,
        compiler_params=pltpu.CompilerParams(
            dimension_semantics=("arbitrary",), vmem_limit_bytes=V7X_VMEM_LIMIT),
        name="kv_proj",
    )(mem2d, wk, wv, *side)
    return out[0], out[1], out[2:]


def _xattn_ln_kernel(*refs, n_tiles, n_side):
    h_ref, k_ref, v_ref, wq_ref, wo_ref, g_ref, b_ref = refs[:7]
    side_in = refs[7:7 + n_side]
    o_ref = refs[7 + n_side]
    side_out = refs[8 + n_side:8 + 2 * n_side]
    y_s = refs[-1]
    step = pl.program_id(0)
    ts = h_ref.shape[0]
    ln_rows = ts // XA_LN_PIECES

    def ln_piece(p):
        rows = slice(p * ln_rows, (p + 1) * ln_rows)
        out = _layernorm(y_s[rows, :], g_ref[...], b_ref[...])
        o_ref[rows, :] = out
        return out

    @pl.when(step == 0)
    def _():
        y_s[...] = jnp.zeros_like(y_s)

    @pl.when(step < n_tiles)
    def _():
        h = h_ref[...]
        hb = h.astype(BF16)
        q = _dot(hb, wq_ref[...])
        pieces = list(range(XA_LN_PIECES))
        heads, raw = [], []
        for hd in range(XA_HEADS):
            cols = slice(hd * XA_HEAD_DIM, (hd + 1) * XA_HEAD_DIM)
            qh = q[:, cols].astype(BF16)
            if pieces:
                take = min(len(pieces), XA_LN_FIRST if hd == 0 else 1)
                zeros = [_zero_after(ln_piece(pieces.pop(0))) for _ in range(take)]
                zero = sum(zeros[1:], zeros[0]).astype(BF16)
                qh = qh + jnp.tile(zero, (ts // V7X_SUBLANES, XA_HEAD_DIM // V7X_LANES))
            raw.append(_dot_nt(qh, k_ref[:, cols]))
        for hd in range(XA_HEADS):
            cols = slice(hd * XA_HEAD_DIM, (hd + 1) * XA_HEAD_DIM)
            s = raw[hd] * (XA_HEAD_DIM ** -0.5)
            s = s - jnp.max(s, axis=-1, keepdims=True)
            p = jnp.exp(s)
            p = p / jnp.sum(p, axis=-1, keepdims=True)
            heads.append(_dot(p.astype(BF16), v_ref[:, cols]).astype(BF16))
        assert not pieces
        casts = list(range(n_side))
        y = None
        for hd in range(XA_HEADS):
            rows = slice(hd * XA_HEAD_DIM, (hd + 1) * XA_HEAD_DIM)
            part = _dot(heads[hd], wo_ref[rows, :])
            y = part if y is None else y + part
            if casts and hd < XA_HEADS - 1:
                n = casts.pop(0)
                v = side_in[n][...].astype(BF16)
                side_out[n][...] = v
                zero = _zero_after(v).astype(BF16)
                heads[hd + 1] = heads[hd + 1] + jnp.tile(zero, (ts // V7X_SUBLANES, XA_HEAD_DIM // V7X_LANES))
        assert not casts
        y_s[...] = ALPHA * h + y

    @pl.when(step == n_tiles)
    def _():
        for p in range(XA_LN_PIECES):
            ln_piece(p)


def _xattn_ln(h2d, batch, seq, k2d, v2d, n_mem, wq, wo, g, b, side=(), layer=0):
    ts = XA_TILE
    assert seq % ts == 0 and len(side) < XA_HEADS
    n_seq = seq // ts
    n_tiles = batch * n_seq
    attended = lambda i: jnp.minimum(i, n_tiles - 1)
    tile = pl.BlockSpec((ts, D_MODEL), lambda i: (attended(i), 0))
    mem_tile = pl.BlockSpec((n_mem, D_MODEL), lambda i: (attended(i) // n_seq, 0))
    blocks = [_side_blocks(w, layer, n_tiles) for w in side]
    out = pl.pallas_call(
        functools.partial(_xattn_ln_kernel, n_tiles=n_tiles, n_side=len(side)),
        grid=(n_tiles + 1,),
        in_specs=[tile, mem_tile, mem_tile, _resident(wq.shape), _resident(wo.shape),
                  _resident(g.shape), _resident(b.shape)] + [blk[0] for blk in blocks],
        out_specs=[pl.BlockSpec((ts, D_MODEL), lambda i: (jnp.maximum(i - 1, 0), 0))] + [blk[1] for blk in blocks],
        out_shape=[jax.ShapeDtypeStruct(h2d.shape, F32)] + [blk[2] for blk in blocks],
        scratch_shapes=[pltpu.VMEM((ts, D_MODEL), F32)],
        compiler_params=pltpu.CompilerParams(
            dimension_semantics=("arbitrary",), vmem_limit_bytes=V7X_VMEM_LIMIT),
        name="xattn_ln",
    )(h2d, k2d, v2d, wq, wo, g, b, *side)
    return out[0], out[1:]


def kernel(x, mem, w_ffn1_in, w_ffn1_out, ln1_g, ln1_b, w_mix_in, hgrn_lb, hgrn_gnorm, pool_w, pool_scale, w_mix_out, ln2_g, ln2_b, xa_wq, xa_wk, xa_wv, xa_wo, ln3_g, ln3_b, w_ffn2_in, w_ffn2_out, ln4_g, ln4_b):
    batch, seq, _ = x.shape
    n_mem = mem.shape[1]
    assert w_ffn1_in.shape[0] == DEPTH == 1
    h = x.reshape(batch * seq, D_MODEL)
    mem2d = mem.reshape(batch * n_mem, D_MODEL)
    for l in range(DEPTH):
        k2d, v2d, (ffn1_in, ffn1_out) = _kv_proj(mem2d, n_mem, xa_wk, xa_wv, side=(w_ffn1_in, w_ffn1_out), layer=l)
        later = (w_mix_in, pool_w.reshape(DEPTH, D_POOL, POOL_GROUP_DIM), w_mix_out, xa_wq, xa_wo)
        h, later = _ffn_ln(h, ffn1_in, ffn1_out, ln1_g[l:l + 1], ln1_b[l:l + 1], side=later, layer=l)
        mix_in, pool, mix_out, wq, wo = later
        pool = pool.reshape(len(POOL_WINDOWS), POOL_GROUP_DIM, POOL_GROUP_DIM)
        h = _mixer_ln(h, batch, seq, mix_in, hgrn_lb, hgrn_gnorm[l:l + 1], pool,
                      pool_scale[l:l + 1], mix_out, ln2_g[l:l + 1], ln2_b[l:l + 1])
        h, (ffn2_in, ffn2_out) = _xattn_ln(h, batch, seq, k2d, v2d, n_mem, wq, wo, ln3_g[l:l + 1], ln3_b[l:l + 1],
                                           side=(w_ffn2_in, w_ffn2_out), layer=l)
        h, _ = _ffn_ln(h, ffn2_in, ffn2_out, ln4_g[l:l + 1], ln4_b[l:l + 1])
    return h.reshape(batch, seq, D_MODEL)
```

```python
import functools

import numpy as np
import jax
import jax.numpy as jnp
from jax import lax
from jax.experimental import pallas as pl
from jax.experimental.pallas import tpu as pltpu

F32 = jnp.float32
BF16 = jnp.bfloat16

D_MODEL = 1024
DEPTH = 1
D_HGRN = 512
D_POOL = 512
HGRN_HEADS = 4
HEAD_DIM = 128
POOL_WINDOWS = (2, 4, 8, 16)
POOL_GROUP_DIM = 128
D_FF = 2816
D_IN_PROJ = 4 * D_HGRN + D_POOL
XA_HEADS = 4
XA_HEAD_DIM = 256
ALPHA = (2.0 * DEPTH) ** 0.25
LN_EPS = 1e-5
RMS_EPS = 1e-6

V7X_SUBLANES = 8
V7X_LANES = 128
V7X_MXU_DIM = 256
V7X_VMEM_LIMIT = 56 * 1024 * 1024

CHUNK = 64
SUB = V7X_SUBLANES
POOL_HALO = 16
FFN_TILE = 512
FFN_LN_PIECES = 4
MIX_TILE = 512
XA_TILE = 512
KV_TILE = 128
KV_SIDE_BUFFERS = 2
XA_LN_PIECES = 4
XA_LN_FIRST = 1
FF_CHUNKS = tuple((c0, min(512, D_FF - c0)) for c0 in range(0, D_FF, 512))


def _dot(a, b):
    return jnp.dot(a, b, preferred_element_type=F32)


def _dot_nt(a, b):
    return lax.dot_general(a, b, (((1,), (1,)), ((), ())), preferred_element_type=F32)


def _dot_tn(a, b):
    return lax.dot_general(a, b, (((0,), (0,)), ((), ())), preferred_element_type=F32)


def _silu(x):
    return x * jax.nn.sigmoid(x)


def _layernorm(y, g, b):
    mu = jnp.mean(y, axis=-1, keepdims=True)
    d = y - mu
    var = jnp.mean(d * d, axis=-1, keepdims=True)
    return d * lax.rsqrt(var + LN_EPS) * g + b


def _zero_after(v):
    sub = V7X_SUBLANES * (4 // v.dtype.itemsize)
    r, c = v.shape
    m = jnp.max(v.reshape(r // sub, sub, c), axis=0)
    m = functools.reduce(jnp.maximum, [m[:, j * V7X_LANES:(j + 1) * V7X_LANES] for j in range(c // V7X_LANES)])
    m = m.astype(F32)
    if sub > V7X_SUBLANES:
        m = jnp.maximum(m[:V7X_SUBLANES], m[V7X_SUBLANES:])
    bits = pltpu.bitcast(m, jnp.uint32)
    return pltpu.bitcast((bits >> 16) >> 16, F32)


def _resident(shape):
    zeros = (0,) * len(shape)
    return pl.BlockSpec(shape, lambda *_: zeros, pipeline_mode=pl.Buffered(1))


def _ffn_ln_kernel(*refs, n_tiles, n_side):
    x_ref, win_ref, wout_ref, g_ref, b_ref = refs[:5]
    side_in = refs[5:5 + n_side]
    o_ref = refs[5 + n_side]
    side_out = refs[6 + n_side:6 + 2 * n_side]
    y_s = refs[-1]
    step = pl.program_id(0)
    ln_rows = FFN_TILE // FFN_LN_PIECES
    gaps = len(FF_CHUNKS) - 1

    def ln_piece(p):
        rows = slice(p * ln_rows, (p + 1) * ln_rows)
        out = _layernorm(y_s[rows, :], g_ref[...], b_ref[...])
        o_ref[rows, :] = out
        return out

    def cast_side(n):
        v = side_in[n][...].astype(BF16)
        side_out[n][...] = v
        return v

    @pl.when(step == 0)
    def _():
        y_s[...] = jnp.zeros_like(y_s)

    @pl.when(step < n_tiles)
    def _():
        x = x_ref[...]
        xb = x.astype(BF16)
        acc = None
        jobs = [(FFN_TILE * D_MODEL // FFN_LN_PIECES * 3, functools.partial(ln_piece, p)) for p in range(FFN_LN_PIECES)]
        jobs += [(side_in[n].shape[0] * side_in[n].shape[1], functools.partial(cast_side, n)) for n in range(n_side)]
        bins = [[0, []] for _ in range(gaps)]
        for cost, job in sorted(jobs, key=lambda cj: -cj[0]):
            target = min(bins, key=lambda bn: bn[0])
            target[0] += cost
            target[1].append(job)
        lhs = xb
        for n, (c0, cw) in enumerate(FF_CHUNKS):
            gate = _dot(lhs, win_ref[:, c0:c0 + cw])
            up = _dot(lhs, win_ref[:, D_FF + c0:D_FF + c0 + cw])
            act = (_silu(gate) * up).astype(BF16)
            part = _dot(act, wout_ref[c0:c0 + cw, :])
            acc = part if acc is None else acc + part
            if n < gaps and bins[n][1]:
                zeros = [_zero_after(job()) for job in bins[n][1]]
                zero = sum(zeros[1:], zeros[0]).astype(BF16)
                lhs = xb + jnp.tile(zero, (FFN_TILE // V7X_SUBLANES, D_MODEL // V7X_LANES))
        y_s[...] = ALPHA * x + 0.5 * acc

    @pl.when(step == n_tiles)
    def _():
        for p in range(FFN_LN_PIECES):
            ln_piece(p)


def _side_blocks(w, layer, n_steps, in_buffers=2):
    _, r, c = w.shape
    packed_rows = 2 * V7X_SUBLANES
    rows = next(n for n in range(packed_rows, r + 1, packed_rows) if r % n == 0 and r // n <= n_steps)
    index = lambda i: (jnp.minimum(i, r // rows - 1), 0)
    return (pl.BlockSpec((None, rows, c), lambda i: (layer,) + index(i), pipeline_mode=pl.Buffered(in_buffers)),
            pl.BlockSpec((rows, c), index), jax.ShapeDtypeStruct((r, c), BF16))


def _ffn_ln(x2d, w_in, w_out, g, b, side=(), layer=0):
    m = x2d.shape[0]
    assert m % FFN_TILE == 0
    n_tiles = m // FFN_TILE
    current = lambda i: (jnp.minimum(i, n_tiles - 1), 0)
    blocks = [_side_blocks(w, layer, n_tiles) for w in side]
    out = pl.pallas_call(
        functools.partial(_ffn_ln_kernel, n_tiles=n_tiles, n_side=len(side)),
        grid=(n_tiles + 1,),
        in_specs=[
            pl.BlockSpec((FFN_TILE, D_MODEL), current),
            _resident((D_MODEL, 2 * D_FF)),
            _resident((D_FF, D_MODEL)),
            _resident((1, D_MODEL)),
            _resident((1, D_MODEL)),
        ] + [blk[0] for blk in blocks],
        out_specs=[pl.BlockSpec((FFN_TILE, D_MODEL), lambda i: (jnp.maximum(i - 1, 0), 0))]
        + [blk[1] for blk in blocks],
        out_shape=[jax.ShapeDtypeStruct((m, D_MODEL), F32)] + [blk[2] for blk in blocks],
        scratch_shapes=[pltpu.VMEM((FFN_TILE, D_MODEL), F32)],
        compiler_params=pltpu.CompilerParams(
            dimension_semantics=("arbitrary",), vmem_limit_bytes=V7X_VMEM_LIMIT),
        name="ffn_ln",
    )(x2d, w_in, w_out, g, b, *side)
    return out[0], out[1:]


N_BLK = CHUNK // SUB
CUM_GROUP = 4
FAR_ORDER = (7, 6, 3, 5, 4, 2, 1)
FAR_COLS = 256
NEAR_LEVELS = (0, 1, 2, 4)
assert sorted(FAR_ORDER) == list(range(1, N_BLK)) and SUB * sum(FAR_ORDER) <= FAR_COLS


def _chunk_constants():
    t = np.arange(CHUNK)[:, None]
    s = np.arange(CHUNK)[None, :]
    tri = np.kron(np.eye(CUM_GROUP), (s <= t)).astype(np.float32)
    far = np.zeros((CHUNK, FAR_COLS), np.float32)
    off = 0
    for i in FAR_ORDER:
        far[:, off:off + SUB * i] = (t // SUB == i)
        off += SUB * i
    near = np.stack([(t == s) if b == 0 else (((t // b) % 2 == 1) & ((s // b) == (t // b) - 1))
                     for b in NEAR_LEVELS]).astype(np.float32)
    return jnp.asarray(tri, BF16), jnp.asarray(far, BF16), jnp.asarray(near, F32)


def _split3(x):
    hi = x.astype(BF16)
    r = x - hi.astype(F32)
    mid = r.astype(BF16)
    lo = (r - mid.astype(F32)).astype(BF16)
    return hi, mid, lo


def _mixer_ln_kernel(hn_ref, hc_ref, win_ref, lb_ref, gn_ref, pw_ref, ps_ref, wout_ref, g_ref, b_ref,
                     tri_ref, mask_ref, near_mask_ref, o_ref,
                     q_a, k_a, f_a, i_a, cum_a, gate_a, vb_a,
                     q_s, k_s, f_s, i_s, cum_s, gate_s, vb_s, hb_s, merged_s, y_s, state_s, *, n_seq):
    ts = hn_ref.shape[0]
    step = pl.program_id(0)
    seq_tile_next = step % n_seq
    seq_tile = (step + n_seq - 1) % n_seq
    handoff = ((q_a, q_s), (k_a, k_s), (f_a, f_s), (i_a, i_s), (cum_a, cum_s), (gate_a, gate_s), (vb_a, vb_s))

    @pl.when(step == 0)
    def _():
        state_s[...] = jnp.zeros_like(state_s)
        for src, _ in handoff:
            src[...] = jnp.zeros_like(src)

    col_blk = V7X_MXU_DIM

    def copy_rows(c):
        rows = slice(c * CHUNK, (c + 1) * CHUNK)
        for src, dst in handoff[:-1]:
            dst[rows, :] = src[rows, :]
        if c == 0:
            vb_s[0:POOL_HALO, :] = vb_a[0:POOL_HALO, :]
        prow = slice(POOL_HALO + c * CHUNK, POOL_HALO + (c + 1) * CHUNK)
        vb_s[prow, :] = vb_a[prow, :]

    def pool_halo():
        vb_a[0:POOL_HALO, :] = jnp.where(seq_tile_next == 0, 0.0, vb_s[ts:ts + POOL_HALO, :])

    a = lb_ref[...]
    e = jnp.exp(a - jnp.max(a, axis=0, keepdims=True))
    lower = e[0:1, :] / jnp.sum(e, axis=0, keepdims=True)

    def cast_next():
        hb_s[...] = hn_ref[...].astype(BF16)

    def in_proj(j):
        x = _dot(hb_s[...], win_ref[:, j * col_blk:(j + 1) * col_blk])
        kind, cb = divmod(j * col_blk, D_HGRN)
        cols = slice(cb, cb + col_blk)
        if kind == 0:
            q_a[:, cols] = _silu(x)
        elif kind == 1:
            forget = lower[:, cols] + (1.0 - lower[:, cols]) * jax.nn.sigmoid(x)
            f_a[:, cols] = forget
            k_a[:, cols] = 1.0 - forget
            cum_a[:, cols] = jnp.log(forget)
        elif kind == 2:
            i_a[:, cols] = x
        elif kind == 3:
            gate_a[:, cols] = _silu(x)
        else:
            vb_a[POOL_HALO:POOL_HALO + ts, cols] = x

    def cum_chunks(g):
        rows = slice(g * CUM_GROUP * CHUNK, (g + 1) * CUM_GROUP * CHUNK)
        tri = tri_ref[...]
        hi, mid, lo = _split3(cum_a[rows, :])
        cum_a[rows, :] = _dot(tri, hi) + _dot(tri, mid) + _dot(tri, lo)

    sub_row = lax.broadcasted_iota(jnp.int32, (SUB, HEAD_DIM), 0)
    zero_blk = jnp.zeros((SUB, HEAD_DIM), F32)
    far_pad = jnp.zeros((FAR_COLS - SUB * sum(FAR_ORDER), HEAD_DIM), F32)
    gn = gn_ref[...]

    def head_cols(hd):
        return slice(hd * HEAD_DIM, (hd + 1) * HEAD_DIM)

    def scores_stage(c, hd):
        r0 = c * CHUNK
        rows = pl.ds(r0, CHUNK)
        cols = head_cols(hd)
        q = q_s[rows, cols]
        k = k_s[rows, cols]
        f = f_s[rows, cols]
        v = i_s[rows, cols]
        cum = cum_s[rows, cols]

        def cum_row(row):
            return jnp.broadcast_to(cum_s[pl.ds(r0 + row, 1), cols], (SUB, HEAD_DIM))

        bound = [zero_blk] + [cum_row(SUB * j - 1) for j in range(1, N_BLK + 1)]

        q_far = (q * jnp.exp(cum - jnp.concatenate(bound[:N_BLK], axis=0))).astype(BF16)
        k_parts, v_parts = [], []
        for i in FAR_ORDER:
            n = SUB * i
            k_parts.append(k[:n] * jnp.exp(jnp.concatenate([bound[i]] * i, axis=0) - cum[:n]))
            v_parts.append(v[:n])
        k_far = jnp.concatenate(k_parts + [far_pad], axis=0).astype(BF16)
        s_far = _dot_nt(q_far, k_far)

        ref_q2, ref_k2, ref_q4, ref_k4 = [], [], [], []
        for j in range(N_BLK):
            lo_b, hi_b = bound[j], bound[j + 1]
            m1, m3, m5 = (cum_row(SUB * j + r) for r in (1, 3, 5))
            ref_q4.append(jnp.where(sub_row < 4, lo_b, m3))
            ref_k4.append(jnp.where(sub_row < 4, m3, hi_b))
            ref_q2.append(jnp.where(sub_row < 2, lo_b, jnp.where(sub_row < 4, m1, jnp.where(sub_row < 6, m3, m5))))
            ref_k2.append(jnp.where(sub_row < 2, m1, jnp.where(sub_row < 4, m3, jnp.where(sub_row < 6, m5, hi_b))))
        cat = lambda parts: jnp.concatenate(parts, axis=0)
        q2 = q * jnp.exp(cum - cat(ref_q2))
        k2 = k * jnp.exp(cat(ref_k2) - cum)
        q4 = q * jnp.exp(cum - cat(ref_q4))
        k4 = k * jnp.exp(cat(ref_k4) - cum)
        kb = k.astype(BF16)
        s_01 = _dot_nt(jnp.concatenate([q, q * f], axis=0).astype(BF16), kb)
        groups = (s_01[:CHUNK], s_01[CHUNK:], _dot_nt(q2.astype(BF16), k2.astype(BF16)),
                  _dot_nt(q4.astype(BF16), k4.astype(BF16)))
        s_near = sum(g * near_mask_ref[n] for n, g in enumerate(groups))

        total = bound[N_BLK]
        vb = v.astype(BF16)
        return dict(
            p_far=s_far.astype(BF16) * mask_ref[...],
            v_far=jnp.concatenate(v_parts + [far_pad], axis=0).astype(BF16),
            p_near=s_near.astype(BF16),
            q_dec=(q * jnp.exp(cum)).astype(BF16),
            k_dec=(k * jnp.exp(jnp.concatenate([total] * N_BLK, axis=0) - cum)).astype(BF16),
            vb=vb,
            decay=jnp.exp(total[0:1, :]))

    def state_stage(c, hd, sc):
        st = state_s[hd]
        if c == 0:
            st = jnp.where(seq_tile == 0, 0.0, st)
        o = (_dot(sc["p_far"], sc["v_far"]) + _dot(sc["p_near"], sc["vb"])
             + _dot_nt(sc["q_dec"], st.astype(BF16)))
        state_s[hd] = st * sc["decay"] + _dot_tn(sc["vb"], sc["k_dec"])
        return o

    def norm_stage(c, hd, o):
        rows = pl.ds(c * CHUNK, CHUNK)
        cols = head_cols(hd)
        o = o * lax.rsqrt(jnp.mean(o * o, axis=-1, keepdims=True) + RMS_EPS) * gn
        merged_s[rows, cols] = (o * gate_s[rows, cols]).astype(BF16)

    quarter = ts // 4

    def pool_rows(r0, nr):
        pos = seq_tile * ts + r0 + lax.broadcasted_iota(jnp.int32, (nr, 1), 0) + 1
        for grp, win in enumerate(POOL_WINDOWS):
            cols = slice(grp * POOL_GROUP_DIM, (grp + 1) * POOL_GROUP_DIM)
            ext = vb_s[r0:r0 + POOL_HALO + nr, cols]
            wsum = ext
            span = 1
            while span < win:
                wsum = wsum + pltpu.roll(wsum, span, 0)
                span *= 2
            cur = ext[POOL_HALO:]
            count = jnp.minimum(pos, win).astype(F32)
            pooled = wsum[POOL_HALO:] / count - cur
            mixed = _dot(pooled.astype(BF16), pw_ref[grp]) * ps_ref[:, cols]
            merged_s[r0:r0 + nr, D_HGRN + grp * POOL_GROUP_DIM:D_HGRN + (grp + 1) * POOL_GROUP_DIM] = (
                mixed.astype(BF16))

    def out_proj(r0, nr, n2):
        rows = slice(r0, r0 + nr)
        cols = slice(n2 * (D_MODEL // 4), (n2 + 1) * (D_MODEL // 4))
        y_s[rows, cols] = _dot(merged_s[rows, :], wout_ref[:, cols])

    eighth = ts // 8

    def out_ln(qd):
        rows = slice(qd * eighth, (qd + 1) * eighth)
        o_ref[rows, :] = _layernorm(ALPHA * hc_ref[rows, :] + y_s[rows, :], g_ref[...], b_ref[...])

    n_chunks = ts // CHUNK
    n_slots = n_chunks * HGRN_HEADS
    P = functools.partial
    COPY, PROJ, CUM, POOL, OUT, LN = 100, 256, 192, 200, 256, 150
    early = [(0, COPY, P(copy_rows, c)) for c in range(2, n_chunks)] + [(0, 10, pool_halo)]
    blocks = list(range(D_IN_PROJ // col_blk))
    forget_blocks = [j for j in blocks if j * col_blk // D_HGRN == 1]
    first = 0
    early += [(first, COPY, cast_next)]
    early += [(first, 3 * PROJ, P(in_proj, j)) for j in forget_blocks]
    early += [(first, 2 * CUM, P(cum_chunks, g)) for g in range(n_chunks // CUM_GROUP)]
    early += [(first, 3 * PROJ, P(in_proj, j)) for j in blocks if j not in forget_blocks]
    late = []
    for r0, nr in ((0, 3 * quarter), (3 * quarter, quarter)):
        first = (r0 + nr) // CHUNK * HGRN_HEADS
        late += [(first, POOL * nr // quarter, P(pool_rows, r0, nr))]
        late += [(first, OUT * nr // quarter // 2, P(out_proj, r0, nr, n2)) for n2 in range(4)]
        late += [(first, LN // 2, P(out_ln, qd)) for qd in range(r0 // eighth, (r0 + nr) // eighth)]
    total = sum(cost for _, cost, _ in early + late)
    copy_rows(0)
    copy_rows(1)
    done = 0
    ch = lambda slot: (slot // HGRN_HEADS, slot % HGRN_HEADS)
    scores, outs = {}, {}
    lead = 2
    for slot in range(n_slots + lead + 1):
        if slot < n_slots:
            scores[slot] = scores_stage(*ch(slot))
        if 0 <= slot - lead < n_slots:
            outs[slot - lead] = state_stage(*ch(slot - lead), scores.pop(slot - lead))
        if 0 <= slot - lead - 1 < n_slots:
            norm_stage(*ch(slot - lead - 1), outs.pop(slot - lead - 1))
        while done < total * slot / n_slots:
            queue = late if late and late[0][0] <= slot - lead else early
            if not queue or queue[0][0] > slot - lead:
                break
            _, cost, fn = queue.pop(0)
            fn()
            done += cost
    for _, _, fn in early + late:
        fn()


def _mixer_ln(h2d, batch, seq, w_in, lb, gnorm, pool_w, pool_scale, w_out, g, b):
    ts = MIX_TILE
    assert seq % ts == 0 and ts % CHUNK == 0
    n_seq = seq // ts
    n_tiles = batch * n_seq
    tri, mask, near_mask = _chunk_constants()
    next_tile = pl.BlockSpec((ts, D_MODEL), lambda i: (jnp.minimum(i, n_tiles - 1), 0))
    done_tile = pl.BlockSpec((ts, D_MODEL), lambda i: (jnp.maximum(i - 1, 0), 0))
    proj_set = [pltpu.VMEM((ts, D_HGRN), F32)] * 6 + [pltpu.VMEM((ts + POOL_HALO, D_POOL), F32)]
    return pl.pallas_call(
        functools.partial(_mixer_ln_kernel, n_seq=n_seq),
        grid=(n_tiles + 1,),
        in_specs=[
            next_tile,
            done_tile,
            _resident(w_in.shape),
            _resident(lb.shape),
            _resident(gnorm.shape),
            _resident(pool_w.shape),
            _resident(pool_scale.shape),
            _resident(w_out.shape),
            _resident(g.shape),
            _resident(b.shape),
            _resident(tri.shape),
            _resident(mask.shape),
            _resident(near_mask.shape),
        ],
        out_specs=done_tile,
        out_shape=jax.ShapeDtypeStruct(h2d.shape, F32),
        scratch_shapes=proj_set + proj_set + [
            pltpu.VMEM((ts, D_MODEL), BF16),
            pltpu.VMEM((ts, D_HGRN + D_POOL), BF16),
            pltpu.VMEM((ts, D_MODEL), F32),
            pltpu.VMEM((HGRN_HEADS, HEAD_DIM, HEAD_DIM), F32),
        ],
        compiler_params=pltpu.CompilerParams(
            dimension_semantics=("arbitrary",), vmem_limit_bytes=V7X_VMEM_LIMIT),
        name="mixer_ln",
    )(h2d, h2d, w_in, lb, gnorm, pool_w, pool_scale, w_out, g, b, tri, mask, near_mask)


def _kv_proj_kernel(*refs, n_side):
    mem_ref, wk_ref, wv_ref = refs[:3]
    side_in = refs[3:3 + n_side]
    k_ref, v_ref = refs[3 + n_side:5 + n_side]
    side_out = refs[5 + n_side:5 + 2 * n_side]
    wk_b, wv_b = refs[-2:]

    @pl.when(pl.program_id(0) == 0)
    def _():
        wk_b[...] = wk_ref[...].astype(BF16)
        wv_b[...] = wv_ref[...].astype(BF16)

    mb = mem_ref[...].astype(BF16)
    k_ref[...] = _dot(mb, wk_b[...]).astype(BF16)
    v_ref[...] = _dot(mb, wv_b[...]).astype(BF16)
    for src, dst in zip(side_in, side_out):
        dst[...] = src[...].astype(BF16)


def _kv_proj(mem2d, n_mem, wk, wv, side=(), layer=0):
    rows = mem2d.shape[0]
    assert rows % KV_TILE == 0
    n_steps = rows // KV_TILE
    tile = pl.BlockSpec((KV_TILE, D_MODEL), lambda i: (i, 0))
    weight = pl.BlockSpec((None, D_MODEL, D_MODEL), lambda i: (layer, 0, 0), pipeline_mode=pl.Buffered(1))
    blocks = [_side_blocks(w, layer, n_steps, in_buffers=KV_SIDE_BUFFERS) for w in side]
    out = pl.pallas_call(
        functools.partial(_kv_proj_kernel, n_side=len(side)),
        grid=(n_steps,),
        in_specs=[tile, weight, weight] + [blk[0] for blk in blocks],
        out_specs=[tile, tile] + [blk[1] for blk in blocks],
        out_shape=[jax.ShapeDtypeStruct((rows, D_MODEL), BF16)] * 2 + [blk[2] for blk in blocks],
        scratch_shapes=[pltpu.VMEM((D_MODEL, D_MODEL), BF16)] * 2,
        compiler_params=pltpu.CompilerParams(
            dimension_semantics=("arbitrary",), vmem_limit_bytes=V7X_VMEM_LIMIT),
        name="kv_proj",
    )(mem2d, wk, wv, *side)
    return out[0], out[1], out[2:]


def _xattn_ln_kernel(*refs, n_tiles, n_side):
    h_ref, k_ref, v_ref, wq_ref, wo_ref, g_ref, b_ref = refs[:7]
    side_in = refs[7:7 + n_side]
    o_ref = refs[7 + n_side]
    side_out = refs[8 + n_side:8 + 2 * n_side]
    y_s = refs[-1]
    step = pl.program_id(0)
    ts = h_ref.shape[0]
    ln_rows = ts // XA_LN_PIECES

    def ln_piece(p):
        rows = slice(p * ln_rows, (p + 1) * ln_rows)
        out = _layernorm(y_s[rows, :], g_ref[...], b_ref[...])
        o_ref[rows, :] = out
        return out

    @pl.when(step == 0)
    def _():
        y_s[...] = jnp.zeros_like(y_s)

    @pl.when(step < n_tiles)
    def _():
        h = h_ref[...]
        hb = h.astype(BF16)
        q = _dot(hb, wq_ref[...])
        pieces = list(range(XA_LN_PIECES))
        heads, raw = [], []
        for hd in range(XA_HEADS):
            cols = slice(hd * XA_HEAD_DIM, (hd + 1) * XA_HEAD_DIM)
            qh = q[:, cols].astype(BF16)
            if pieces:
                take = min(len(pieces), XA_LN_FIRST if hd == 0 else 1)
                zeros = [_zero_after(ln_piece(pieces.pop(0))) for _ in range(take)]
                zero = sum(zeros[1:], zeros[0]).astype(BF16)
                qh = qh + jnp.tile(zero, (ts // V7X_SUBLANES, XA_HEAD_DIM // V7X_LANES))
            raw.append(_dot_nt(qh, k_ref[:, cols]))
        for hd in range(XA_HEADS):
            cols = slice(hd * XA_HEAD_DIM, (hd + 1) * XA_HEAD_DIM)
            s = raw[hd] * (XA_HEAD_DIM ** -0.5)
            s = s - jnp.max(s, axis=-1, keepdims=True)
            p = jnp.exp(s)
            p = p / jnp.sum(p, axis=-1, keepdims=True)
            heads.append(_dot(p.astype(BF16), v_ref[:, cols]).astype(BF16))
        assert not pieces
        casts = list(range(n_side))
        y = None
        for hd in range(XA_HEADS):
            rows = slice(hd * XA_HEAD_DIM, (hd + 1) * XA_HEAD_DIM)
            part = _dot(heads[hd], wo_ref[rows, :])
            y = part if y is None else y + part
            if casts and hd < XA_HEADS - 1:
                n = casts.pop(0)
                v = side_in[n][...].astype(BF16)
                side_out[n][...] = v
                zero = _zero_after(v).astype(BF16)
                heads[hd + 1] = heads[hd + 1] + jnp.tile(zero, (ts // V7X_SUBLANES, XA_HEAD_DIM // V7X_LANES))
        assert not casts
        y_s[...] = ALPHA * h + y

    @pl.when(step == n_tiles)
    def _():
        for p in range(XA_LN_PIECES):
            ln_piece(p)


def _xattn_ln(h2d, batch, seq, k2d, v2d, n_mem, wq, wo, g, b, side=(), layer=0):
    ts = XA_TILE
    assert seq % ts == 0 and len(side) < XA_HEADS
    n_seq = seq // ts
    n_tiles = batch * n_seq
    attended = lambda i: jnp.minimum(i, n_tiles - 1)
    tile = pl.BlockSpec((ts, D_MODEL), lambda i: (attended(i), 0))
    mem_tile = pl.BlockSpec((n_mem, D_MODEL), lambda i: (attended(i) // n_seq, 0))
    blocks = [_side_blocks(w, layer, n_tiles) for w in side]
    out = pl.pallas_call(
        functools.partial(_xattn_ln_kernel, n_tiles=n_tiles, n_side=len(side)),
        grid=(n_tiles + 1,),
        in_specs=[tile, mem_tile, mem_tile, _resident(wq.shape), _resident(wo.shape),
                  _resident(g.shape), _resident(b.shape)] + [blk[0] for blk in blocks],
        out_specs=[pl.BlockSpec((ts, D_MODEL), lambda i: (jnp.maximum(i - 1, 0), 0))] + [blk[1] for blk in blocks],
        out_shape=[jax.ShapeDtypeStruct(h2d.shape, F32)] + [blk[2] for blk in blocks],
        scratch_shapes=[pltpu.VMEM((ts, D_MODEL), F32)],
        compiler_params=pltpu.CompilerParams(
            dimension_semantics=("arbitrary",), vmem_limit_bytes=V7X_VMEM_LIMIT),
        name="xattn_ln",
    )(h2d, k2d, v2d, wq, wo, g, b, *side)
    return out[0], out[1:]


def kernel(x, mem, w_ffn1_in, w_ffn1_out, ln1_g, ln1_b, w_mix_in, hgrn_lb, hgrn_gnorm, pool_w, pool_scale, w_mix_out, ln2_g, ln2_b, xa_wq, xa_wk, xa_wv, xa_wo, ln3_g, ln3_b, w_ffn2_in, w_ffn2_out, ln4_g, ln4_b):
    batch, seq, _ = x.shape
    n_mem = mem.shape[1]
    assert w_ffn1_in.shape[0] == DEPTH == 1
    h = x.reshape(batch * seq, D_MODEL)
    mem2d = mem.reshape(batch * n_mem, D_MODEL)
    for l in range(DEPTH):
        k2d, v2d, (ffn1_in, ffn1_out) = _kv_proj(mem2d, n_mem, xa_wk, xa_wv, side=(w_ffn1_in, w_ffn1_out), layer=l)
        later = (w_mix_in, pool_w.reshape(DEPTH, D_POOL, POOL_GROUP_DIM), w_mix_out, xa_wq, xa_wo)
        h, later = _ffn_ln(h, ffn1_in, ffn1_out, ln1_g[l:l + 1], ln1_b[l:l + 1], side=later, layer=l)
        mix_in, pool, mix_out, wq, wo = later
        pool = pool.reshape(len(POOL_WINDOWS), POOL_GROUP_DIM, POOL_GROUP_DIM)
        h = _mixer_ln(h, batch, seq, mix_in, hgrn_lb, hgrn_gnorm[l:l + 1], pool,
                      pool_scale[l:l + 1], mix_out, ln2_g[l:l + 1], ln2_b[l:l + 1])
        h, (ffn2_in, ffn2_out) = _xattn_ln(h, batch, seq, k2d, v2d, n_mem, wq, wo, ln3_g[l:l + 1], ln3_b[l:l + 1],
                                           side=(w_ffn2_in, w_ffn2_out), layer=l)
        h, _ = _ffn_ln(h, ffn2_in, ffn2_out, ln4_g[l:l + 1], ln4_b[l:l + 1])
    return h.reshape(batch, seq, D_MODEL)
```
